```python
import jax, jax.numpy as jnp
from jax import lax
import numpy as np

D_MODEL = 4096
BATCH = 2
SEQ = 4096
DEPTH = 2

GRID_W = 64
CTX_LEN = 256
N_MIXERS = 2
HEAD_DIM = 128
NA_HEADS = D_MODEL // HEAD_DIM
NA_WIN_H = 8
NA_WIN_W = 16
GQA_Q_HEADS = D_MODEL // HEAD_DIM
GQA_KV_HEADS = max(1, GQA_Q_HEADS // 4)
GQA_GROUP = GQA_Q_HEADS // GQA_KV_HEADS
Q_BLOCK = 128
D_FF = ((8 * D_MODEL + 3 * 256 - 1) // (3 * 256)) * 256
N_MOD = 6
ROPE_THETA = 10000.0
NORM_EPS = 1e-6
NEG_INF = -1e30

kernel_name = 'hybrid_natten_gqa_dit_block'


def rms_norm(x, g):
    xf = x.astype(jnp.float32)
    y = xf * lax.rsqrt(jnp.mean(xf * xf, axis=-1, keepdims=True) + NORM_EPS)
    return (y * g.astype(jnp.float32)).astype(x.dtype)


def modulate(h, shift, scale):
    return h * (1 + scale) + shift


def rope_1d(x, pos):
    half = x.shape[-1] // 2
    freqs = ROPE_THETA ** (-jnp.arange(half, dtype=jnp.float32) / half)
    ang = pos.astype(jnp.float32)[:, None] * freqs[None, :]
    cos = jnp.cos(ang)[None, :, None, :]
    sin = jnp.sin(ang)[None, :, None, :]
    xf = x.astype(jnp.float32)
    x1, x2 = xf[..., :half], xf[..., half:]
    return jnp.concatenate([x1 * cos - x2 * sin, x2 * cos + x1 * sin], axis=-1).astype(x.dtype)


def axial_rope(x, row, col):
    h = x.shape[-1] // 2
    return jnp.concatenate([rope_1d(x[..., :h], row), rope_1d(x[..., h:], col)], axis=-1)


def swiglu(h, w13, w2):
    a, b = jnp.split(h @ w13, 2, axis=-1)
    return (jax.nn.silu(a) * b) @ w2


def neighborhood_attention(h, hc, wqkv, wo, rpb, with_ctx_out):
    B, S, D = h.shape
    L = hc.shape[1]
    rows = S // GRID_W
    wh = min(NA_WIN_H, rows)
    ww = NA_WIN_W
    H, dh = NA_HEADS, HEAD_DIM
    scale = dh ** -0.5
    qkv = (h @ wqkv).reshape(B, rows, GRID_W, 3, H, dh)
    q = qkv[..., 0, :, :] * scale
    k = qkv[..., 1, :, :]
    v = qkv[..., 2, :, :]
    kv_c = (hc @ wqkv[:, D:]).reshape(B, L, 2, H, dh)
    kc, vc = kv_c[:, :, 0], kv_c[:, :, 1]

    col = jnp.arange(GRID_W)
    cs = jnp.clip(col - ww // 2, 0, GRID_W - ww)
    in_win = (col[None, :] >= cs[:, None]) & (col[None, :] < cs[:, None] + ww)
    dc_idx = jnp.clip(col[None, :] - col[:, None], -(ww - 1), ww - 1) + ww - 1
    bias_cols = jnp.transpose(rpb[:, :, dc_idx], (0, 2, 1, 3)).astype(jnp.float32)
    bias_cols = jnp.where(in_win[None, :, None, :], bias_cols, NEG_INF)

    def row_block(r):
        rs = jnp.clip(r - wh // 2, 0, rows - wh)
        qr = lax.dynamic_index_in_dim(q, r, axis=1, keepdims=False)
        kb = lax.dynamic_slice_in_dim(k, rs, wh, axis=1)
        vb = lax.dynamic_slice_in_dim(v, rs, wh, axis=1)
        dr_idx = rs + jnp.arange(wh) - r + NA_WIN_H - 1
        bias = jnp.take(bias_cols, dr_idx, axis=2)
        s_loc = jnp.einsum('bqhd,brkhd->bhqrk', qr, kb, preferred_element_type=jnp.float32) + bias[None]
        s_ctx = jnp.einsum('bqhd,bchd->bhqc', qr, kc, preferred_element_type=jnp.float32)
        s = jnp.concatenate([s_loc.reshape(B, H, GRID_W, wh * GRID_W), s_ctx], axis=-1)
        p = jax.nn.softmax(s, axis=-1).astype(v.dtype)
        p_loc = p[..., :wh * GRID_W].reshape(B, H, GRID_W, wh, GRID_W)
        p_ctx = p[..., wh * GRID_W:]
        return (jnp.einsum('bhqrk,brkhd->bqhd', p_loc, vb)
                + jnp.einsum('bhqc,bchd->bqhd', p_ctx, vc))

    o = lax.map(row_block, jnp.arange(rows))
    y = jnp.transpose(o, (1, 0, 2, 3, 4)).reshape(B, S, H * dh) @ wo
    if not with_ctx_out:
        return y, None
    qc = (hc @ wqkv[:, :D]).reshape(B, L, H, dh) * scale
    s = jnp.einsum('bqhd,bkhd->bhqk', qc, kc, preferred_element_type=jnp.float32)
    p = jax.nn.softmax(s, axis=-1).astype(vc.dtype)
    yc = jnp.einsum('bhqk,bkhd->bqhd', p, vc).reshape(B, L, H * dh) @ wo
    return y, yc


def gqa_axial_attention(h, hc, wq, wkv, q_norm, k_norm, wo, with_ctx_out):
    B, S, D = h.shape
    L = hc.shape[1]
    Hq, Hkv, G, dh = GQA_Q_HEADS, GQA_KV_HEADS, GQA_GROUP, HEAD_DIM
    scale = dh ** -0.5
    t = jnp.arange(S)
    row, col = t // GRID_W, t % GRID_W
    q = axial_rope(rms_norm((h @ wq).reshape(B, S, Hq, dh), q_norm), row, col) * scale
    kv = (h @ wkv).reshape(B, S, 2, Hkv, dh)
    k = axial_rope(rms_norm(kv[:, :, 0], k_norm), row, col)
    v = kv[:, :, 1]
    kv_c = (hc @ wkv).reshape(B, L, 2, Hkv, dh)
    kc = rms_norm(kv_c[:, :, 0], k_norm)
    vc = kv_c[:, :, 1]

    nb = S // Q_BLOCK
    qb = jnp.transpose(q.reshape(B, nb, Q_BLOCK, Hkv, G, dh), (1, 0, 2, 3, 4, 5))

    def block(qi):
        s_lat = jnp.einsum('bqkgd,bskd->bkgqs', qi, k, preferred_element_type=jnp.float32)
        s_ctx = jnp.einsum('bqkgd,bckd->bkgqc', qi, kc, preferred_element_type=jnp.float32)
        p = jax.nn.softmax(jnp.concatenate([s_lat, s_ctx], axis=-1), axis=-1).astype(v.dtype)
        return (jnp.einsum('bkgqs,bskd->bqkgd', p[..., :S], v)
                + jnp.einsum('bkgqc,bckd->bqkgd', p[..., S:], vc))

    o = lax.map(block, qb)
    y = jnp.transpose(o, (1, 0, 2, 3, 4, 5)).reshape(B, S, Hq * dh) @ wo
    if not with_ctx_out:
        return y, None
    qc = (rms_norm((hc @ wq).reshape(B, L, Hq, dh), q_norm) * scale).reshape(B, L, Hkv, G, dh)
    s = jnp.einsum('bqkgd,bckd->bkgqc', qc, kc, preferred_element_type=jnp.float32)
    p = jax.nn.softmax(s, axis=-1).astype(vc.dtype)
    yc = jnp.einsum('bkgqc,bckd->bqkgd', p, vc).reshape(B, L, Hq * dh) @ wo
    return y, yc


def _dense(key, shape, fan_in, gain=1.0):
    return jax.random.normal(key, shape, jnp.float32) * (gain * fan_in ** -0.5)


def setup_inputs(seed: int = 0) -> dict:
    key = jax.random.key(seed)
    ks = jax.random.split(key, 20)
    D, F = D_MODEL, D_FF
    n_na = (DEPTH + N_MIXERS - 1) // N_MIXERS
    n_gqa = DEPTH // N_MIXERS
    kv_w = 2 * GQA_KV_HEADS * HEAD_DIM
    return {
        'x': jax.random.normal(ks[0], (BATCH, SEQ, D), jnp.float32),
        'c': jax.random.normal(ks[1], (BATCH, D), jnp.float32),
        'ctx': jax.random.normal(ks[2], (BATCH, CTX_LEN, D), jnp.float32),
        'c_ctx': jax.random.normal(ks[3], (D,), jnp.float32),
        'ada_w': _dense(ks[4], (DEPTH, D, N_MOD * D), D, 0.5),
        'ada_b': 0.01 * jax.random.normal(ks[5], (DEPTH, N_MOD * D), jnp.float32),
        'norm_g': 1.0 + 0.05 * jax.random.normal(ks[6], (DEPTH, 4, D), jnp.float32),
        'na_wqkv': _dense(ks[7], (n_na, D, 3 * NA_HEADS * HEAD_DIM), D),
        'na_wo': _dense(ks[8], (n_na, NA_HEADS * HEAD_DIM, D), NA_HEADS * HEAD_DIM),
        'na_rpb': 0.1 * jax.random.normal(ks[9], (n_na, NA_HEADS, 2 * NA_WIN_H - 1, 2 * NA_WIN_W - 1), jnp.float32),
        'gqa_wq': _dense(ks[10], (n_gqa, D, GQA_Q_HEADS * HEAD_DIM), D),
        'gqa_wkv': _dense(ks[11], (n_gqa, D, kv_w), D),
        'gqa_q_norm': 1.0 + 0.05 * jax.random.normal(ks[12], (n_gqa, HEAD_DIM), jnp.float32),
        'gqa_k_norm': 1.0 + 0.05 * jax.random.normal(ks[13], (n_gqa, HEAD_DIM), jnp.float32),
        'gqa_wo': _dense(ks[14], (n_gqa, GQA_Q_HEADS * HEAD_DIM, D), GQA_Q_HEADS * HEAD_DIM),
        'ffn_w13': _dense(ks[15], (DEPTH, D, 2 * F), D),
        'ffn_w2': _dense(ks[16], (DEPTH, F, D), F),
    }


def reference(x, c, ctx, c_ctx, ada_w, ada_b, norm_g, na_wqkv, na_wo, na_rpb,
              gqa_wq, gqa_wkv, gqa_q_norm, gqa_k_norm, gqa_wo, ffn_w13, ffn_w2):
    xc = ctx
    silu_c = jax.nn.silu(c)
    silu_cc = jax.nn.silu(c_ctx)
    for i in range(DEPTH):
        last = i == DEPTH - 1
        mod = silu_c @ ada_w[i] + ada_b[i]
        sh_a, sc_a, gt_a, sh_f, sc_f, gt_f = jnp.split(mod[:, None, :], N_MOD, axis=-1)
        mod_c = silu_cc @ ada_w[i] + ada_b[i]
        csh_a, csc_a, cgt_a, csh_f, csc_f, cgt_f = jnp.split(mod_c, N_MOD, axis=-1)
        g_pre_a, g_post_a, g_pre_f, g_post_f = norm_g[i]

        h = modulate(rms_norm(x, g_pre_a), sh_a, sc_a)
        hc = modulate(rms_norm(xc, g_pre_a), csh_a, csc_a)
        j = i // N_MIXERS
        if i % N_MIXERS == 0:
            y, yc = neighborhood_attention(h, hc, na_wqkv[j], na_wo[j], na_rpb[j], not last)
        else:
            y, yc = gqa_axial_attention(h, hc, gqa_wq[j], gqa_wkv[j], gqa_q_norm[j],
                                        gqa_k_norm[j], gqa_wo[j], not last)
        x = x + gt_a * rms_norm(y, g_post_a)
        h = modulate(rms_norm(x, g_pre_f), sh_f, sc_f)
        x = x + gt_f * rms_norm(swiglu(h, ffn_w13[i], ffn_w2[i]), g_post_f)

        if not last:
            xc = xc + cgt_a * rms_norm(yc, g_post_a)
            hc = modulate(rms_norm(xc, g_pre_f), csh_f, csc_f)
            xc = xc + cgt_f * rms_norm(swiglu(hc, ffn_w13[i], ffn_w2[i]), g_post_f)
    return x
```

```python
import functools

import jax
import jax.numpy as jnp
from jax import lax
from jax.experimental import pallas as pl
from jax.experimental.pallas import tpu as pltpu

GRID_W = 64
NA_WIN_H = 8
NA_WIN_W = 16
HEAD_DIM = 128
GQA_GROUP = 4
ROPE_THETA = 10000.0
NORM_EPS = 1e-6
NEG_INF = -1e30
N_MOD = 6

VMEM_LIMIT_BYTES = 56 * 1024 * 1024
MOD_ROWS = 8

F32 = jnp.float32
BF16 = jnp.bfloat16


def _params(*sem):
    return pltpu.CompilerParams(dimension_semantics=sem, vmem_limit_bytes=VMEM_LIMIT_BYTES)


def _rms(x, g):
    ms = jnp.mean(x * x, axis=-1, keepdims=True)
    return x * lax.rsqrt(ms + NORM_EPS) * g


def _seg_index(rows_per_seg_tiles, n_batch):
    return lambda i: jnp.minimum(i // rows_per_seg_tiles, n_batch)


def _ada_kernel(c_ref, w_ref, b_ref, o_ref):
    c = c_ref[...]
    s = (c * jax.nn.sigmoid(c)).astype(BF16)
    o_ref[...] = jnp.dot(s, w_ref[...].astype(BF16), preferred_element_type=F32) + b_ref[...]


def _ada_mod(cvec, ada_w, ada_b, tn=1024):
    depth, d, n = ada_w.shape
    return pl.pallas_call(
        _ada_kernel,
        grid=(depth, n // tn),
        in_specs=[
            pl.BlockSpec((MOD_ROWS, d), lambda l, j: (0, 0)),
            pl.BlockSpec((None, d, tn), lambda l, j: (l, 0, j)),
            pl.BlockSpec((None, 1, tn), lambda l, j: (l, 0, j)),
        ],
        out_specs=pl.BlockSpec((None, MOD_ROWS, tn), lambda l, j: (l, 0, j)),
        out_shape=jax.ShapeDtypeStruct((depth, MOD_ROWS, n), F32),
        compiler_params=_params("arbitrary", "arbitrary"),
        name="ada_mod",
    )(cvec, ada_w, ada_b.reshape(depth, 1, n))


def _prenorm_kernel(x_ref, g_ref, mod_ref, h_ref, *, g_row, sh_row, sc_row):
    y = _rms(x_ref[...], g_ref[g_row:g_row + 1, :])
    h = y * (1.0 + mod_ref[sc_row:sc_row + 1, :]) + mod_ref[sh_row:sh_row + 1, :]
    h_ref[...] = h.astype(h_ref.dtype)


def _prenorm(x, g, mod, *, seg_tiles, n_batch, g_row, sh_row, sc_row, tm=256):
    m, d = x.shape
    seg = _seg_index(seg_tiles(tm), n_batch)
    return pl.pallas_call(
        functools.partial(_prenorm_kernel, g_row=g_row, sh_row=sh_row, sc_row=sc_row),
        grid=(m // tm,),
        in_specs=[
            pl.BlockSpec((tm, d), lambda i: (i, 0)),
            pl.BlockSpec(g.shape, lambda i: (0, 0)),
            pl.BlockSpec((None, N_MOD, d), lambda i: (seg(i), 0, 0)),
        ],
        out_specs=pl.BlockSpec((tm, d), lambda i: (i, 0)),
        out_shape=jax.ShapeDtypeStruct((m, d), BF16),
        compiler_params=_params("arbitrary"),
        name="prenorm",
    )(x, g, mod)


def _resid_kernel(y_ref, x_ref, g_ref, mod_ref, *rest, gt_row, gpost_row, nxt):
    xn = x_ref[...] + mod_ref[gt_row:gt_row + 1, :] * _rms(y_ref[...], g_ref[gpost_row:gpost_row + 1, :])
    if nxt is None:
        (xo_ref,) = rest
        xo_ref[...] = xn
        return
    g2_ref, mod2_ref, xo_ref, h_ref = rest
    gpre_row, sh_row, sc_row = nxt
    xo_ref[...] = xn
    h = _rms(xn, g2_ref[gpre_row:gpre_row + 1, :])
    h = h * (1.0 + mod2_ref[sc_row:sc_row + 1, :]) + mod2_ref[sh_row:sh_row + 1, :]
    h_ref[...] = h.astype(h_ref.dtype)


def _resid(y, x, g, mod, *, rows, seg_tiles, n_batch, gt_row, gpost_row, nxt=None, g2=None, mod2=None, tm=256):
    d = x.shape[1]
    seg = _seg_index(seg_tiles(tm), n_batch)
    row_spec = pl.BlockSpec((tm, d), lambda i: (i, 0))
    mod_spec = pl.BlockSpec((None, N_MOD, d), lambda i: (seg(i), 0, 0))
    in_specs = [row_spec, row_spec, pl.BlockSpec(g.shape, lambda i: (0, 0)), mod_spec]
    args = [y, x, g, mod]
    out_specs = [row_spec]
    out_shape = [jax.ShapeDtypeStruct((rows, d), F32)]
    if nxt is not None:
        in_specs += [pl.BlockSpec(g2.shape, lambda i: (0, 0)), mod_spec]
        args += [g2, mod2]
        out_specs.append(row_spec)
        out_shape.append(jax.ShapeDtypeStruct((rows, d), BF16))
    out = pl.pallas_call(
        functools.partial(_resid_kernel, gt_row=gt_row, gpost_row=gpost_row, nxt=nxt),
        grid=(rows // tm,),
        in_specs=in_specs,
        out_specs=out_specs,
        out_shape=out_shape,
        compiler_params=_params("arbitrary"),
        name="resid_norm",
    )(*args)
    return out if nxt is not None else out[0]


def _mm_kernel(x_ref, w_ref, o_ref, *, scale_blocks, scale):
    acc = jnp.dot(x_ref[...], w_ref[...], preferred_element_type=F32)
    if scale_blocks:
        acc = acc * jnp.where(pl.program_id(1) < scale_blocks, scale, 1.0)
    o_ref[...] = acc.astype(o_ref.dtype)


def _matmul(x, w, *, rows, n_out, out_dtype, tm, tn, col_blk_off=0, scale_blocks=0, scale=1.0, name="matmul"):
    k = x.shape[1]
    return pl.pallas_call(
        functools.partial(_mm_kernel, scale_blocks=scale_blocks, scale=scale),
        grid=(rows // tm, n_out // tn),
        in_specs=[
            pl.BlockSpec((tm, k), lambda i, j: (i, 0)),
            pl.BlockSpec((k, tn), lambda i, j: (0, j + col_blk_off)),
        ],
        out_specs=pl.BlockSpec((tm, tn), lambda i, j: (i, j)),
        out_shape=jax.ShapeDtypeStruct((rows, n_out), out_dtype),
        compiler_params=_params("arbitrary", "arbitrary"),
        name=name,
    )(x, w)


def _swap_halves(y):
    lane = lax.broadcasted_iota(jnp.int32, y.shape, 1)
    return jnp.where((lane & 32) == 0, pltpu.roll(y, 96, 1), pltpu.roll(y, 32, 1))


def _mm_rope_kernel(x_ref, w_ref, g_ref, cos_ref, sin_ref, o_ref, *, scale):
    acc = jnp.dot(x_ref[...], w_ref[...], preferred_element_type=F32)
    cos = cos_ref[...]
    sin = sin_ref[...]
    g = g_ref[...]
    for hh in range(acc.shape[1] // HEAD_DIM):
        cols = slice(hh * HEAD_DIM, (hh + 1) * HEAD_DIM)
        y = _rms(acc[:, cols], g)
        y = y * cos + _swap_halves(y) * sin
        if scale != 1.0:
            y = y * scale
        o_ref[:, cols] = y.astype(o_ref.dtype)


def _matmul_rope(x, w, g, cos, sin, *, rows, n_out, tm, tn, scale, name):
    k = x.shape[1]
    tab_spec = pl.BlockSpec((tm, HEAD_DIM), lambda i, j: (i, 0))
    return pl.pallas_call(
        functools.partial(_mm_rope_kernel, scale=scale),
        grid=(rows // tm, n_out // tn),
        in_specs=[
            pl.BlockSpec((tm, k), lambda i, j: (i, 0)),
            pl.BlockSpec((k, tn), lambda i, j: (0, j)),
            pl.BlockSpec((1, HEAD_DIM), lambda i, j: (0, 0)),
            tab_spec,
            tab_spec,
        ],
        out_specs=pl.BlockSpec((tm, tn), lambda i, j: (i, j)),
        out_shape=jax.ShapeDtypeStruct((rows, n_out), BF16),
        compiler_params=_params("arbitrary", "arbitrary"),
        name=name,
    )(x, w, g.reshape(1, HEAD_DIM), cos, sin)


def _gateup_kernel(x_ref, w1_ref, w3_ref, o_ref):
    x = x_ref[...]
    a = jnp.dot(x, w1_ref[...], preferred_element_type=F32)
    b = jnp.dot(x, w3_ref[...], preferred_element_type=F32)
    o_ref[...] = (a * jax.nn.sigmoid(a) * b).astype(o_ref.dtype)


def _gateup(x, w13, *, rows, tm, tn):
    k = x.shape[1]
    f = w13.shape[1] // 2
    return pl.pallas_call(
        _gateup_kernel,
        grid=(rows // tm, f // tn),
        in_specs=[
            pl.BlockSpec((tm, k), lambda i, j: (i, 0)),
            pl.BlockSpec((k, tn), lambda i, j: (0, j)),
            pl.BlockSpec((k, tn), lambda i, j: (0, j + f // tn)),
        ],
        out_specs=pl.BlockSpec((tm, tn), lambda i, j: (i, j)),
        out_shape=jax.ShapeDtypeStruct((rows, f), BF16),
        compiler_params=_params("arbitrary", "arbitrary"),
        name="ffn_gateup",
    )(x, w13, w13)


_NT = (((1,), (1,)), ((), ()))


def _na_kernel(q_ref, k_ref, v_ref, kc_ref, vc_ref, bias_ref, o_ref, *, rb, hb, rows):
    r0 = pl.program_id(2) * rb
    win = NA_WIN_H * GRID_W
    for hh in range(hb):
        cols = slice(hh * HEAD_DIM, (hh + 1) * HEAD_DIM)
        kc = kc_ref[:, cols]
        vc = vc_ref[:, cols]
        for i in range(rb):
            r = r0 + i
            rs = jnp.clip(r - NA_WIN_H // 2, 0, rows - NA_WIN_H)
            off = rs - r + NA_WIN_H - 1
            start = pl.multiple_of(rs * GRID_W, GRID_W)
            q = q_ref[i * GRID_W:(i + 1) * GRID_W, cols]
            kw = k_ref[pl.ds(start, win), cols]
            vw = v_ref[pl.ds(start, win), cols]
            s_loc = lax.dot_general(q, kw, _NT, preferred_element_type=F32) + bias_ref[hh, off]
            s_ctx = lax.dot_general(q, kc, _NT, preferred_element_type=F32)
            m = jnp.maximum(jnp.max(s_loc, axis=-1, keepdims=True), jnp.max(s_ctx, axis=-1, keepdims=True))
            p_loc = jnp.exp(s_loc - m)
            p_ctx = jnp.exp(s_ctx - m)
            l = jnp.sum(p_loc, axis=-1, keepdims=True) + jnp.sum(p_ctx, axis=-1, keepdims=True)
            o = (jnp.dot(p_loc.astype(BF16), vw, preferred_element_type=F32)
                 + jnp.dot(p_ctx.astype(BF16), vc, preferred_element_type=F32))
            o_ref[i * GRID_W:(i + 1) * GRID_W, cols] = (o / l).astype(o_ref.dtype)


def _na_attention(qkv, bias8, *, n_batch, seq, ctx_len, n_heads, rb=8, hb=2):
    m_all = qkv.shape[0]
    rows = seq // GRID_W
    hw = hb * HEAD_DIM
    hblocks = n_heads // hb
    qrows = rb * GRID_W
    lat_spec = lambda part: pl.BlockSpec((seq, hw), lambda b, h, r: (b, part * hblocks + h))
    ctx_spec = lambda part: pl.BlockSpec((ctx_len, hw), lambda b, h, r: (n_batch * seq // ctx_len + b, part * hblocks + h))
    return pl.pallas_call(
        functools.partial(_na_kernel, rb=rb, hb=hb, rows=rows),
        grid=(n_batch, hblocks, rows // rb),
        in_specs=[
            pl.BlockSpec((qrows, hw), lambda b, h, r: (b * (seq // qrows) + r, h)),
            lat_spec(1),
            lat_spec(2),
            ctx_spec(1),
            ctx_spec(2),
            pl.BlockSpec((hb, NA_WIN_H, GRID_W, NA_WIN_H * GRID_W), lambda b, h, r: (h, 0, 0, 0)),
        ],
        out_specs=pl.BlockSpec((qrows, hw), lambda b, h, r: (b * (seq // qrows) + r, h)),
        out_shape=jax.ShapeDtypeStruct((m_all, n_heads * HEAD_DIM), BF16),
        compiler_params=_params("arbitrary", "arbitrary", "arbitrary"),
        name="na_attention",
    )(qkv, qkv, qkv, qkv, qkv, bias8)


def _ctx_attn_kernel(q_ref, k_ref, v_ref, o_in_ref, o_ref, *, hb):
    del o_in_ref
    for hh in range(hb):
        cols = slice(hh * HEAD_DIM, (hh + 1) * HEAD_DIM)
        s = lax.dot_general(q_ref[:, cols], k_ref[:, cols], _NT, preferred_element_type=F32)
        p = jnp.exp(s - jnp.max(s, axis=-1, keepdims=True))
        l = jnp.sum(p, axis=-1, keepdims=True)
        o = jnp.dot(p.astype(BF16), v_ref[:, cols], preferred_element_type=F32)
        o_ref[:, cols] = (o / l).astype(o_ref.dtype)


def _ctx_attention(qkv, o, *, n_batch, seq, ctx_len, n_heads, hb=2):
    hw = hb * HEAD_DIM
    hblocks = n_heads // hb
    row0 = n_batch * seq // ctx_len
    spec = lambda part: pl.BlockSpec((ctx_len, hw), lambda b, h: (row0 + b, part * hblocks + h))
    return pl.pallas_call(
        functools.partial(_ctx_attn_kernel, hb=hb),
        grid=(n_batch, hblocks),
        in_specs=[spec(0), spec(1), spec(2), pl.BlockSpec(memory_space=pl.ANY)],
        out_specs=spec(0),
        out_shape=jax.ShapeDtypeStruct(o.shape, o.dtype),
        input_output_aliases={3: 0},
        compiler_params=_params("arbitrary", "arbitrary"),
        name="ctx_attention",
    )(qkv, qkv, qkv, o)


def _gqa_kernel(q_ref, k_ref, v_ref, kc_ref, vc_ref, o_ref, *, tk):
    tq = q_ref.shape[0]
    seq = k_ref.shape[0]

    def step(q, k, v, carry):
        m, l, acc = carry
        s = lax.dot_general(q, k, _NT, preferred_element_type=F32)
        m_new = jnp.maximum(m, jnp.max(s, axis=-1, keepdims=True))
        alpha = jnp.exp(m - m_new)
        p = jnp.exp(s - m_new)
        l = alpha * l + jnp.sum(p, axis=-1, keepdims=True)
        acc = alpha * acc + jnp.dot(p.astype(BF16), v, preferred_element_type=F32)
        return m_new, l, acc

    for g in range(GQA_GROUP):
        cols = slice(g * HEAD_DIM, (g + 1) * HEAD_DIM)
        q = q_ref[:, cols]

        def body(c, carry, q=q):
            st = pl.multiple_of(c * tk, tk)
            return step(q, k_ref[pl.ds(st, tk), :], v_ref[pl.ds(st, tk), :], carry)

        init = (jnp.full((tq, 1), NEG_INF, F32), jnp.zeros((tq, 1), F32), jnp.zeros((tq, HEAD_DIM), F32))
        carry = lax.fori_loop(0, seq // tk, body, init)
        _, l, acc = step(q, kc_ref[...], vc_ref[...], carry)
        o_ref[:, cols] = (acc / l).astype(o_ref.dtype)


def _gqa_attention(q, k, v, *, n_batch, seq, ctx_len, tq=512, tk=512):
    n_kv = k.shape[1] // HEAD_DIM
    gw = GQA_GROUP * HEAD_DIM
    lat_spec = pl.BlockSpec((seq, HEAD_DIM), lambda b, h, i: (b, h))
    ctx_spec = pl.BlockSpec((ctx_len, HEAD_DIM), lambda b, h, i: (n_batch * seq // ctx_len + b, h))
    q_spec = pl.BlockSpec((tq, gw), lambda b, h, i: (b * (seq // tq) + i, h))
    return pl.pallas_call(
        functools.partial(_gqa_kernel, tk=tk),
        grid=(n_batch, n_kv, seq // tq),
        in_specs=[q_spec, lat_spec, lat_spec, ctx_spec, ctx_spec],
        out_specs=q_spec,
        out_shape=jax.ShapeDtypeStruct(q.shape, BF16),
        compiler_params=_params("arbitrary", "arbitrary", "arbitrary"),
        name="gqa_attention",
    )(q, k, v, k, v)


def _na_bias_table(rpb):
    ww = NA_WIN_W
    col = jnp.arange(GRID_W)
    cs = jnp.clip(col - ww // 2, 0, GRID_W - ww)
    in_win = (col[None, :] >= cs[:, None]) & (col[None, :] < cs[:, None] + ww)
    dc_idx = jnp.clip(col[None, :] - col[:, None], -(ww - 1), ww - 1) + ww - 1
    bias_cols = jnp.transpose(rpb[:, :, dc_idx], (0, 2, 1, 3)).astype(F32)
    bias_cols = jnp.where(in_win[None, :, None, :], bias_cols, NEG_INF)
    per_off = [bias_cols[:, :, off:off + NA_WIN_H, :] for off in range(NA_WIN_H)]
    tab = jnp.stack(per_off, axis=1)
    return tab.reshape(rpb.shape[0], NA_WIN_H, GRID_W, NA_WIN_H * GRID_W)


def _rope_tables(n_batch, seq, ctx_len):
    quarter = HEAD_DIM // 4
    t = jnp.arange(seq)
    freqs = ROPE_THETA ** (-jnp.arange(quarter, dtype=F32) / quarter)
    ang_r = (t // GRID_W).astype(F32)[:, None] * freqs[None, :]
    ang_c = (t % GRID_W).astype(F32)[:, None] * freqs[None, :]
    cos = jnp.concatenate([jnp.cos(ang_r)] * 2 + [jnp.cos(ang_c)] * 2, axis=-1)
    sin = jnp.concatenate([-jnp.sin(ang_r), jnp.sin(ang_r), -jnp.sin(ang_c), jnp.sin(ang_c)], axis=-1)
    n_ctx = n_batch * ctx_len
    cos = jnp.concatenate([jnp.tile(cos, (n_batch, 1)), jnp.ones((n_ctx, HEAD_DIM), F32)], axis=0)
    sin = jnp.concatenate([jnp.tile(sin, (n_batch, 1)), jnp.zeros((n_ctx, HEAD_DIM), F32)], axis=0)
    return cos, sin


def kernel(x, c, ctx, c_ctx, ada_w, ada_b, norm_g, na_wqkv, na_wo, na_rpb, gqa_wq, gqa_wkv, gqa_q_norm,
           gqa_k_norm, gqa_wo, ffn_w13, ffn_w2):
    n_batch, seq, d = x.shape
    ctx_len = ctx.shape[1]
    depth = ada_w.shape[0]
    assert depth == 2 and na_wqkv.shape[0] == 1 and gqa_wq.shape[0] == 1
    assert seq % GRID_W == 0 and n_batch + 1 <= MOD_ROWS
    n_heads = d // HEAD_DIM
    m_lat = n_batch * seq
    m_all = m_lat + n_batch * ctx_len
    scale = HEAD_DIM ** -0.5
    tm = 512
    seg_tiles = lambda t: seq // t

    cvec = jnp.zeros((MOD_ROWS, d), F32).at[:n_batch].set(c).at[n_batch].set(c_ctx)
    mod = _ada_mod(cvec, ada_w, ada_b).reshape(depth, MOD_ROWS, N_MOD, d)
    bias8 = _na_bias_table(na_rpb[0])
    cos, sin = _rope_tables(n_batch, seq, ctx_len)

    wqkv = na_wqkv[0].astype(BF16)
    na_wo_b = na_wo[0].astype(BF16)
    wq = gqa_wq[0].astype(BF16)
    wkv = gqa_wkv[0].astype(BF16)
    gqa_wo_b = gqa_wo[0].astype(BF16)
    w13 = [ffn_w13[i].astype(BF16) for i in range(depth)]
    w2 = [ffn_w2[i].astype(BF16) for i in range(depth)]

    xa = jnp.concatenate([x.reshape(m_lat, d), ctx.reshape(n_batch * ctx_len, d)], axis=0)
    seg_kw = dict(seg_tiles=seg_tiles, n_batch=n_batch)

    h = _prenorm(xa, norm_g[0], mod[0], g_row=0, sh_row=0, sc_row=1, **seg_kw)
    qkv = _matmul(h, wqkv, rows=m_all, n_out=wqkv.shape[1], out_dtype=BF16, tm=tm, tn=1024,
                  scale_blocks=d // 1024, scale=scale, name="na_qkv")
    o = _na_attention(qkv, bias8, n_batch=n_batch, seq=seq, ctx_len=ctx_len, n_heads=n_heads)
    o = _ctx_attention(qkv, o, n_batch=n_batch, seq=seq, ctx_len=ctx_len, n_heads=n_heads)
    y = _matmul(o, na_wo_b, rows=m_all, n_out=d, out_dtype=F32, tm=tm, tn=1024, name="na_wo")
    xa, h = _resid(y, xa, norm_g[0], mod[0], rows=m_all, gt_row=2, gpost_row=1,
                   nxt=(2, 3, 4), g2=norm_g[0], mod2=mod[0], **seg_kw)
    gu = _gateup(h, w13[0], rows=m_all, tm=tm, tn=256)
    y = _matmul(gu, w2[0], rows=m_all, n_out=d, out_dtype=F32, tm=tm, tn=256, name="ffn_down")
    xa, h = _resid(y, xa, norm_g[0], mod[0], rows=m_all, gt_row=5, gpost_row=3,
                   nxt=(0, 0, 1), g2=norm_g[1], mod2=mod[1], **seg_kw)

    kv_w = wkv.shape[1] // 2
    q = _matmul_rope(h, wq, gqa_q_norm[0], cos, sin, rows=m_lat, n_out=d, tm=tm, tn=1024, scale=scale, name="gqa_q")
    k = _matmul_rope(h, wkv, gqa_k_norm[0], cos, sin, rows=m_all, n_out=kv_w, tm=tm, tn=kv_w, scale=1.0, name="gqa_k")
    v = _matmul(h, wkv, rows=m_all, n_out=kv_w, out_dtype=BF16, tm=tm, tn=kv_w, col_blk_off=1, name="gqa_v")
    o = _gqa_attention(q, k, v, n_batch=n_batch, seq=seq, ctx_len=ctx_len)
    y = _matmul(o, gqa_wo_b, rows=m_lat, n_out=d, out_dtype=F32, tm=tm, tn=1024, name="gqa_wo")
    xl, h = _resid(y, xa, norm_g[1], mod[1], rows=m_lat, gt_row=2, gpost_row=1,
                   nxt=(2, 3, 4), g2=norm_g[1], mod2=mod[1], **seg_kw)
    gu = _gateup(h, w13[1], rows=m_lat, tm=tm, tn=256)
    y = _matmul(gu, w2[1], rows=m_lat, n_out=d, out_dtype=F32, tm=tm, tn=256, name="ffn_down")
    xl = _resid(y, xl, norm_g[1], mod[1], rows=m_lat, gt_row=5, gpost_row=3, **seg_kw)
    return xl.reshape(n_batch, seq, d)
```

```python
import functools

import jax
import jax.numpy as jnp
from jax import lax
from jax.experimental import pallas as pl
from jax.experimental.pallas import tpu as pltpu

GRID_W = 64
NA_WIN_H = 8
NA_WIN_W = 16
HEAD_DIM = 128
GQA_GROUP = 4
ROPE_THETA = 10000.0
NORM_EPS = 1e-6
NEG_INF = -1e30
LOG2E = 1.4426950408889634
N_MOD = 6

VMEM_LIMIT_BYTES = 56 * 1024 * 1024
MOD_ROWS = 8

F32 = jnp.float32
BF16 = jnp.bfloat16


def _params(*sem):
    return pltpu.CompilerParams(dimension_semantics=sem, vmem_limit_bytes=VMEM_LIMIT_BYTES)


def _rms(x, g):
    ms = jnp.mean(x * x, axis=-1, keepdims=True)
    return x * lax.rsqrt(ms + NORM_EPS) * g


def _seg_index(rows_per_seg_tiles, n_batch):
    return lambda i: jnp.minimum(i // rows_per_seg_tiles, n_batch)


def _ada_kernel(c_ref, w_ref, b_ref, o_ref):
    c = c_ref[...]
    s = (c * jax.nn.sigmoid(c)).astype(BF16)
    o_ref[...] = jnp.dot(s, w_ref[...].astype(BF16), preferred_element_type=F32) + b_ref[...]


def _ada_mod(cvec, ada_w, ada_b, tn=1024):
    depth, d, n = ada_w.shape
    return pl.pallas_call(
        _ada_kernel,
        grid=(depth, n // tn),
        in_specs=[
            pl.BlockSpec((MOD_ROWS, d), lambda l, j: (0, 0)),
            pl.BlockSpec((None, d, tn), lambda l, j: (l, 0, j)),
            pl.BlockSpec((None, 1, tn), lambda l, j: (l, 0, j)),
        ],
        out_specs=pl.BlockSpec((None, MOD_ROWS, tn), lambda l, j: (l, 0, j)),
        out_shape=jax.ShapeDtypeStruct((depth, MOD_ROWS, n), F32),
        compiler_params=_params("arbitrary", "arbitrary"),
        name="ada_mod",
    )(cvec, ada_w, ada_b.reshape(depth, 1, n))


def _prenorm_kernel(x_ref, g_ref, mod_ref, h_ref, *, g_row, sh_row, sc_row):
    y = _rms(x_ref[...], g_ref[g_row:g_row + 1, :])
    h = y * (1.0 + mod_ref[sc_row:sc_row + 1, :]) + mod_ref[sh_row:sh_row + 1, :]
    h_ref[...] = h.astype(h_ref.dtype)


def _prenorm(x, g, mod, *, seg_tiles, n_batch, g_row, sh_row, sc_row, tm=256):
    m, d = x.shape
    seg = _seg_index(seg_tiles(tm), n_batch)
    return pl.pallas_call(
        functools.partial(_prenorm_kernel, g_row=g_row, sh_row=sh_row, sc_row=sc_row),
        grid=(m // tm,),
        in_specs=[
            pl.BlockSpec((tm, d), lambda i: (i, 0)),
            pl.BlockSpec(g.shape, lambda i: (0, 0)),
            pl.BlockSpec((None, N_MOD, d), lambda i: (seg(i), 0, 0)),
        ],
        out_specs=pl.BlockSpec((tm, d), lambda i: (i, 0)),
        out_shape=jax.ShapeDtypeStruct((m, d), BF16),
        compiler_params=_params("arbitrary"),
        name="prenorm",
    )(x, g, mod)


def _resid_kernel(y_ref, x_ref, g_ref, mod_ref, *rest, gt_row, gpost_row, nxt):
    xn = x_ref[...] + mod_ref[gt_row:gt_row + 1, :] * _rms(y_ref[...], g_ref[gpost_row:gpost_row + 1, :])
    if nxt is None:
        (xo_ref,) = rest
        xo_ref[...] = xn
        return
    g2_ref, mod2_ref, xo_ref, h_ref = rest
    gpre_row, sh_row, sc_row = nxt
    xo_ref[...] = xn
    h = _rms(xn, g2_ref[gpre_row:gpre_row + 1, :])
    h = h * (1.0 + mod2_ref[sc_row:sc_row + 1, :]) + mod2_ref[sh_row:sh_row + 1, :]
    h_ref[...] = h.astype(h_ref.dtype)


def _resid(y, x, g, mod, *, rows, seg_tiles, n_batch, gt_row, gpost_row, nxt=None, g2=None, mod2=None, tm=256):
    d = x.shape[1]
    seg = _seg_index(seg_tiles(tm), n_batch)
    row_spec = pl.BlockSpec((tm, d), lambda i: (i, 0))
    mod_spec = pl.BlockSpec((None, N_MOD, d), lambda i: (seg(i), 0, 0))
    in_specs = [row_spec, row_spec, pl.BlockSpec(g.shape, lambda i: (0, 0)), mod_spec]
    args = [y, x, g, mod]
    out_specs = [row_spec]
    out_shape = [jax.ShapeDtypeStruct((rows, d), F32)]
    if nxt is not None:
        in_specs += [pl.BlockSpec(g2.shape, lambda i: (0, 0)), mod_spec]
        args += [g2, mod2]
        out_specs.append(row_spec)
        out_shape.append(jax.ShapeDtypeStruct((rows, d), BF16))
    out = pl.pallas_call(
        functools.partial(_resid_kernel, gt_row=gt_row, gpost_row=gpost_row, nxt=nxt),
        grid=(rows // tm,),
        in_specs=in_specs,
        out_specs=out_specs,
        out_shape=out_shape,
        compiler_params=_params("arbitrary"),
        name="resid_norm",
    )(*args)
    return out if nxt is not None else out[0]


def _mm_kernel(x_ref, w_ref, o_ref, *, scale_blocks, scale):
    acc = jnp.dot(x_ref[...], w_ref[...], preferred_element_type=F32)
    if scale_blocks:
        acc = acc * jnp.where(pl.program_id(1) < scale_blocks, scale, 1.0)
    o_ref[...] = acc.astype(o_ref.dtype)


def _matmul(x, w, *, rows, n_out, out_dtype, tm, tn, col_blk_off=0, scale_blocks=0, scale=1.0, name="matmul"):
    k = x.shape[1]
    return pl.pallas_call(
        functools.partial(_mm_kernel, scale_blocks=scale_blocks, scale=scale),
        grid=(rows // tm, n_out // tn),
        in_specs=[
            pl.BlockSpec((tm, k), lambda i, j: (i, 0)),
            pl.BlockSpec((k, tn), lambda i, j: (0, j + col_blk_off)),
        ],
        out_specs=pl.BlockSpec((tm, tn), lambda i, j: (i, j)),
        out_shape=jax.ShapeDtypeStruct((rows, n_out), out_dtype),
        compiler_params=_params("arbitrary", "arbitrary"),
        name=name,
    )(x, w)


def _mm_t_kernel(x_ref, w_ref, o_ref):
    acc = jnp.dot(x_ref[...], w_ref[...], preferred_element_type=F32)
    o_ref[...] = acc.T.astype(o_ref.dtype)


def _matmul_t(x, w, *, rows, n_out, tm, tn, col_blk_off=0, name="matmul_t"):
    k = x.shape[1]
    return pl.pallas_call(
        _mm_t_kernel,
        grid=(rows // tm, n_out // tn),
        in_specs=[
            pl.BlockSpec((tm, k), lambda i, j: (i, 0)),
            pl.BlockSpec((k, tn), lambda i, j: (0, j + col_blk_off)),
        ],
        out_specs=pl.BlockSpec((None, tn, tm), lambda i, j: (i, j, 0)),
        out_shape=jax.ShapeDtypeStruct((rows // tm, n_out, tm), BF16),
        compiler_params=_params("arbitrary", "arbitrary"),
        name=name,
    )(x, w)


def _swap_halves(y):
    lane = lax.broadcasted_iota(jnp.int32, y.shape, 1)
    return jnp.where((lane & 32) == 0, pltpu.roll(y, 96, 1), pltpu.roll(y, 32, 1))


def _mm_rope_kernel(x_ref, w_ref, g_ref, cos_ref, sin_ref, o_ref, *, scale):
    acc = jnp.dot(x_ref[...], w_ref[...], preferred_element_type=F32)
    cos = cos_ref[...]
    sin = sin_ref[...]
    g = g_ref[...]
    for hh in range(acc.shape[1] // HEAD_DIM):
        cols = slice(hh * HEAD_DIM, (hh + 1) * HEAD_DIM)
        y = _rms(acc[:, cols], g)
        y = y * cos + _swap_halves(y) * sin
        if scale != 1.0:
            y = y * scale
        o_ref[:, cols] = y.astype(o_ref.dtype)


def _matmul_rope(x, w, g, cos, sin, *, rows, n_out, tm, tn, scale, name):
    k = x.shape[1]
    tab_spec = pl.BlockSpec((tm, HEAD_DIM), lambda i, j: (i, 0))
    return pl.pallas_call(
        functools.partial(_mm_rope_kernel, scale=scale),
        grid=(rows // tm, n_out // tn),
        in_specs=[
            pl.BlockSpec((tm, k), lambda i, j: (i, 0)),
            pl.BlockSpec((k, tn), lambda i, j: (0, j)),
            pl.BlockSpec((1, HEAD_DIM), lambda i, j: (0, 0)),
            tab_spec,
            tab_spec,
        ],
        out_specs=pl.BlockSpec((tm, tn), lambda i, j: (i, j)),
        out_shape=jax.ShapeDtypeStruct((rows, n_out), BF16),
        compiler_params=_params("arbitrary", "arbitrary"),
        name=name,
    )(x, w, g.reshape(1, HEAD_DIM), cos, sin)


def _gateup_kernel(x_ref, w1_ref, w3_ref, o_ref):
    x = x_ref[...]
    a = jnp.dot(x, w1_ref[...], preferred_element_type=F32)
    b = jnp.dot(x, w3_ref[...], preferred_element_type=F32)
    o_ref[...] = (a * jax.nn.sigmoid(a) * b).astype(o_ref.dtype)


def _gateup(x, w13, *, rows, tm, tn):
    k = x.shape[1]
    f = w13.shape[1] // 2
    return pl.pallas_call(
        _gateup_kernel,
        grid=(rows // tm, f // tn),
        in_specs=[
            pl.BlockSpec((tm, k), lambda i, j: (i, 0)),
            pl.BlockSpec((k, tn), lambda i, j: (0, j)),
            pl.BlockSpec((k, tn), lambda i, j: (0, j + f // tn)),
        ],
        out_specs=pl.BlockSpec((tm, tn), lambda i, j: (i, j)),
        out_shape=jax.ShapeDtypeStruct((rows, f), BF16),
        compiler_params=_params("arbitrary", "arbitrary"),
        name="ffn_gateup",
    )(x, w13, w13)


_NT = (((1,), (1,)), ((), ()))


def _na_block_rows(rb):
    return rb + NA_WIN_H


def _na_kernel(q_ref, k_ref, v_ref, kc_ref, vc_ref, bias_ref, o_ref, *, rb, hb, nsb, rows):
    kr = _na_block_rows(rb)
    qn = rb * GRID_W
    n_blocks = rows // rb
    for sb in range(nsb):
        blk = pl.program_id(2) * nsb + sb
        cls = jnp.where(blk == 0, 0, jnp.where(blk == n_blocks - 1, 2, 1))
        kstart = jnp.clip(blk * rb - NA_WIN_H // 2, 0, rows - kr)
        start = pl.multiple_of(kstart * GRID_W, GRID_W)
        for hh in range(hb):
            cols = slice(hh * HEAD_DIM, (hh + 1) * HEAD_DIM)
            q = q_ref[sb * qn:(sb + 1) * qn, cols]
            ku = k_ref[pl.ds(start, kr * GRID_W), cols]
            vu = v_ref[pl.ds(start, kr * GRID_W), cols]
            s_loc = lax.dot_general(q, ku, _NT, preferred_element_type=F32) + bias_ref[cls, hh]
            s_ctx = lax.dot_general(q, kc_ref[:, cols], _NT, preferred_element_type=F32)
            m = jnp.maximum(jnp.max(s_loc, axis=-1, keepdims=True), jnp.max(s_ctx, axis=-1, keepdims=True))
            p_loc = jnp.exp2(s_loc - m)
            p_ctx = jnp.exp2(s_ctx - m)
            l = jnp.sum(p_loc, axis=-1, keepdims=True) + jnp.sum(p_ctx, axis=-1, keepdims=True)
            o = (jnp.dot(p_loc.astype(BF16), vu, preferred_element_type=F32)
                 + jnp.dot(p_ctx.astype(BF16), vc_ref[:, cols], preferred_element_type=F32))
            o_ref[sb * qn:(sb + 1) * qn, cols] = (o / l).astype(o_ref.dtype)


def _na_attention(qkv, bias, *, n_batch, seq, ctx_len, n_heads, rb, hb=2, nsb=4):
    m_all = qkv.shape[0]
    rows = seq // GRID_W
    hw = hb * HEAD_DIM
    hblocks = n_heads // hb
    qrows = nsb * rb * GRID_W
    lat_spec = lambda part: pl.BlockSpec((seq, hw), lambda b, h, r: (b, part * hblocks + h))
    ctx_spec = lambda part: pl.BlockSpec((ctx_len, hw), lambda b, h, r: (n_batch * seq // ctx_len + b, part * hblocks + h))
    q_spec = pl.BlockSpec((qrows, hw), lambda b, h, r: (b * (seq // qrows) + r, h))
    return pl.pallas_call(
        functools.partial(_na_kernel, rb=rb, hb=hb, nsb=nsb, rows=rows),
        grid=(n_batch, hblocks, rows // (rb * nsb)),
        in_specs=[
            q_spec,
            lat_spec(1),
            lat_spec(2),
            ctx_spec(1),
            ctx_spec(2),
            pl.BlockSpec((bias.shape[0], hb) + bias.shape[2:], lambda b, h, r: (0, h, 0, 0)),
        ],
        out_specs=q_spec,
        out_shape=jax.ShapeDtypeStruct((m_all, n_heads * HEAD_DIM), BF16),
        compiler_params=_params("arbitrary", "arbitrary", "arbitrary"),
        name="na_attention",
    )(qkv, qkv, qkv, qkv, qkv, bias)


def _ctx_attn_kernel(q_ref, k_ref, v_ref, o_in_ref, o_ref, *, hb):
    del o_in_ref
    for hh in range(hb):
        cols = slice(hh * HEAD_DIM, (hh + 1) * HEAD_DIM)
        s = lax.dot_general(q_ref[:, cols], k_ref[:, cols], _NT, preferred_element_type=F32)
        p = jnp.exp2(s - jnp.max(s, axis=-1, keepdims=True))
        l = jnp.sum(p, axis=-1, keepdims=True)
        o = jnp.dot(p.astype(BF16), v_ref[:, cols], preferred_element_type=F32)
        o_ref[:, cols] = (o / l).astype(o_ref.dtype)


def _ctx_attention(qkv, o, *, n_batch, seq, ctx_len, n_heads, hb=2):
    hw = hb * HEAD_DIM
    hblocks = n_heads // hb
    row0 = n_batch * seq // ctx_len
    spec = lambda part: pl.BlockSpec((ctx_len, hw), lambda b, h: (row0 + b, part * hblocks + h))
    return pl.pallas_call(
        functools.partial(_ctx_attn_kernel, hb=hb),
        grid=(n_batch, hblocks),
        in_specs=[spec(0), spec(1), spec(2), pl.BlockSpec(memory_space=pl.ANY)],
        out_specs=spec(0),
        out_shape=jax.ShapeDtypeStruct(o.shape, o.dtype),
        input_output_aliases={3: 0},
        compiler_params=_params("arbitrary", "arbitrary"),
        name="ctx_attention",
    )(qkv, qkv, qkv, o)


def _gqa_kernel(q_ref, k_ref, vt_ref, kc_ref, vtc_ref, o_ref, m_scr, l_scr, acc_scr):
    m_scr[...] = jnp.full(m_scr.shape, NEG_INF, F32)
    l_scr[...] = jnp.zeros(l_scr.shape, F32)
    acc_scr[...] = jnp.zeros(acc_scr.shape, F32)

    def chunk(k, vt):
        for g in range(GQA_GROUP):
            q = q_ref[:, g * HEAD_DIM:(g + 1) * HEAD_DIM]
            s = lax.dot_general(k, q, _NT, preferred_element_type=F32)
            m_old = m_scr[g]
            m_new = jnp.maximum(m_old, jnp.max(s, axis=0, keepdims=True))
            alpha = jnp.exp2(m_old - m_new)
            p = jnp.exp2(s - m_new)
            l_scr[g] = alpha * l_scr[g] + jnp.sum(p, axis=0, keepdims=True)
            acc_scr[g] = alpha * acc_scr[g] + jnp.dot(vt, p.astype(BF16), preferred_element_type=F32)
            m_scr[g] = m_new

    tk = vt_ref.shape[2]

    def body(c, carry):
        chunk(k_ref[pl.ds(pl.multiple_of(c * tk, tk), tk), :], vt_ref[c])
        return carry

    lax.fori_loop(0, vt_ref.shape[0], body, 0)
    chunk(kc_ref[...], vtc_ref[...])
    for g in range(GQA_GROUP):
        o_ref[:, g * HEAD_DIM:(g + 1) * HEAD_DIM] = (acc_scr[g] / l_scr[g]).T.astype(o_ref.dtype)


def _gqa_attention(q, k, vt, *, n_batch, seq, ctx_len, tq=1024):
    n_kv = k.shape[1] // HEAD_DIM
    tk = vt.shape[2]
    gw = GQA_GROUP * HEAD_DIM
    q_spec = pl.BlockSpec((tq, gw), lambda b, h, i: (b * (seq // tq) + i, h))
    return pl.pallas_call(
        _gqa_kernel,
        grid=(n_batch, n_kv, seq // tq),
        in_specs=[
            q_spec,
            pl.BlockSpec((seq, HEAD_DIM), lambda b, h, i: (b, h)),
            pl.BlockSpec((seq // tk, HEAD_DIM, tk), lambda b, h, i: (b, h, 0)),
            pl.BlockSpec((ctx_len, HEAD_DIM), lambda b, h, i: (n_batch * seq // ctx_len + b, h)),
            pl.BlockSpec((None, HEAD_DIM, ctx_len), lambda b, h, i: (n_batch * seq // tk, h, b)),
        ],
        out_specs=q_spec,
        out_shape=jax.ShapeDtypeStruct(q.shape, BF16),
        scratch_shapes=[
            pltpu.VMEM((GQA_GROUP, 1, tq), F32),
            pltpu.VMEM((GQA_GROUP, 1, tq), F32),
            pltpu.VMEM((GQA_GROUP, HEAD_DIM, tq), F32),
        ],
        compiler_params=_params("arbitrary", "arbitrary", "arbitrary"),
        name="gqa_attention",
    )(q, k, vt, k, vt)


def _na_bias_table(rpb, rb, rows):
    ww = NA_WIN_W
    kr = _na_block_rows(rb)
    col = jnp.arange(GRID_W)
    cs = jnp.clip(col - ww // 2, 0, GRID_W - ww)
    in_win = (col[None, :] >= cs[:, None]) & (col[None, :] < cs[:, None] + ww)
    dc_idx = jnp.clip(col[None, :] - col[:, None], -(ww - 1), ww - 1) + ww - 1
    bias_cols = jnp.transpose(rpb[:, :, dc_idx], (0, 2, 1, 3)).astype(F32)
    bias_cols = jnp.where(in_win[None, :, None, :], bias_cols, NEG_INF)

    def pattern(r0):
        kstart = min(max(r0 - NA_WIN_H // 2, 0), rows - kr)
        r = r0 + jnp.arange(rb)[:, None]
        krow = kstart + jnp.arange(kr)[None, :]
        rs = jnp.clip(r - NA_WIN_H // 2, 0, rows - NA_WIN_H)
        valid = (krow >= rs) & (krow < rs + NA_WIN_H)
        return valid, jnp.clip(krow - r + NA_WIN_H - 1, 0, 2 * NA_WIN_H - 2)

    assert rows // rb >= 3 and rb >= NA_WIN_H // 2
    tabs = []
    for r0 in (0, rb, rows - rb):
        valid, idx = pattern(r0)
        t = jnp.where(valid[None, None, :, :, None], bias_cols[:, :, idx, :], NEG_INF)
        tabs.append(jnp.transpose(t, (0, 2, 1, 3, 4)).reshape(rpb.shape[0], rb * GRID_W, kr * GRID_W))
    return jnp.stack(tabs) * LOG2E


def _rope_tables(n_batch, seq, ctx_len):
    quarter = HEAD_DIM // 4
    t = jnp.arange(seq)
    freqs = ROPE_THETA ** (-jnp.arange(quarter, dtype=F32) / quarter)
    ang_r = (t // GRID_W).astype(F32)[:, None] * freqs[None, :]
    ang_c = (t % GRID_W).astype(F32)[:, None] * freqs[None, :]
    cos = jnp.concatenate([jnp.cos(ang_r)] * 2 + [jnp.cos(ang_c)] * 2, axis=-1)
    sin = jnp.concatenate([-jnp.sin(ang_r), jnp.sin(ang_r), -jnp.sin(ang_c), jnp.sin(ang_c)], axis=-1)
    n_ctx = n_batch * ctx_len
    cos = jnp.concatenate([jnp.tile(cos, (n_batch, 1)), jnp.ones((n_ctx, HEAD_DIM), F32)], axis=0)
    sin = jnp.concatenate([jnp.tile(sin, (n_batch, 1)), jnp.zeros((n_ctx, HEAD_DIM), F32)], axis=0)
    return cos, sin


def kernel(x, c, ctx, c_ctx, ada_w, ada_b, norm_g, na_wqkv, na_wo, na_rpb, gqa_wq, gqa_wkv, gqa_q_norm,
           gqa_k_norm, gqa_wo, ffn_w13, ffn_w2):
    n_batch, seq, d = x.shape
    ctx_len = ctx.shape[1]
    depth = ada_w.shape[0]
    assert depth == 2 and na_wqkv.shape[0] == 1 and gqa_wq.shape[0] == 1
    assert seq % GRID_W == 0 and n_batch + 1 <= MOD_ROWS
    n_heads = d // HEAD_DIM
    m_lat = n_batch * seq
    m_all = m_lat + n_batch * ctx_len
    scale = HEAD_DIM ** -0.5 * LOG2E
    tm_all = m_all // 8
    tm_lat = m_lat // 8
    tm_down = 2
    na_rb = 4
    seg_tiles = lambda t: seq // t
    assert tm_all % 16 == 0 and tm_lat % 16 == 0

    cvec = jnp.zeros((MOD_ROWS, d), F32).at[:n_batch].set(c).at[n_batch].set(c_ctx)
    mod = _ada_mod(cvec, ada_w, ada_b).reshape(depth, MOD_ROWS, N_MOD, d)
    na_bias = _na_bias_table(na_rpb[0], na_rb, seq // GRID_W)
    cos, sin = _rope_tables(n_batch, seq, ctx_len)

    wqkv = na_wqkv[0].astype(BF16)
    na_wo_b = na_wo[0].astype(BF16)
    wq = gqa_wq[0].astype(BF16)
    wkv = gqa_wkv[0].astype(BF16)
    gqa_wo_b = gqa_wo[0].astype(BF16)
    w13 = [ffn_w13[i].astype(BF16) for i in range(depth)]
    w2 = [ffn_w2[i].astype(BF16) for i in range(depth)]

    xa = jnp.concatenate([x.reshape(m_lat, d), ctx.reshape(n_batch * ctx_len, d)], axis=0)
    seg_kw = dict(seg_tiles=seg_tiles, n_batch=n_batch)

    h = _prenorm(xa, norm_g[0], mod[0], g_row=0, sh_row=0, sc_row=1, **seg_kw)
    qkv = _matmul(h, wqkv, rows=m_all, n_out=wqkv.shape[1], out_dtype=BF16, tm=tm_all, tn=1024,
                  scale_blocks=d // 1024, scale=scale, name="na_qkv")
    o = _na_attention(qkv, na_bias, n_batch=n_batch, seq=seq, ctx_len=ctx_len, n_heads=n_heads, rb=na_rb)
    o = _ctx_attention(qkv, o, n_batch=n_batch, seq=seq, ctx_len=ctx_len, n_heads=n_heads)
    y = _matmul(o, na_wo_b, rows=m_all, n_out=d, out_dtype=F32, tm=tm_all, tn=1024, name="na_wo")
    xa, h = _resid(y, xa, norm_g[0], mod[0], rows=m_all, gt_row=2, gpost_row=1,
                   nxt=(2, 3, 4), g2=norm_g[0], mod2=mod[0], **seg_kw)
    gu = _gateup(h, w13[0], rows=m_all, tm=tm_all, tn=256)
    y = _matmul(gu, w2[0], rows=m_all, n_out=d, out_dtype=F32, tm=tm_all // tm_down, tn=512, name="ffn_down")
    xa, h = _resid(y, xa, norm_g[0], mod[0], rows=m_all, gt_row=5, gpost_row=3,
                   nxt=(0, 0, 1), g2=norm_g[1], mod2=mod[1], **seg_kw)

    kv_w = wkv.shape[1] // 2
    q = _matmul_rope(h, wq, gqa_q_norm[0], cos, sin, rows=m_lat, n_out=d, tm=tm_lat, tn=1024, scale=scale, name="gqa_q")
    k = _matmul_rope(h, wkv, gqa_k_norm[0], cos, sin, rows=m_all, n_out=kv_w, tm=tm_all, tn=kv_w, scale=1.0, name="gqa_k")
    gqa_tk = 512
    assert seq % gqa_tk == 0 and n_batch * ctx_len == gqa_tk
    vt = _matmul_t(h, wkv, rows=m_all, n_out=kv_w, tm=gqa_tk, tn=kv_w, col_blk_off=1, name="gqa_v")
    o = _gqa_attention(q, k, vt, n_batch=n_batch, seq=seq, ctx_len=ctx_len)
    y = _matmul(o, gqa_wo_b, rows=m_lat, n_out=d, out_dtype=F32, tm=tm_lat, tn=1024, name="gqa_wo")
    xl, h = _resid(y, xa, norm_g[1], mod[1], rows=m_lat, gt_row=2, gpost_row=1,
                   nxt=(2, 3, 4), g2=norm_g[1], mod2=mod[1], **seg_kw)
    gu = _gateup(h, w13[1], rows=m_lat, tm=tm_lat, tn=256)
    y = _matmul(gu, w2[1], rows=m_lat, n_out=d, out_dtype=F32, tm=tm_lat // tm_down, tn=512, name="ffn_down")
    xl = _resid(y, xl, norm_g[1], mod[1], rows=m_lat, gt_row=5, gpost_row=3, **seg_kw)
    return xl.reshape(n_batch, seq, d)
```

```python
import functools

import jax
import jax.numpy as jnp
from jax import lax
from jax.experimental import pallas as pl
from jax.experimental.pallas import tpu as pltpu

GRID_W = 64
NA_WIN_H = 8
NA_WIN_W = 16
HEAD_DIM = 128
GQA_GROUP = 4
ROPE_THETA = 10000.0
NORM_EPS = 1e-6
NEG_INF = -1e30
LOG2E = 1.4426950408889634
N_MOD = 6

VMEM_LIMIT_BYTES = 56 * 1024 * 1024
MXU_COLS = 256
MOD_ROWS = 8

F32 = jnp.float32
BF16 = jnp.bfloat16


def _params(*sem):
    return pltpu.CompilerParams(dimension_semantics=sem, vmem_limit_bytes=VMEM_LIMIT_BYTES)


def _rms(x, g):
    ms = jnp.mean(x * x, axis=-1, keepdims=True)
    return x * lax.rsqrt(ms + NORM_EPS) * g


def _seg_index(rows_per_seg_tiles, n_batch):
    return lambda i: jnp.minimum(i // rows_per_seg_tiles, n_batch)


def _ada_kernel(c_ref, w_ref, b_ref, o_ref):
    c = c_ref[...]
    s = (c * jax.nn.sigmoid(c)).astype(BF16)
    o_ref[...] = jnp.dot(s, w_ref[...].astype(BF16), preferred_element_type=F32) + b_ref[...]


def _ada_mod(cvec, ada_w, ada_b, tn=1024):
    depth, d, n = ada_w.shape
    return pl.pallas_call(
        _ada_kernel,
        grid=(depth, n // tn),
        in_specs=[
            pl.BlockSpec((MOD_ROWS, d), lambda l, j: (0, 0)),
            pl.BlockSpec((None, d, tn), lambda l, j: (l, 0, j)),
            pl.BlockSpec((None, 1, tn), lambda l, j: (l, 0, j)),
        ],
        out_specs=pl.BlockSpec((None, MOD_ROWS, tn), lambda l, j: (l, 0, j)),
        out_shape=jax.ShapeDtypeStruct((depth, MOD_ROWS, n), F32),
        compiler_params=_params("arbitrary", "arbitrary"),
        name="ada_mod",
    )(cvec, ada_w, ada_b.reshape(depth, 1, n))


def _prenorm_kernel(x_ref, g_ref, mod_ref, h_ref, *, g_row, sh_row, sc_row):
    y = _rms(x_ref[...], g_ref[g_row:g_row + 1, :])
    h = y * (1.0 + mod_ref[sc_row:sc_row + 1, :]) + mod_ref[sh_row:sh_row + 1, :]
    h_ref[...] = h.astype(h_ref.dtype)


def _prenorm(x, g, mod, *, seg_tiles, n_batch, g_row, sh_row, sc_row, tm=256):
    m, d = x.shape
    seg = _seg_index(seg_tiles(tm), n_batch)
    return pl.pallas_call(
        functools.partial(_prenorm_kernel, g_row=g_row, sh_row=sh_row, sc_row=sc_row),
        grid=(m // tm,),
        in_specs=[
            pl.BlockSpec((tm, d), lambda i: (i, 0)),
            pl.BlockSpec(g.shape, lambda i: (0, 0)),
            pl.BlockSpec((None, N_MOD, d), lambda i: (seg(i), 0, 0)),
        ],
        out_specs=pl.BlockSpec((tm, d), lambda i: (i, 0)),
        out_shape=jax.ShapeDtypeStruct((m, d), BF16),
        compiler_params=_params("arbitrary"),
        name="prenorm",
    )(x, g, mod)


def _resid_kernel(y_ref, x_ref, g_ref, mod_ref, *rest, gt_row, gpost_row, nxt):
    xn = x_ref[...] + mod_ref[gt_row:gt_row + 1, :] * _rms(y_ref[...], g_ref[gpost_row:gpost_row + 1, :])
    if nxt is None:
        (xo_ref,) = rest
        xo_ref[...] = xn
        return
    g2_ref, mod2_ref, xo_ref, h_ref = rest
    gpre_row, sh_row, sc_row = nxt
    xo_ref[...] = xn
    h = _rms(xn, g2_ref[gpre_row:gpre_row + 1, :])
    h = h * (1.0 + mod2_ref[sc_row:sc_row + 1, :]) + mod2_ref[sh_row:sh_row + 1, :]
    h_ref[...] = h.astype(h_ref.dtype)


def _resid(y, x, g, mod, *, rows, seg_tiles, n_batch, gt_row, gpost_row, nxt=None, g2=None, mod2=None, tm=256):
    d = x.shape[1]
    seg = _seg_index(seg_tiles(tm), n_batch)
    row_spec = pl.BlockSpec((tm, d), lambda i: (i, 0))
    mod_spec = pl.BlockSpec((None, N_MOD, d), lambda i: (seg(i), 0, 0))
    in_specs = [row_spec, row_spec, pl.BlockSpec(g.shape, lambda i: (0, 0)), mod_spec]
    args = [y, x, g, mod]
    out_specs = [row_spec]
    out_shape = [jax.ShapeDtypeStruct((rows, d), F32)]
    if nxt is not None:
        in_specs += [pl.BlockSpec(g2.shape, lambda i: (0, 0)), mod_spec]
        args += [g2, mod2]
        out_specs.append(row_spec)
        out_shape.append(jax.ShapeDtypeStruct((rows, d), BF16))
    out = pl.pallas_call(
        functools.partial(_resid_kernel, gt_row=gt_row, gpost_row=gpost_row, nxt=nxt),
        grid=(rows // tm,),
        in_specs=in_specs,
        out_specs=out_specs,
        out_shape=out_shape,
        compiler_params=_params("arbitrary"),
        name="resid_norm",
    )(*args)
    return out if nxt is not None else out[0]


def _mm_kernel(x_ref, w_ref, o_ref, *, scale_blocks, scale):
    acc = jnp.dot(x_ref[...], w_ref[...], preferred_element_type=F32)
    if scale_blocks:
        acc = acc * jnp.where(pl.program_id(1) < scale_blocks, scale, 1.0)
    o_ref[...] = acc.astype(o_ref.dtype)


def _matmul(x, w, *, rows, n_out, out_dtype, tm, tn, col_blk_off=0, scale_blocks=0, scale=1.0, name="matmul"):
    k = x.shape[1]
    return pl.pallas_call(
        functools.partial(_mm_kernel, scale_blocks=scale_blocks, scale=scale),
        grid=(rows // tm, n_out // tn),
        in_specs=[
            pl.BlockSpec((tm, k), lambda i, j: (i, 0)),
            pl.BlockSpec((k, tn), lambda i, j: (0, j + col_blk_off)),
        ],
        out_specs=pl.BlockSpec((tm, tn), lambda i, j: (i, j)),
        out_shape=jax.ShapeDtypeStruct((rows, n_out), out_dtype),
        compiler_params=_params("arbitrary", "arbitrary"),
        name=name,
    )(x, w)


def _mm_t_kernel(x_ref, w_ref, o_ref):
    acc = jnp.dot(x_ref[...], w_ref[...], preferred_element_type=F32)
    o_ref[...] = acc.T.astype(o_ref.dtype)


def _matmul_t(x, w, *, rows, n_out, tm, tn, col_blk_off=0, name="matmul_t"):
    k = x.shape[1]
    return pl.pallas_call(
        _mm_t_kernel,
        grid=(rows // tm, n_out // tn),
        in_specs=[
            pl.BlockSpec((tm, k), lambda i, j: (i, 0)),
            pl.BlockSpec((k, tn), lambda i, j: (0, j + col_blk_off)),
        ],
        out_specs=pl.BlockSpec((None, tn, tm), lambda i, j: (i, j, 0)),
        out_shape=jax.ShapeDtypeStruct((rows // tm, n_out, tm), BF16),
        compiler_params=_params("arbitrary", "arbitrary"),
        name=name,
    )(x, w)


def _swap_halves(y):
    lane = lax.broadcasted_iota(jnp.int32, y.shape, 1)
    return jnp.where((lane & 32) == 0, pltpu.roll(y, 96, 1), pltpu.roll(y, 32, 1))


def _mm_rope_kernel(x_ref, w_ref, g_ref, cos_ref, sin_ref, o_ref, *, scale):
    cos = cos_ref[...]
    sin = sin_ref[...]
    g = g_ref[...]
    x = x_ref[...]
    for grp in range(w_ref.shape[1] // MXU_COLS):
        acc = jnp.dot(x, w_ref[:, grp * MXU_COLS:(grp + 1) * MXU_COLS], preferred_element_type=F32)
        for hh in range(MXU_COLS // HEAD_DIM):
            y = _rms(acc[:, hh * HEAD_DIM:(hh + 1) * HEAD_DIM], g)
            y = y * cos + _swap_halves(y) * sin
            if scale != 1.0:
                y = y * scale
            col0 = grp * MXU_COLS + hh * HEAD_DIM
            o_ref[:, col0:col0 + HEAD_DIM] = y.astype(o_ref.dtype)


def _matmul_rope(x, w, g, cos, sin, *, rows, n_out, tm, tn, scale, name):
    k = x.shape[1]
    tab_spec = pl.BlockSpec((tm, HEAD_DIM), lambda i, j: (i, 0))
    return pl.pallas_call(
        functools.partial(_mm_rope_kernel, scale=scale),
        grid=(rows // tm, n_out // tn),
        in_specs=[
            pl.BlockSpec((tm, k), lambda i, j: (i, 0)),
            pl.BlockSpec((k, tn), lambda i, j: (0, j)),
            pl.BlockSpec((1, HEAD_DIM), lambda i, j: (0, 0)),
            tab_spec,
            tab_spec,
        ],
        out_specs=pl.BlockSpec((tm, tn), lambda i, j: (i, j)),
        out_shape=jax.ShapeDtypeStruct((rows, n_out), BF16),
        compiler_params=_params("arbitrary", "arbitrary"),
        name=name,
    )(x, w, g.reshape(1, HEAD_DIM), cos, sin)


def _mm_rope_t_kernel(x_ref, w_ref, cos_ref, sin_ref, o_ref):
    cos = cos_ref[...]
    sin = sin_ref[...]
    x = x_ref[...]
    q4 = HEAD_DIM // 4
    for grp in range(w_ref.shape[1] // MXU_COLS):
        acc_t = jnp.dot(x, w_ref[:, grp * MXU_COLS:(grp + 1) * MXU_COLS], preferred_element_type=F32).T
        for hh in range(MXU_COLS // HEAD_DIM):
            y = acc_t[hh * HEAD_DIM:(hh + 1) * HEAD_DIM, :]
            r = lax.rsqrt(jnp.mean(y * y, axis=0, keepdims=True) + NORM_EPS)
            partner = jnp.concatenate([y[q4:2 * q4], y[:q4], y[3 * q4:], y[2 * q4:3 * q4]], axis=0)
            row0 = grp * MXU_COLS + hh * HEAD_DIM
            o_ref[row0:row0 + HEAD_DIM, :] = ((y * cos + partner * sin) * r).astype(o_ref.dtype)


def _matmul_rope_t(x, w, cos_t, sin_t, *, rows, n_out, tn, name):
    k = x.shape[1]
    n_tiles, _, tm = cos_t.shape
    assert n_tiles * tm == rows
    tab_spec = pl.BlockSpec((None, HEAD_DIM, tm), lambda i, j: (i, 0, 0))
    return pl.pallas_call(
        _mm_rope_t_kernel,
        grid=(n_tiles, n_out // tn),
        in_specs=[
            pl.BlockSpec((tm, k), lambda i, j: (i, 0)),
            pl.BlockSpec((k, tn), lambda i, j: (0, j)),
            tab_spec,
            tab_spec,
        ],
        out_specs=pl.BlockSpec((None, tn, tm), lambda i, j: (i, j, 0)),
        out_shape=jax.ShapeDtypeStruct((n_tiles, n_out, tm), BF16),
        compiler_params=_params("arbitrary", "arbitrary"),
        name=name,
    )(x, w, cos_t, sin_t)


def _gateup_kernel(x_ref, w1_ref, w3_ref, o_ref):
    x = x_ref[...]
    a = jnp.dot(x, w1_ref[...], preferred_element_type=F32)
    b = jnp.dot(x, w3_ref[...], preferred_element_type=F32)
    o_ref[...] = (a * jax.nn.sigmoid(a) * b).astype(o_ref.dtype)


def _gateup(x, w13, *, rows, tm, tn):
    k = x.shape[1]
    f = w13.shape[1] // 2
    return pl.pallas_call(
        _gateup_kernel,
        grid=(rows // tm, f // tn),
        in_specs=[
            pl.BlockSpec((tm, k), lambda i, j: (i, 0)),
            pl.BlockSpec((k, tn), lambda i, j: (0, j)),
            pl.BlockSpec((k, tn), lambda i, j: (0, j + f // tn)),
        ],
        out_specs=pl.BlockSpec((tm, tn), lambda i, j: (i, j)),
        out_shape=jax.ShapeDtypeStruct((rows, f), BF16),
        compiler_params=_params("arbitrary", "arbitrary"),
        name="ffn_gateup",
    )(x, w13, w13)


_NT = (((1,), (1,)), ((), ()))


def _na_block_rows(rb):
    return rb + NA_WIN_H


N_DR = 2 * NA_WIN_H - 1
N_DC = 2 * NA_WIN_W - 1
_TAB_BOTH, _TAB_FIRST, _TAB_SECOND, _TAB_NONE = 0, N_DR + 1, 2 * N_DR + 1, 3 * N_DR + 1
_TAB_SIZE = 3 * N_DR + 2


def _na_build_bias_tiles(rpb_ref, tab_scr, h0, hb):
    shape = (GRID_W, 2 * GRID_W)
    c = lax.broadcasted_iota(jnp.int32, shape, 0)
    lane = lax.broadcasted_iota(jnp.int32, shape, 1)
    kc = lane & (GRID_W - 1)
    second = lane >= GRID_W
    cs = jnp.clip(c - NA_WIN_W // 2, 0, GRID_W - NA_WIN_W)
    in_win = (kc >= cs) & (kc < cs + NA_WIN_W)
    dci = kc - c + (NA_WIN_W - 1)
    is_dc = [dci == k for k in range(N_DC)]
    neg = jnp.full(shape, NEG_INF, F32)
    for hh in range(hb):
        base = (h0 + hh) * (N_DR * N_DC)
        rows = []
        for d in range(N_DR):
            t = neg
            for k in range(N_DC):
                t = jnp.where(is_dc[k], rpb_ref[base + d * N_DC + k] * LOG2E, t)
            rows.append(jnp.where(in_win, t, NEG_INF))
        for d in range(N_DR + 1):
            lo = rows[d - 1] if d >= 1 else neg
            hi = rows[d] if d < N_DR else neg
            tab_scr[hh, _TAB_BOTH + d] = jnp.where(second, hi, lo)
        for d in range(N_DR):
            tab_scr[hh, _TAB_FIRST + d] = jnp.where(second, neg, rows[d])
            tab_scr[hh, _TAB_SECOND + d] = jnp.where(second, rows[d], neg)
        tab_scr[hh, _TAB_NONE] = neg


def _na_kernel(rpb_ref, q_ref, k_ref, v_ref, kc_ref, vc_ref, o_ref, tab_scr, *, rb, hb, nsb, rows):
    kr = _na_block_rows(rb)
    qn = rb * GRID_W
    half = NA_WIN_H // 2

    @pl.when(pl.program_id(2) == 0)
    def _():
        _na_build_bias_tiles(rpb_ref, tab_scr, pl.program_id(1) * hb, hb)

    for sb in range(nsb):
        r0 = (pl.program_id(2) * nsb + sb) * rb
        kstart = jnp.clip(r0 - half, 0, rows - kr)
        start = pl.multiple_of(kstart * GRID_W, GRID_W)
        tile_idx = []
        for i in range(rb):
            r = r0 + i
            rs = jnp.clip(r - half, 0, rows - NA_WIN_H)
            row_idx = []
            for jp in range(kr // 2):
                k0 = kstart + 2 * jp
                v0 = (k0 >= rs) & (k0 < rs + NA_WIN_H)
                v1 = (k0 + 1 >= rs) & (k0 + 1 < rs + NA_WIN_H)
                d0 = k0 - r + NA_WIN_H - 1
                idx = jnp.where(v0 & v1, _TAB_BOTH + d0 + 1,
                                jnp.where(v0, _TAB_FIRST + d0, jnp.where(v1, _TAB_SECOND + d0 + 1, _TAB_NONE)))
                row_idx.append(jnp.clip(idx, 0, _TAB_SIZE - 1))
            tile_idx.append(row_idx)
        for hh in range(hb):
            cols = slice(hh * HEAD_DIM, (hh + 1) * HEAD_DIM)
            q = q_ref[sb * qn:(sb + 1) * qn, cols]
            ku = k_ref[pl.ds(start, kr * GRID_W), cols]
            vu = v_ref[pl.ds(start, kr * GRID_W), cols]
            bias = jnp.concatenate(
                [jnp.concatenate([tab_scr[hh, idx] for idx in row_idx], axis=1) for row_idx in tile_idx], axis=0)
            s_loc = lax.dot_general(q, ku, _NT, preferred_element_type=F32) + bias
            s_ctx = lax.dot_general(q, kc_ref[:, cols], _NT, preferred_element_type=F32)
            m = jnp.maximum(jnp.max(s_loc, axis=-1, keepdims=True), jnp.max(s_ctx, axis=-1, keepdims=True))
            p_loc = jnp.exp2(s_loc - m)
            p_ctx = jnp.exp2(s_ctx - m)
            l = jnp.sum(p_loc, axis=-1, keepdims=True) + jnp.sum(p_ctx, axis=-1, keepdims=True)
            o = (jnp.dot(p_loc.astype(BF16), vu, preferred_element_type=F32)
                 + jnp.dot(p_ctx.astype(BF16), vc_ref[:, cols], preferred_element_type=F32))
            o_ref[sb * qn:(sb + 1) * qn, cols] = (o / l).astype(o_ref.dtype)


def _na_attention(qkv, rpb, *, n_batch, seq, ctx_len, n_heads, rb=4, hb=2, nsb=4):
    m_all = qkv.shape[0]
    rows = seq // GRID_W
    assert rpb.shape == (n_heads, N_DR, N_DC) and _na_block_rows(rb) % 2 == 0 and rows % (rb * nsb) == 0
    hw = hb * HEAD_DIM
    hblocks = n_heads // hb
    qrows = nsb * rb * GRID_W
    lat_spec = lambda part: pl.BlockSpec((seq, hw), lambda b, h, r, _: (b, part * hblocks + h))
    ctx_spec = lambda part: pl.BlockSpec((ctx_len, hw),
                                         lambda b, h, r, _: (n_batch * seq // ctx_len + b, part * hblocks + h))
    q_spec = pl.BlockSpec((qrows, hw), lambda b, h, r, _: (b * (seq // qrows) + r, h))
    return pl.pallas_call(
        functools.partial(_na_kernel, rb=rb, hb=hb, nsb=nsb, rows=rows),
        grid_spec=pltpu.PrefetchScalarGridSpec(
            num_scalar_prefetch=1,
            grid=(n_batch, hblocks, rows // (rb * nsb)),
            in_specs=[q_spec, lat_spec(1), lat_spec(2), ctx_spec(1), ctx_spec(2)],
            out_specs=q_spec,
            scratch_shapes=[pltpu.VMEM((hb, _TAB_SIZE, GRID_W, 2 * GRID_W), F32)],
        ),
        out_shape=jax.ShapeDtypeStruct((m_all, n_heads * HEAD_DIM), BF16),
        compiler_params=_params("arbitrary", "arbitrary", "arbitrary"),
        name="na_attention",
    )(rpb.reshape(-1), qkv, qkv, qkv, qkv, qkv)


def _ctx_attn_kernel(q_ref, k_ref, v_ref, o_in_ref, o_ref, *, hb):
    del o_in_ref
    for hh in range(hb):
        cols = slice(hh * HEAD_DIM, (hh + 1) * HEAD_DIM)
        s = lax.dot_general(q_ref[:, cols], k_ref[:, cols], _NT, preferred_element_type=F32)
        p = jnp.exp2(s - jnp.max(s, axis=-1, keepdims=True))
        l = jnp.sum(p, axis=-1, keepdims=True)
        o = jnp.dot(p.astype(BF16), v_ref[:, cols], preferred_element_type=F32)
        o_ref[:, cols] = (o / l).astype(o_ref.dtype)


def _ctx_attention(qkv, o, *, n_batch, seq, ctx_len, n_heads, hb=2):
    hw = hb * HEAD_DIM
    hblocks = n_heads // hb
    row0 = n_batch * seq // ctx_len
    spec = lambda part: pl.BlockSpec((ctx_len, hw), lambda b, h: (row0 + b, part * hblocks + h))
    return pl.pallas_call(
        functools.partial(_ctx_attn_kernel, hb=hb),
        grid=(n_batch, hblocks),
        in_specs=[spec(0), spec(1), spec(2), pl.BlockSpec(memory_space=pl.ANY)],
        out_specs=spec(0),
        out_shape=jax.ShapeDtypeStruct(o.shape, o.dtype),
        input_output_aliases={3: 0},
        compiler_params=_params("arbitrary", "arbitrary"),
        name="ctx_attention",
    )(qkv, qkv, qkv, o)


def _gqa_kernel(q_ref, k_ref, vt_ref, kc_ref, vtc_ref, o_ref, m_scr, l_scr, acc_scr, s0_scr, s1_scr):
    n_chunks, _, tk = vt_ref.shape
    m_scr[...] = jnp.full(m_scr.shape, NEG_INF, F32)
    l_scr[...] = jnp.zeros(l_scr.shape, F32)
    acc_scr[...] = jnp.zeros(acc_scr.shape, F32)

    def scores(k, g):
        return jnp.dot(k, q_ref[g * HEAD_DIM:(g + 1) * HEAD_DIM, :], preferred_element_type=F32)

    def accumulate(s, vt, g):
        m_old = m_scr[g]
        m_new = jnp.maximum(m_old, jnp.max(s, axis=0, keepdims=True))
        alpha = jnp.exp2(m_old - m_new)
        p = jnp.exp2(s - m_new)
        l_scr[g] = alpha * l_scr[g] + jnp.sum(p, axis=0, keepdims=True)
        acc_scr[g] = alpha * acc_scr[g] + jnp.dot(vt, p.astype(BF16), preferred_element_type=F32)
        m_scr[g] = m_new

    def k_chunk(c):
        return k_ref[pl.ds(pl.multiple_of(c * tk, tk), tk), :]

    def stage(cur_scr, nxt_scr, c):
        k_next = k_chunk(c + 1)
        vt = vt_ref[c]
        for g in range(GQA_GROUP):
            nxt_scr[g] = scores(k_next, g)
            accumulate(cur_scr[g], vt, g)

    for g in range(GQA_GROUP):
        s0_scr[g] = scores(k_chunk(0), g)

    def body(j, carry):
        stage(s0_scr, s1_scr, 2 * j)
        stage(s1_scr, s0_scr, 2 * j + 1)
        return carry

    lax.fori_loop(0, n_chunks // 2 - 1, body, 0)
    stage(s0_scr, s1_scr, n_chunks - 2)
    vt_last = vt_ref[n_chunks - 1]
    for g in range(GQA_GROUP):
        s_ctx = scores(kc_ref[...], g)
        accumulate(s1_scr[g], vt_last, g)
        accumulate(s_ctx, vtc_ref[...], g)
    for g in range(GQA_GROUP):
        o_ref[:, g * HEAD_DIM:(g + 1) * HEAD_DIM] = (acc_scr[g] / l_scr[g]).T.astype(o_ref.dtype)


def _gqa_attention(qt, k, vt, *, n_batch, seq, ctx_len):
    n_kv = k.shape[1] // HEAD_DIM
    tq = qt.shape[2]
    tk = vt.shape[2]
    assert (seq // tk) % 2 == 0 and seq % tq == 0
    gw = GQA_GROUP * HEAD_DIM
    q_spec = pl.BlockSpec((tq, gw), lambda b, h, i: (b * (seq // tq) + i, h))
    return pl.pallas_call(
        _gqa_kernel,
        grid=(n_batch, n_kv, seq // tq),
        in_specs=[
            pl.BlockSpec((None, gw, tq), lambda b, h, i: (b * (seq // tq) + i, h, 0)),
            pl.BlockSpec((seq, HEAD_DIM), lambda b, h, i: (b, h)),
            pl.BlockSpec((seq // tk, HEAD_DIM, tk), lambda b, h, i: (b, h, 0)),
            pl.BlockSpec((ctx_len, HEAD_DIM), lambda b, h, i: (n_batch * seq // ctx_len + b, h)),
            pl.BlockSpec((None, HEAD_DIM, ctx_len), lambda b, h, i: (n_batch * seq // tk, h, b)),
        ],
        out_specs=q_spec,
        out_shape=jax.ShapeDtypeStruct((n_batch * seq, qt.shape[1]), BF16),
        scratch_shapes=[
            pltpu.VMEM((GQA_GROUP, 1, tq), F32),
            pltpu.VMEM((GQA_GROUP, 1, tq), F32),
            pltpu.VMEM((GQA_GROUP, HEAD_DIM, tq), F32),
            pltpu.VMEM((GQA_GROUP, tk, tq), F32),
            pltpu.VMEM((GQA_GROUP, tk, tq), F32),
        ],
        compiler_params=_params("arbitrary", "arbitrary", "arbitrary"),
        name="gqa_attention",
    )(qt, k, vt, k, vt)


def _rope_partner(v):
    q4 = HEAD_DIM // 4
    return jnp.concatenate([v[..., q4:2 * q4], v[..., :q4], v[..., 3 * q4:], v[..., 2 * q4:3 * q4]], axis=-1)


def _rope_tables_t(cos, sin, g, scale, rows, tm):
    cos_t = (cos[:rows] * (g * scale)[None, :]).reshape(rows // tm, tm, HEAD_DIM)
    sin_t = (sin[:rows] * (_rope_partner(g) * scale)[None, :]).reshape(rows // tm, tm, HEAD_DIM)
    return jnp.transpose(cos_t, (0, 2, 1)), jnp.transpose(sin_t, (0, 2, 1))


def _rope_tables(n_batch, seq, ctx_len):
    quarter = HEAD_DIM // 4
    t = jnp.arange(seq)
    freqs = ROPE_THETA ** (-jnp.arange(quarter, dtype=F32) / quarter)
    ang_r = (t // GRID_W).astype(F32)[:, None] * freqs[None, :]
    ang_c = (t % GRID_W).astype(F32)[:, None] * freqs[None, :]
    cos = jnp.concatenate([jnp.cos(ang_r)] * 2 + [jnp.cos(ang_c)] * 2, axis=-1)
    sin = jnp.concatenate([-jnp.sin(ang_r), jnp.sin(ang_r), -jnp.sin(ang_c), jnp.sin(ang_c)], axis=-1)
    n_ctx = n_batch * ctx_len
    cos = jnp.concatenate([jnp.tile(cos, (n_batch, 1)), jnp.ones((n_ctx, HEAD_DIM), F32)], axis=0)
    sin = jnp.concatenate([jnp.tile(sin, (n_batch, 1)), jnp.zeros((n_ctx, HEAD_DIM), F32)], axis=0)
    return cos, sin


def kernel(x, c, ctx, c_ctx, ada_w, ada_b, norm_g, na_wqkv, na_wo, na_rpb, gqa_wq, gqa_wkv, gqa_q_norm,
           gqa_k_norm, gqa_wo, ffn_w13, ffn_w2):
    n_batch, seq, d = x.shape
    ctx_len = ctx.shape[1]
    depth = ada_w.shape[0]
    assert depth == 2 and na_wqkv.shape[0] == 1 and gqa_wq.shape[0] == 1
    assert seq % GRID_W == 0 and n_batch + 1 <= MOD_ROWS
    n_heads = d // HEAD_DIM
    m_lat = n_batch * seq
    m_all = m_lat + n_batch * ctx_len
    scale = HEAD_DIM ** -0.5 * LOG2E
    tm_all = m_all // 8
    tm_lat = m_lat // 8
    tm_down = 2
    seg_tiles = lambda t: seq // t
    assert tm_all % 16 == 0 and tm_lat % 16 == 0

    cvec = jnp.zeros((MOD_ROWS, d), F32).at[:n_batch].set(c).at[n_batch].set(c_ctx)
    mod = _ada_mod(cvec, ada_w, ada_b).reshape(depth, MOD_ROWS, N_MOD, d)
    cos, sin = _rope_tables(n_batch, seq, ctx_len)
    cos_qt, sin_qt = _rope_tables_t(cos, sin, gqa_q_norm[0], scale, m_lat, tm_lat)

    wqkv = na_wqkv[0].astype(BF16)
    na_wo_b = na_wo[0].astype(BF16)
    wq = gqa_wq[0].astype(BF16)
    wkv = gqa_wkv[0].astype(BF16)
    gqa_wo_b = gqa_wo[0].astype(BF16)
    w13 = [ffn_w13[i].astype(BF16) for i in range(depth)]
    w2 = [ffn_w2[i].astype(BF16) for i in range(depth)]

    xa = jnp.concatenate([x.reshape(m_lat, d), ctx.reshape(n_batch * ctx_len, d)], axis=0)
    seg_kw = dict(seg_tiles=seg_tiles, n_batch=n_batch)

    h = _prenorm(xa, norm_g[0], mod[0], g_row=0, sh_row=0, sc_row=1, **seg_kw)
    qkv = _matmul(h, wqkv, rows=m_all, n_out=wqkv.shape[1], out_dtype=BF16, tm=tm_all, tn=1024,
                  scale_blocks=d // 1024, scale=scale, name="na_qkv")
    o = _na_attention(qkv, na_rpb[0], n_batch=n_batch, seq=seq, ctx_len=ctx_len, n_heads=n_heads)
    o = _ctx_attention(qkv, o, n_batch=n_batch, seq=seq, ctx_len=ctx_len, n_heads=n_heads)
    y = _matmul(o, na_wo_b, rows=m_all, n_out=d, out_dtype=F32, tm=tm_all, tn=1024, name="na_wo")
    xa, h = _resid(y, xa, norm_g[0], mod[0], rows=m_all, gt_row=2, gpost_row=1,
                   nxt=(2, 3, 4), g2=norm_g[0], mod2=mod[0], **seg_kw)
    gu = _gateup(h, w13[0], rows=m_all, tm=tm_all, tn=256)
    y = _matmul(gu, w2[0], rows=m_all, n_out=d, out_dtype=F32, tm=tm_all // tm_down, tn=512, name="ffn_down")
    xa, h = _resid(y, xa, norm_g[0], mod[0], rows=m_all, gt_row=5, gpost_row=3,
                   nxt=(0, 0, 1), g2=norm_g[1], mod2=mod[1], **seg_kw)

    kv_w = wkv.shape[1] // 2
    qt = _matmul_rope_t(h, wq, cos_qt, sin_qt, rows=m_lat, n_out=d, tn=1024, name="gqa_q")
    k = _matmul_rope(h, wkv, gqa_k_norm[0], cos, sin, rows=m_all, n_out=kv_w, tm=tm_all, tn=kv_w, scale=1.0, name="gqa_k")
    gqa_tk = 512
    assert seq % gqa_tk == 0 and n_batch * ctx_len == gqa_tk
    vt = _matmul_t(h, wkv, rows=m_all, n_out=kv_w, tm=gqa_tk, tn=kv_w, col_blk_off=1, name="gqa_v")
    o = _gqa_attention(qt, k, vt, n_batch=n_batch, seq=seq, ctx_len=ctx_len)
    y = _matmul(o, gqa_wo_b, rows=m_lat, n_out=d, out_dtype=F32, tm=tm_lat, tn=1024, name="gqa_wo")
    xl, h = _resid(y, xa, norm_g[1], mod[1], rows=m_lat, gt_row=2, gpost_row=1,
                   nxt=(2, 3, 4), g2=norm_g[1], mod2=mod[1], **seg_kw)
    gu = _gateup(h, w13[1], rows=m_lat, tm=tm_lat, tn=256)
    y = _matmul(gu, w2[1], rows=m_lat, n_out=d, out_dtype=F32, tm=tm_lat // tm_down, tn=512, name="ffn_down")
    xl = _resid(y, xl, norm_g[1], mod[1], rows=m_lat, gt_row=5, gpost_row=3, **seg_kw)
    return xl.reshape(n_batch, seq, d)
```

```python
import functools

import jax
import jax.numpy as jnp
from jax import lax
from jax.experimental import pallas as pl
from jax.experimental.pallas import tpu as pltpu

GRID_W = 64
NA_WIN_H = 8
NA_WIN_W = 16
HEAD_DIM = 128
GQA_GROUP = 4
ROPE_THETA = 10000.0
NORM_EPS = 1e-6
NEG_INF = -1e30
LOG2E = 1.4426950408889634
N_MOD = 6

VMEM_LIMIT_BYTES = 56 * 1024 * 1024
MXU_COLS = 256
MOD_ROWS = 8

F32 = jnp.float32
BF16 = jnp.bfloat16


def _params(*sem):
    return pltpu.CompilerParams(dimension_semantics=sem, vmem_limit_bytes=VMEM_LIMIT_BYTES)


def _rms(x, g):
    ms = jnp.mean(x * x, axis=-1, keepdims=True)
    return x * lax.rsqrt(ms + NORM_EPS) * g


def _seg_index(rows_per_seg_tiles, n_batch):
    return lambda i: jnp.minimum(i // rows_per_seg_tiles, n_batch)


def _ada_kernel(c_ref, w_ref, b_ref, o_ref):
    c = c_ref[...]
    s = (c * jax.nn.sigmoid(c)).astype(BF16)
    o_ref[...] = jnp.dot(s, w_ref[...].astype(BF16), preferred_element_type=F32) + b_ref[...]


def _ada_mod(cvec, ada_w, ada_b, tn=1024):
    depth, d, n = ada_w.shape
    return pl.pallas_call(
        _ada_kernel,
        grid=(depth, n // tn),
        in_specs=[
            pl.BlockSpec((MOD_ROWS, d), lambda l, j: (0, 0)),
            pl.BlockSpec((None, d, tn), lambda l, j: (l, 0, j)),
            pl.BlockSpec((None, 1, tn), lambda l, j: (l, 0, j)),
        ],
        out_specs=pl.BlockSpec((None, MOD_ROWS, tn), lambda l, j: (l, 0, j)),
        out_shape=jax.ShapeDtypeStruct((depth, MOD_ROWS, n), F32),
        compiler_params=_params("arbitrary", "arbitrary"),
        name="ada_mod",
    )(cvec, ada_w, ada_b.reshape(depth, 1, n))


def _token_rows(x_lat, x_ctx, rows, tm):
    d = x_lat.shape[1]
    n_lat = min(rows, x_lat.shape[0]) // tm
    if x_ctx is None:
        x_ctx = x_lat
        assert rows <= x_lat.shape[0]
    else:
        assert x_lat.shape[0] % tm == 0 and rows == x_lat.shape[0] + x_ctx.shape[0]
    specs = [pl.BlockSpec((tm, d), lambda i: (jnp.minimum(i, n_lat - 1), 0)),
             pl.BlockSpec((tm, d), lambda i: (jnp.maximum(i - n_lat, 0), 0))]
    return [x_lat, x_ctx], specs, n_lat


def _read_token_rows(xl_ref, xc_ref, n_lat):
    return jnp.where(pl.program_id(0) < n_lat, xl_ref[...], xc_ref[...])


def _prenorm_kernel(xl_ref, xc_ref, g_ref, mod_ref, h_ref, *, n_lat, g_row, sh_row, sc_row):
    y = _rms(_read_token_rows(xl_ref, xc_ref, n_lat), g_ref[g_row:g_row + 1, :])
    h = y * (1.0 + mod_ref[sc_row:sc_row + 1, :]) + mod_ref[sh_row:sh_row + 1, :]
    h_ref[...] = h.astype(h_ref.dtype)


def _prenorm(x_lat, x_ctx, g, mod, *, seg_tiles, n_batch, g_row, sh_row, sc_row, tm=256):
    d = x_lat.shape[1]
    m = x_lat.shape[0] + x_ctx.shape[0]
    seg = _seg_index(seg_tiles(tm), n_batch)
    x_args, x_specs, n_lat = _token_rows(x_lat, x_ctx, m, tm)
    return pl.pallas_call(
        functools.partial(_prenorm_kernel, n_lat=n_lat, g_row=g_row, sh_row=sh_row, sc_row=sc_row),
        grid=(m // tm,),
        in_specs=x_specs + [
            pl.BlockSpec(g.shape, lambda i: (0, 0)),
            pl.BlockSpec((None, N_MOD, d), lambda i: (seg(i), 0, 0)),
        ],
        out_specs=pl.BlockSpec((tm, d), lambda i: (i, 0)),
        out_shape=jax.ShapeDtypeStruct((m, d), BF16),
        compiler_params=_params("arbitrary"),
        name="prenorm",
    )(*x_args, g, mod)


def _resid_kernel(y_ref, xl_ref, xc_ref, g_ref, mod_ref, *rest, n_lat, gt_row, gpost_row, nxt):
    x = _read_token_rows(xl_ref, xc_ref, n_lat)
    xn = x + mod_ref[gt_row:gt_row + 1, :] * _rms(y_ref[...], g_ref[gpost_row:gpost_row + 1, :])
    if nxt is None:
        (xo_ref,) = rest
        xo_ref[...] = xn
        return
    g2_ref, mod2_ref, xo_ref, h_ref = rest
    gpre_row, sh_row, sc_row = nxt
    xo_ref[...] = xn
    h = _rms(xn, g2_ref[gpre_row:gpre_row + 1, :])
    h = h * (1.0 + mod2_ref[sc_row:sc_row + 1, :]) + mod2_ref[sh_row:sh_row + 1, :]
    h_ref[...] = h.astype(h_ref.dtype)


def _resid(y, x, g, mod, *, rows, seg_tiles, n_batch, gt_row, gpost_row, x_ctx=None, nxt=None, g2=None, mod2=None,
           tm=256):
    d = x.shape[1]
    seg = _seg_index(seg_tiles(tm), n_batch)
    row_spec = pl.BlockSpec((tm, d), lambda i: (i, 0))
    mod_spec = pl.BlockSpec((None, N_MOD, d), lambda i: (seg(i), 0, 0))
    x_args, x_specs, n_lat = _token_rows(x, x_ctx, rows, tm)
    in_specs = [row_spec] + x_specs + [pl.BlockSpec(g.shape, lambda i: (0, 0)), mod_spec]
    args = [y] + x_args + [g, mod]
    out_specs = [row_spec]
    out_shape = [jax.ShapeDtypeStruct((rows, d), F32)]
    if nxt is not None:
        in_specs += [pl.BlockSpec(g2.shape, lambda i: (0, 0)), mod_spec]
        args += [g2, mod2]
        out_specs.append(row_spec)
        out_shape.append(jax.ShapeDtypeStruct((rows, d), BF16))
    out = pl.pallas_call(
        functools.partial(_resid_kernel, n_lat=n_lat, gt_row=gt_row, gpost_row=gpost_row, nxt=nxt),
        grid=(rows // tm,),
        in_specs=in_specs,
        out_specs=out_specs,
        out_shape=out_shape,
        compiler_params=_params("arbitrary"),
        name="resid_norm",
    )(*args)
    return out if nxt is not None else out[0]


def _mm_kernel(x_ref, w_ref, o_ref, *, scale_blocks, scale):
    acc = jnp.dot(x_ref[...], w_ref[...].astype(BF16), preferred_element_type=F32)
    if scale_blocks:
        acc = acc * jnp.where(pl.program_id(1) < scale_blocks, scale, 1.0)
    o_ref[...] = acc.astype(o_ref.dtype)


def _matmul(x, w, *, rows, n_out, out_dtype, tm, tn, col_blk_off=0, scale_blocks=0, scale=1.0, name="matmul"):
    k = x.shape[1]
    return pl.pallas_call(
        functools.partial(_mm_kernel, scale_blocks=scale_blocks, scale=scale),
        grid=(rows // tm, n_out // tn),
        in_specs=[
            pl.BlockSpec((tm, k), lambda i, j: (i, 0)),
            pl.BlockSpec((k, tn), lambda i, j: (0, j + col_blk_off)),
        ],
        out_specs=pl.BlockSpec((tm, tn), lambda i, j: (i, j)),
        out_shape=jax.ShapeDtypeStruct((rows, n_out), out_dtype),
        compiler_params=_params("arbitrary", "arbitrary"),
        name=name,
    )(x, w)


def _mm_t_kernel(x_ref, w_ref, o_ref):
    acc = jnp.dot(x_ref[...], w_ref[...].astype(BF16), preferred_element_type=F32)
    o_ref[...] = acc.T.astype(o_ref.dtype)


def _matmul_t(x, w, *, rows, n_out, tm, tn, col_blk_off=0, name="matmul_t"):
    k = x.shape[1]
    return pl.pallas_call(
        _mm_t_kernel,
        grid=(rows // tm, n_out // tn),
        in_specs=[
            pl.BlockSpec((tm, k), lambda i, j: (i, 0)),
            pl.BlockSpec((k, tn), lambda i, j: (0, j + col_blk_off)),
        ],
        out_specs=pl.BlockSpec((None, tn, tm), lambda i, j: (i, j, 0)),
        out_shape=jax.ShapeDtypeStruct((rows // tm, n_out, tm), BF16),
        compiler_params=_params("arbitrary", "arbitrary"),
        name=name,
    )(x, w)


def _swap_halves(y):
    lane = lax.broadcasted_iota(jnp.int32, y.shape, 1)
    return jnp.where((lane & 32) == 0, pltpu.roll(y, 96, 1), pltpu.roll(y, 32, 1))


def _mm_rope_kernel(x_ref, w_ref, g_ref, cos_ref, sin_ref, o_ref, *, scale):
    cos = cos_ref[...]
    sin = sin_ref[...]
    g = g_ref[...]
    x = x_ref[...]
    for grp in range(w_ref.shape[1] // MXU_COLS):
        w = w_ref[:, grp * MXU_COLS:(grp + 1) * MXU_COLS].astype(BF16)
        acc = jnp.dot(x, w, preferred_element_type=F32)
        for hh in range(MXU_COLS // HEAD_DIM):
            y = _rms(acc[:, hh * HEAD_DIM:(hh + 1) * HEAD_DIM], g)
            y = y * cos + _swap_halves(y) * sin
            if scale != 1.0:
                y = y * scale
            col0 = grp * MXU_COLS + hh * HEAD_DIM
            o_ref[:, col0:col0 + HEAD_DIM] = y.astype(o_ref.dtype)


def _matmul_rope(x, w, g, cos, sin, *, rows, n_out, tm, tn, scale, name):
    k = x.shape[1]
    tab_spec = pl.BlockSpec((tm, HEAD_DIM), lambda i, j: (i, 0))
    return pl.pallas_call(
        functools.partial(_mm_rope_kernel, scale=scale),
        grid=(rows // tm, n_out // tn),
        in_specs=[
            pl.BlockSpec((tm, k), lambda i, j: (i, 0)),
            pl.BlockSpec((k, tn), lambda i, j: (0, j)),
            pl.BlockSpec((1, HEAD_DIM), lambda i, j: (0, 0)),
            tab_spec,
            tab_spec,
        ],
        out_specs=pl.BlockSpec((tm, tn), lambda i, j: (i, j)),
        out_shape=jax.ShapeDtypeStruct((rows, n_out), BF16),
        compiler_params=_params("arbitrary", "arbitrary"),
        name=name,
    )(x, w, g.reshape(1, HEAD_DIM), cos, sin)


def _mm_rope_t_kernel(x_ref, w_ref, cos_ref, sin_ref, o_ref):
    cos = cos_ref[...]
    sin = sin_ref[...]
    x = x_ref[...]
    q4 = HEAD_DIM // 4
    for grp in range(w_ref.shape[1] // MXU_COLS):
        w = w_ref[:, grp * MXU_COLS:(grp + 1) * MXU_COLS].astype(BF16)
        acc_t = jnp.dot(x, w, preferred_element_type=F32).T
        for hh in range(MXU_COLS // HEAD_DIM):
            y = acc_t[hh * HEAD_DIM:(hh + 1) * HEAD_DIM, :]
            r = lax.rsqrt(jnp.mean(y * y, axis=0, keepdims=True) + NORM_EPS)
            partner = jnp.concatenate([y[q4:2 * q4], y[:q4], y[3 * q4:], y[2 * q4:3 * q4]], axis=0)
            row0 = grp * MXU_COLS + hh * HEAD_DIM
            o_ref[row0:row0 + HEAD_DIM, :] = ((y * cos + partner * sin) * r).astype(o_ref.dtype)


def _matmul_rope_t(x, w, cos_t, sin_t, *, rows, n_out, tn, name):
    k = x.shape[1]
    n_tiles, _, tm = cos_t.shape
    assert n_tiles * tm == rows
    tab_spec = pl.BlockSpec((None, HEAD_DIM, tm), lambda i, j: (i, 0, 0))
    return pl.pallas_call(
        _mm_rope_t_kernel,
        grid=(n_tiles, n_out // tn),
        in_specs=[
            pl.BlockSpec((tm, k), lambda i, j: (i, 0)),
            pl.BlockSpec((k, tn), lambda i, j: (0, j)),
            tab_spec,
            tab_spec,
        ],
        out_specs=pl.BlockSpec((None, tn, tm), lambda i, j: (i, j, 0)),
        out_shape=jax.ShapeDtypeStruct((n_tiles, n_out, tm), BF16),
        compiler_params=_params("arbitrary", "arbitrary"),
        name=name,
    )(x, w, cos_t, sin_t)


def _gateup_kernel(x_ref, w1_ref, w3_ref, o_ref):
    x = x_ref[...]
    a = jnp.dot(x, w1_ref[...].astype(BF16), preferred_element_type=F32)
    b = jnp.dot(x, w3_ref[...].astype(BF16), preferred_element_type=F32)
    o_ref[...] = (a * jax.nn.sigmoid(a) * b).astype(o_ref.dtype)


def _gateup(x, w13, layer, *, rows, tm, tn):
    k = x.shape[1]
    f = w13.shape[2] // 2
    return pl.pallas_call(
        _gateup_kernel,
        grid=(rows // tm, f // tn),
        in_specs=[
            pl.BlockSpec((tm, k), lambda i, j: (i, 0)),
            pl.BlockSpec((None, k, tn), lambda i, j: (layer, 0, j)),
            pl.BlockSpec((None, k, tn), lambda i, j: (layer, 0, j + f // tn)),
        ],
        out_specs=pl.BlockSpec((tm, tn), lambda i, j: (i, j)),
        out_shape=jax.ShapeDtypeStruct((rows, f), BF16),
        compiler_params=_params("arbitrary", "arbitrary"),
        name="ffn_gateup",
    )(x, w13, w13)


_NT = (((1,), (1,)), ((), ()))


def _na_block_rows(rb):
    return rb + NA_WIN_H


N_DR = 2 * NA_WIN_H - 1
N_DC = 2 * NA_WIN_W - 1
_TAB_BOTH, _TAB_FIRST, _TAB_SECOND, _TAB_NONE = 0, N_DR + 1, 2 * N_DR + 1, 3 * N_DR + 1
_TAB_SIZE = 3 * N_DR + 2


def _na_build_bias_tiles(rpb_ref, tab_scr, h0, hb):
    shape = (GRID_W, 2 * GRID_W)
    c = lax.broadcasted_iota(jnp.int32, shape, 0)
    lane = lax.broadcasted_iota(jnp.int32, shape, 1)
    kc = lane & (GRID_W - 1)
    second = lane >= GRID_W
    cs = jnp.clip(c - NA_WIN_W // 2, 0, GRID_W - NA_WIN_W)
    in_win = (kc >= cs) & (kc < cs + NA_WIN_W)
    dci = kc - c + (NA_WIN_W - 1)
    is_dc = [dci == k for k in range(N_DC)]
    neg = jnp.full(shape, NEG_INF, F32)
    for hh in range(hb):
        base = (h0 + hh) * (N_DR * N_DC)
        rows = []
        for d in range(N_DR):
            t = neg
            for k in range(N_DC):
                t = jnp.where(is_dc[k], rpb_ref[base + d * N_DC + k] * LOG2E, t)
            rows.append(jnp.where(in_win, t, NEG_INF))
        for d in range(N_DR + 1):
            lo = rows[d - 1] if d >= 1 else neg
            hi = rows[d] if d < N_DR else neg
            tab_scr[hh, _TAB_BOTH + d] = jnp.where(second, hi, lo)
        for d in range(N_DR):
            tab_scr[hh, _TAB_FIRST + d] = jnp.where(second, neg, rows[d])
            tab_scr[hh, _TAB_SECOND + d] = jnp.where(second, rows[d], neg)
        tab_scr[hh, _TAB_NONE] = neg


def _na_kernel(rpb_ref, q_ref, k_ref, v_ref, kc_ref, vc_ref, o_ref, tab_scr, *, rb, hb, nsb, rows):
    kr = _na_block_rows(rb)
    qn = rb * GRID_W
    half = NA_WIN_H // 2

    @pl.when(pl.program_id(2) == 0)
    def _():
        _na_build_bias_tiles(rpb_ref, tab_scr, pl.program_id(1) * hb, hb)

    for sb in range(nsb):
        r0 = (pl.program_id(2) * nsb + sb) * rb
        kstart = jnp.clip(r0 - half, 0, rows - kr)
        start = pl.multiple_of(kstart * GRID_W, GRID_W)
        tile_idx = []
        for i in range(rb):
            r = r0 + i
            rs = jnp.clip(r - half, 0, rows - NA_WIN_H)
            row_idx = []
            for jp in range(kr // 2):
                k0 = kstart + 2 * jp
                v0 = (k0 >= rs) & (k0 < rs + NA_WIN_H)
                v1 = (k0 + 1 >= rs) & (k0 + 1 < rs + NA_WIN_H)
                d0 = k0 - r + NA_WIN_H - 1
                idx = jnp.where(v0 & v1, _TAB_BOTH + d0 + 1,
                                jnp.where(v0, _TAB_FIRST + d0, jnp.where(v1, _TAB_SECOND + d0 + 1, _TAB_NONE)))
                row_idx.append(jnp.clip(idx, 0, _TAB_SIZE - 1))
            tile_idx.append(row_idx)
        for hh in range(hb):
            cols = slice(hh * HEAD_DIM, (hh + 1) * HEAD_DIM)
            q = q_ref[sb * qn:(sb + 1) * qn, cols]
            ku = k_ref[pl.ds(start, kr * GRID_W), cols]
            vu = v_ref[pl.ds(start, kr * GRID_W), cols]
            bias = jnp.concatenate(
                [jnp.concatenate([tab_scr[hh, idx] for idx in row_idx], axis=1) for row_idx in tile_idx], axis=0)
            s_loc = lax.dot_general(q, ku, _NT, preferred_element_type=F32) + bias
            s_ctx = lax.dot_general(q, kc_ref[:, cols], _NT, preferred_element_type=F32)
            m = jnp.maximum(jnp.max(s_loc, axis=-1, keepdims=True), jnp.max(s_ctx, axis=-1, keepdims=True))
            p_loc = jnp.exp2(s_loc - m)
            p_ctx = jnp.exp2(s_ctx - m)
            l = jnp.sum(p_loc, axis=-1, keepdims=True) + jnp.sum(p_ctx, axis=-1, keepdims=True)
            o = (jnp.dot(p_loc.astype(BF16), vu, preferred_element_type=F32)
                 + jnp.dot(p_ctx.astype(BF16), vc_ref[:, cols], preferred_element_type=F32))
            o_ref[sb * qn:(sb + 1) * qn, cols] = (o / l).astype(o_ref.dtype)


def _na_attention(qkv, rpb, *, n_batch, seq, ctx_len, n_heads, rb=4, hb=2, nsb=4):
    m_all = qkv.shape[0]
    rows = seq // GRID_W
    assert rpb.shape == (n_heads, N_DR, N_DC) and _na_block_rows(rb) % 2 == 0 and rows % (rb * nsb) == 0
    hw = hb * HEAD_DIM
    hblocks = n_heads // hb
    qrows = nsb * rb * GRID_W
    lat_spec = lambda part: pl.BlockSpec((seq, hw), lambda b, h, r, _: (b, part * hblocks + h))
    ctx_spec = lambda part: pl.BlockSpec((ctx_len, hw),
                                         lambda b, h, r, _: (n_batch * seq // ctx_len + b, part * hblocks + h))
    q_spec = pl.BlockSpec((qrows, hw), lambda b, h, r, _: (b * (seq // qrows) + r, h))
    return pl.pallas_call(
        functools.partial(_na_kernel, rb=rb, hb=hb, nsb=nsb, rows=rows),
        grid_spec=pltpu.PrefetchScalarGridSpec(
            num_scalar_prefetch=1,
            grid=(n_batch, hblocks, rows // (rb * nsb)),
            in_specs=[q_spec, lat_spec(1), lat_spec(2), ctx_spec(1), ctx_spec(2)],
            out_specs=q_spec,
            scratch_shapes=[pltpu.VMEM((hb, _TAB_SIZE, GRID_W, 2 * GRID_W), F32)],
        ),
        out_shape=jax.ShapeDtypeStruct((m_all, n_heads * HEAD_DIM), BF16),
        compiler_params=_params("arbitrary", "arbitrary", "arbitrary"),
        name="na_attention",
    )(rpb.reshape(-1), qkv, qkv, qkv, qkv, qkv)


def _ctx_attn_kernel(q_ref, k_ref, v_ref, o_in_ref, o_ref, *, hb):
    del o_in_ref
    for hh in range(hb):
        cols = slice(hh * HEAD_DIM, (hh + 1) * HEAD_DIM)
        s = lax.dot_general(q_ref[:, cols], k_ref[:, cols], _NT, preferred_element_type=F32)
        p = jnp.exp2(s - jnp.max(s, axis=-1, keepdims=True))
        l = jnp.sum(p, axis=-1, keepdims=True)
        o = jnp.dot(p.astype(BF16), v_ref[:, cols], preferred_element_type=F32)
        o_ref[:, cols] = (o / l).astype(o_ref.dtype)


def _ctx_attention(qkv, o, *, n_batch, seq, ctx_len, n_heads, hb=2):
    hw = hb * HEAD_DIM
    hblocks = n_heads // hb
    row0 = n_batch * seq // ctx_len
    spec = lambda part: pl.BlockSpec((ctx_len, hw), lambda b, h: (row0 + b, part * hblocks + h))
    return pl.pallas_call(
        functools.partial(_ctx_attn_kernel, hb=hb),
        grid=(n_batch, hblocks),
        in_specs=[spec(0), spec(1), spec(2), pl.BlockSpec(memory_space=pl.ANY)],
        out_specs=spec(0),
        out_shape=jax.ShapeDtypeStruct(o.shape, o.dtype),
        input_output_aliases={3: 0},
        compiler_params=_params("arbitrary", "arbitrary"),
        name="ctx_attention",
    )(qkv, qkv, qkv, o)


def _gqa_kernel(q_ref, k_ref, vt_ref, kc_ref, vtc_ref, o_ref, m_scr, l_scr, acc_scr, s0_scr, s1_scr):
    n_chunks, _, tk = vt_ref.shape
    m_scr[...] = jnp.full(m_scr.shape, NEG_INF, F32)
    l_scr[...] = jnp.zeros(l_scr.shape, F32)
    acc_scr[...] = jnp.zeros(acc_scr.shape, F32)

    def scores(k, g):
        return jnp.dot(k, q_ref[g * HEAD_DIM:(g + 1) * HEAD_DIM, :], preferred_element_type=F32)

    def accumulate(s, vt, g):
        m_old = m_scr[g]
        m_new = jnp.maximum(m_old, jnp.max(s, axis=0, keepdims=True))
        alpha = jnp.exp2(m_old - m_new)
        p = jnp.exp2(s - m_new)
        l_scr[g] = alpha * l_scr[g] + jnp.sum(p, axis=0, keepdims=True)
        acc_scr[g] = alpha * acc_scr[g] + jnp.dot(vt, p.astype(BF16), preferred_element_type=F32)
        m_scr[g] = m_new

    def k_chunk(c):
        return k_ref[pl.ds(pl.multiple_of(c * tk, tk), tk), :]

    def stage(cur_scr, nxt_scr, c):
        k_next = k_chunk(c + 1)
        vt = vt_ref[c]
        for g in range(GQA_GROUP):
            nxt_scr[g] = scores(k_next, g)
            accumulate(cur_scr[g], vt, g)

    for g in range(GQA_GROUP):
        s0_scr[g] = scores(k_chunk(0), g)

    def body(j, carry):
        stage(s0_scr, s1_scr, 2 * j)
        stage(s1_scr, s0_scr, 2 * j + 1)
        return carry

    lax.fori_loop(0, n_chunks // 2 - 1, body, 0)
    stage(s0_scr, s1_scr, n_chunks - 2)
    vt_last = vt_ref[n_chunks - 1]
    for g in range(GQA_GROUP):
        s_ctx = scores(kc_ref[...], g)
        accumulate(s1_scr[g], vt_last, g)
        accumulate(s_ctx, vtc_ref[...], g)
    for g in range(GQA_GROUP):
        o_ref[:, g * HEAD_DIM:(g + 1) * HEAD_DIM] = (acc_scr[g] / l_scr[g]).T.astype(o_ref.dtype)


def _gqa_attention(qt, k, vt, *, n_batch, seq, ctx_len):
    n_kv = k.shape[1] // HEAD_DIM
    tq = qt.shape[2]
    tk = vt.shape[2]
    assert (seq // tk) % 2 == 0 and seq % tq == 0
    gw = GQA_GROUP * HEAD_DIM
    q_spec = pl.BlockSpec((tq, gw), lambda b, h, i: (b * (seq // tq) + i, h))
    return pl.pallas_call(
        _gqa_kernel,
        grid=(n_batch, n_kv, seq // tq),
        in_specs=[
            pl.BlockSpec((None, gw, tq), lambda b, h, i: (b * (seq // tq) + i, h, 0)),
            pl.BlockSpec((seq, HEAD_DIM), lambda b, h, i: (b, h)),
            pl.BlockSpec((seq // tk, HEAD_DIM, tk), lambda b, h, i: (b, h, 0)),
            pl.BlockSpec((ctx_len, HEAD_DIM), lambda b, h, i: (n_batch * seq // ctx_len + b, h)),
            pl.BlockSpec((None, HEAD_DIM, ctx_len), lambda b, h, i: (n_batch * seq // tk, h, b)),
        ],
        out_specs=q_spec,
        out_shape=jax.ShapeDtypeStruct((n_batch * seq, qt.shape[1]), BF16),
        scratch_shapes=[
            pltpu.VMEM((GQA_GROUP, 1, tq), F32),
            pltpu.VMEM((GQA_GROUP, 1, tq), F32),
            pltpu.VMEM((GQA_GROUP, HEAD_DIM, tq), F32),
            pltpu.VMEM((GQA_GROUP, tk, tq), F32),
            pltpu.VMEM((GQA_GROUP, tk, tq), F32),
        ],
        compiler_params=_params("arbitrary", "arbitrary", "arbitrary"),
        name="gqa_attention",
    )(qt, k, vt, k, vt)


def _rope_partner(v):
    q4 = HEAD_DIM // 4
    return jnp.concatenate([v[..., q4:2 * q4], v[..., :q4], v[..., 3 * q4:], v[..., 2 * q4:3 * q4]], axis=-1)


def _rope_tables_t(cos, sin, g, scale, rows, tm):
    cos_t = (cos[:rows] * (g * scale)[None, :]).reshape(rows // tm, tm, HEAD_DIM)
    sin_t = (sin[:rows] * (_rope_partner(g) * scale)[None, :]).reshape(rows // tm, tm, HEAD_DIM)
    return jnp.transpose(cos_t, (0, 2, 1)), jnp.transpose(sin_t, (0, 2, 1))


def _rope_tables(n_batch, seq, ctx_len):
    quarter = HEAD_DIM // 4
    t = jnp.arange(seq)
    freqs = ROPE_THETA ** (-jnp.arange(quarter, dtype=F32) / quarter)
    ang_r = (t // GRID_W).astype(F32)[:, None] * freqs[None, :]
    ang_c = (t % GRID_W).astype(F32)[:, None] * freqs[None, :]
    cos = jnp.concatenate([jnp.cos(ang_r)] * 2 + [jnp.cos(ang_c)] * 2, axis=-1)
    sin = jnp.concatenate([-jnp.sin(ang_r), jnp.sin(ang_r), -jnp.sin(ang_c), jnp.sin(ang_c)], axis=-1)
    n_ctx = n_batch * ctx_len
    cos = jnp.concatenate([jnp.tile(cos, (n_batch, 1)), jnp.ones((n_ctx, HEAD_DIM), F32)], axis=0)
    sin = jnp.concatenate([jnp.tile(sin, (n_batch, 1)), jnp.zeros((n_ctx, HEAD_DIM), F32)], axis=0)
    return cos, sin


def kernel(x, c, ctx, c_ctx, ada_w, ada_b, norm_g, na_wqkv, na_wo, na_rpb, gqa_wq, gqa_wkv, gqa_q_norm,
           gqa_k_norm, gqa_wo, ffn_w13, ffn_w2):
    n_batch, seq, d = x.shape
    ctx_len = ctx.shape[1]
    depth = ada_w.shape[0]
    assert depth == 2 and na_wqkv.shape[0] == 1 and gqa_wq.shape[0] == 1
    assert seq % GRID_W == 0 and n_batch + 1 <= MOD_ROWS
    n_heads = d // HEAD_DIM
    m_lat = n_batch * seq
    m_all = m_lat + n_batch * ctx_len
    scale = HEAD_DIM ** -0.5 * LOG2E
    tm_all = m_all // 8
    tm_lat = m_lat // 8
    tm_down = 2
    seg_tiles = lambda t: seq // t
    assert tm_all % 16 == 0 and tm_lat % 16 == 0

    cvec = jnp.zeros((MOD_ROWS, d), F32).at[:n_batch].set(c).at[n_batch].set(c_ctx)
    mod = _ada_mod(cvec, ada_w, ada_b).reshape(depth, MOD_ROWS, N_MOD, d)
    cos, sin = _rope_tables(n_batch, seq, ctx_len)
    cos_qt, sin_qt = _rope_tables_t(cos, sin, gqa_q_norm[0], scale, m_lat, tm_lat)

    tn_w = 512
    wqkv, wq, wkv = na_wqkv[0], gqa_wq[0], gqa_wkv[0]
    w2 = [ffn_w2[i].astype(BF16) for i in range(depth)]

    x_lat = x.reshape(m_lat, d)
    x_ctx = ctx.reshape(n_batch * ctx_len, d)
    seg_kw = dict(seg_tiles=seg_tiles, n_batch=n_batch)

    h = _prenorm(x_lat, x_ctx, norm_g[0], mod[0], g_row=0, sh_row=0, sc_row=1, **seg_kw)
    qkv = _matmul(h, wqkv, rows=m_all, n_out=wqkv.shape[1], out_dtype=BF16, tm=tm_all, tn=tn_w,
                  scale_blocks=d // tn_w, scale=scale, name="na_qkv")
    o = _na_attention(qkv, na_rpb[0], n_batch=n_batch, seq=seq, ctx_len=ctx_len, n_heads=n_heads)
    o = _ctx_attention(qkv, o, n_batch=n_batch, seq=seq, ctx_len=ctx_len, n_heads=n_heads)
    y = _matmul(o, na_wo[0], rows=m_all, n_out=d, out_dtype=F32, tm=tm_all, tn=tn_w, name="na_wo")
    xa, h = _resid(y, x_lat, norm_g[0], mod[0], x_ctx=x_ctx, rows=m_all, gt_row=2, gpost_row=1,
                   nxt=(2, 3, 4), g2=norm_g[0], mod2=mod[0], **seg_kw)
    gu = _gateup(h, ffn_w13, 0, rows=m_all, tm=tm_all, tn=256)
    y = _matmul(gu, w2[0], rows=m_all, n_out=d, out_dtype=F32, tm=tm_all // tm_down, tn=512, name="ffn_down")
    xa, h = _resid(y, xa, norm_g[0], mod[0], rows=m_all, gt_row=5, gpost_row=3,
                   nxt=(0, 0, 1), g2=norm_g[1], mod2=mod[1], **seg_kw)

    kv_w = wkv.shape[1] // 2
    qt = _matmul_rope_t(h, wq, cos_qt, sin_qt, rows=m_lat, n_out=d, tn=tn_w, name="gqa_q")
    k = _matmul_rope(h, wkv, gqa_k_norm[0], cos, sin, rows=m_all, n_out=kv_w, tm=tm_all, tn=tn_w, scale=1.0, name="gqa_k")
    gqa_tk = 512
    assert seq % gqa_tk == 0 and n_batch * ctx_len == gqa_tk
    vt = _matmul_t(h, wkv, rows=m_all, n_out=kv_w, tm=gqa_tk, tn=tn_w, col_blk_off=kv_w // tn_w, name="gqa_v")
    o = _gqa_attention(qt, k, vt, n_batch=n_batch, seq=seq, ctx_len=ctx_len)
    y = _matmul(o, gqa_wo[0], rows=m_lat, n_out=d, out_dtype=F32, tm=tm_lat, tn=tn_w, name="gqa_wo")
    xl, h = _resid(y, xa, norm_g[1], mod[1], rows=m_lat, gt_row=2, gpost_row=1,
                   nxt=(2, 3, 4), g2=norm_g[1], mod2=mod[1], **seg_kw)
    gu = _gateup(h, ffn_w13, 1, rows=m_lat, tm=tm_lat, tn=256)
    y = _matmul(gu, w2[1], rows=m_lat, n_out=d, out_dtype=F32, tm=tm_lat // tm_down, tn=512, name="ffn_down")
    xl = _resid(y, xl, norm_g[1], mod[1], rows=m_lat, gt_row=5, gpost_row=3, **seg_kw)
    return xl.reshape(n_batch, seq, d)
```

```python
import functools

import jax
import jax.numpy as jnp
from jax import lax
from jax.experimental import pallas as pl
from jax.experimental.pallas import tpu as pltpu

GRID_W = 64
NA_WIN_H = 8
NA_WIN_W = 16
HEAD_DIM = 128
GQA_GROUP = 4
ROPE_THETA = 10000.0
NORM_EPS = 1e-6
NEG_INF = -1e30
LOG2E = 1.4426950408889634
N_MOD = 6

VMEM_LIMIT_BYTES = 56 * 1024 * 1024
MXU_COLS = 256
MOD_ROWS = 8

F32 = jnp.float32
BF16 = jnp.bfloat16


def _params(*sem):
    return pltpu.CompilerParams(dimension_semantics=sem, vmem_limit_bytes=VMEM_LIMIT_BYTES)


def _rms(x, g):
    ms = jnp.mean(x * x, axis=-1, keepdims=True)
    return x * lax.rsqrt(ms + NORM_EPS) * g


def _seg_index(rows_per_seg_tiles, n_batch):
    return lambda i: jnp.minimum(i // rows_per_seg_tiles, n_batch)


def _ada_kernel(c_ref, w_ref, b_ref, o_ref):
    c = c_ref[...]
    s = (c * jax.nn.sigmoid(c)).astype(BF16)
    o_ref[...] = jnp.dot(s, w_ref[...].astype(BF16), preferred_element_type=F32) + b_ref[...]


def _ada_mod(cvec, ada_w, ada_b, tn=1024):
    depth, d, n = ada_w.shape
    return pl.pallas_call(
        _ada_kernel,
        grid=(depth, n // tn),
        in_specs=[
            pl.BlockSpec((MOD_ROWS, d), lambda l, j: (0, 0)),
            pl.BlockSpec((None, d, tn), lambda l, j: (l, 0, j)),
            pl.BlockSpec((None, 1, tn), lambda l, j: (l, 0, j)),
        ],
        out_specs=pl.BlockSpec((None, MOD_ROWS, tn), lambda l, j: (l, 0, j)),
        out_shape=jax.ShapeDtypeStruct((depth, MOD_ROWS, n), F32),
        compiler_params=_params("arbitrary", "arbitrary"),
        name="ada_mod",
    )(cvec, ada_w, ada_b.reshape(depth, 1, n))


def _token_rows(x_lat, x_ctx, rows, tm):
    d = x_lat.shape[1]
    n_lat = min(rows, x_lat.shape[0]) // tm
    if x_ctx is None:
        x_ctx = x_lat
        assert rows <= x_lat.shape[0]
    else:
        assert x_lat.shape[0] % tm == 0 and rows == x_lat.shape[0] + x_ctx.shape[0]
    specs = [pl.BlockSpec((tm, d), lambda i: (jnp.minimum(i, n_lat - 1), 0)),
             pl.BlockSpec((tm, d), lambda i: (jnp.maximum(i - n_lat, 0), 0))]
    return [x_lat, x_ctx], specs, n_lat


def _read_token_rows(xl_ref, xc_ref, n_lat):
    return jnp.where(pl.program_id(0) < n_lat, xl_ref[...], xc_ref[...])


def _prenorm_kernel(xl_ref, xc_ref, g_ref, mod_ref, h_ref, *, n_lat, g_row, sh_row, sc_row):
    y = _rms(_read_token_rows(xl_ref, xc_ref, n_lat), g_ref[g_row:g_row + 1, :])
    h = y * (1.0 + mod_ref[sc_row:sc_row + 1, :]) + mod_ref[sh_row:sh_row + 1, :]
    h_ref[...] = h.astype(h_ref.dtype)


def _prenorm(x_lat, x_ctx, g, mod, *, seg_tiles, n_batch, g_row, sh_row, sc_row, tm=256):
    d = x_lat.shape[1]
    m = x_lat.shape[0] + x_ctx.shape[0]
    seg = _seg_index(seg_tiles(tm), n_batch)
    x_args, x_specs, n_lat = _token_rows(x_lat, x_ctx, m, tm)
    return pl.pallas_call(
        functools.partial(_prenorm_kernel, n_lat=n_lat, g_row=g_row, sh_row=sh_row, sc_row=sc_row),
        grid=(m // tm,),
        in_specs=x_specs + [
            pl.BlockSpec(g.shape, lambda i: (0, 0)),
            pl.BlockSpec((None, N_MOD, d), lambda i: (seg(i), 0, 0)),
        ],
        out_specs=pl.BlockSpec((tm, d), lambda i: (i, 0)),
        out_shape=jax.ShapeDtypeStruct((m, d), BF16),
        compiler_params=_params("arbitrary"),
        name="prenorm",
    )(*x_args, g, mod)


def _resid_kernel(y_ref, xl_ref, xc_ref, g_ref, mod_ref, *rest, n_lat, gt_row, gpost_row, nxt):
    x = _read_token_rows(xl_ref, xc_ref, n_lat)
    xn = x + mod_ref[gt_row:gt_row + 1, :] * _rms(y_ref[...], g_ref[gpost_row:gpost_row + 1, :])
    if nxt is None:
        (xo_ref,) = rest
        xo_ref[...] = xn
        return
    g2_ref, mod2_ref, xo_ref, h_ref = rest
    gpre_row, sh_row, sc_row = nxt
    xo_ref[...] = xn
    h = _rms(xn, g2_ref[gpre_row:gpre_row + 1, :])
    h = h * (1.0 + mod2_ref[sc_row:sc_row + 1, :]) + mod2_ref[sh_row:sh_row + 1, :]
    h_ref[...] = h.astype(h_ref.dtype)


def _resid(y, x, g, mod, *, rows, seg_tiles, n_batch, gt_row, gpost_row, x_ctx=None, nxt=None, g2=None, mod2=None,
           tm=256):
    d = x.shape[1]
    seg = _seg_index(seg_tiles(tm), n_batch)
    row_spec = pl.BlockSpec((tm, d), lambda i: (i, 0))
    mod_spec = pl.BlockSpec((None, N_MOD, d), lambda i: (seg(i), 0, 0))
    x_args, x_specs, n_lat = _token_rows(x, x_ctx, rows, tm)
    in_specs = [row_spec] + x_specs + [pl.BlockSpec(g.shape, lambda i: (0, 0)), mod_spec]
    args = [y] + x_args + [g, mod]
    out_specs = [row_spec]
    out_shape = [jax.ShapeDtypeStruct((rows, d), F32)]
    if nxt is not None:
        in_specs += [pl.BlockSpec(g2.shape, lambda i: (0, 0)), mod_spec]
        args += [g2, mod2]
        out_specs.append(row_spec)
        out_shape.append(jax.ShapeDtypeStruct((rows, d), BF16))
    out = pl.pallas_call(
        functools.partial(_resid_kernel, n_lat=n_lat, gt_row=gt_row, gpost_row=gpost_row, nxt=nxt),
        grid=(rows // tm,),
        in_specs=in_specs,
        out_specs=out_specs,
        out_shape=out_shape,
        compiler_params=_params("arbitrary"),
        name="resid_norm",
    )(*args)
    return out if nxt is not None else out[0]


def _mm_kernel(x_ref, w_ref, o_ref, *, scale_blocks, scale):
    acc = jnp.dot(x_ref[...], w_ref[...].astype(BF16), preferred_element_type=F32)
    if scale_blocks:
        acc = acc * jnp.where(pl.program_id(1) < scale_blocks, scale, 1.0)
    o_ref[...] = acc.astype(o_ref.dtype)


def _matmul(x, w, *, rows, n_out, out_dtype, tm, tn, col_blk_off=0, scale_blocks=0, scale=1.0, name="matmul"):
    k = x.shape[1]
    return pl.pallas_call(
        functools.partial(_mm_kernel, scale_blocks=scale_blocks, scale=scale),
        grid=(rows // tm, n_out // tn),
        in_specs=[
            pl.BlockSpec((tm, k), lambda i, j: (i, 0)),
            pl.BlockSpec((k, tn), lambda i, j: (0, j + col_blk_off)),
        ],
        out_specs=pl.BlockSpec((tm, tn), lambda i, j: (i, j)),
        out_shape=jax.ShapeDtypeStruct((rows, n_out), out_dtype),
        compiler_params=_params("arbitrary", "arbitrary"),
        name=name,
    )(x, w)


def _mm_t_kernel(x_ref, w_ref, o_ref):
    acc = jnp.dot(x_ref[...], w_ref[...].astype(BF16), preferred_element_type=F32)
    o_ref[...] = acc.T.astype(o_ref.dtype)


def _matmul_t(x, w, *, rows, n_out, tm, tn, col_blk_off=0, name="matmul_t"):
    k = x.shape[1]
    return pl.pallas_call(
        _mm_t_kernel,
        grid=(rows // tm, n_out // tn),
        in_specs=[
            pl.BlockSpec((tm, k), lambda i, j: (i, 0)),
            pl.BlockSpec((k, tn), lambda i, j: (0, j + col_blk_off)),
        ],
        out_specs=pl.BlockSpec((None, tn, tm), lambda i, j: (i, j, 0)),
        out_shape=jax.ShapeDtypeStruct((rows // tm, n_out, tm), BF16),
        compiler_params=_params("arbitrary", "arbitrary"),
        name=name,
    )(x, w)


def _swap_halves(y):
    lane = lax.broadcasted_iota(jnp.int32, y.shape, 1)
    return jnp.where((lane & 32) == 0, pltpu.roll(y, 96, 1), pltpu.roll(y, 32, 1))


def _mm_rope_kernel(x_ref, w_ref, g_ref, cos_ref, sin_ref, o_ref, *, scale):
    cos = cos_ref[...]
    sin = sin_ref[...]
    g = g_ref[...]
    x = x_ref[...]
    for grp in range(w_ref.shape[1] // MXU_COLS):
        w = w_ref[:, grp * MXU_COLS:(grp + 1) * MXU_COLS].astype(BF16)
        acc = jnp.dot(x, w, preferred_element_type=F32)
        for hh in range(MXU_COLS // HEAD_DIM):
            y = _rms(acc[:, hh * HEAD_DIM:(hh + 1) * HEAD_DIM], g)
            y = y * cos + _swap_halves(y) * sin
            if scale != 1.0:
                y = y * scale
            col0 = grp * MXU_COLS + hh * HEAD_DIM
            o_ref[:, col0:col0 + HEAD_DIM] = y.astype(o_ref.dtype)


def _matmul_rope(x, w, g, cos, sin, *, rows, n_out, tm, tn, scale, name):
    k = x.shape[1]
    tab_spec = pl.BlockSpec((tm, HEAD_DIM), lambda i, j: (i, 0))
    return pl.pallas_call(
        functools.partial(_mm_rope_kernel, scale=scale),
        grid=(rows // tm, n_out // tn),
        in_specs=[
            pl.BlockSpec((tm, k), lambda i, j: (i, 0)),
            pl.BlockSpec((k, tn), lambda i, j: (0, j)),
            pl.BlockSpec((1, HEAD_DIM), lambda i, j: (0, 0)),
            tab_spec,
            tab_spec,
        ],
        out_specs=pl.BlockSpec((tm, tn), lambda i, j: (i, j)),
        out_shape=jax.ShapeDtypeStruct((rows, n_out), BF16),
        compiler_params=_params("arbitrary", "arbitrary"),
        name=name,
    )(x, w, g.reshape(1, HEAD_DIM), cos, sin)


def _mm_rope_t_kernel(x_ref, w_ref, cos_ref, sin_ref, o_ref):
    cos = cos_ref[...]
    sin = sin_ref[...]
    x = x_ref[...]
    q4 = HEAD_DIM // 4
    for grp in range(w_ref.shape[1] // MXU_COLS):
        w = w_ref[:, grp * MXU_COLS:(grp + 1) * MXU_COLS].astype(BF16)
        acc_t = jnp.dot(x, w, preferred_element_type=F32).T
        for hh in range(MXU_COLS // HEAD_DIM):
            y = acc_t[hh * HEAD_DIM:(hh + 1) * HEAD_DIM, :]
            r = lax.rsqrt(jnp.mean(y * y, axis=0, keepdims=True) + NORM_EPS)
            partner = jnp.concatenate([y[q4:2 * q4], y[:q4], y[3 * q4:], y[2 * q4:3 * q4]], axis=0)
            row0 = grp * MXU_COLS + hh * HEAD_DIM
            o_ref[row0:row0 + HEAD_DIM, :] = ((y * cos + partner * sin) * r).astype(o_ref.dtype)


def _matmul_rope_t(x, w, cos_t, sin_t, *, rows, n_out, tn, name):
    k = x.shape[1]
    n_tiles, _, tm = cos_t.shape
    assert n_tiles * tm == rows
    tab_spec = pl.BlockSpec((None, HEAD_DIM, tm), lambda i, j: (i, 0, 0))
    return pl.pallas_call(
        _mm_rope_t_kernel,
        grid=(n_tiles, n_out // tn),
        in_specs=[
            pl.BlockSpec((tm, k), lambda i, j: (i, 0)),
            pl.BlockSpec((k, tn), lambda i, j: (0, j)),
            tab_spec,
            tab_spec,
        ],
        out_specs=pl.BlockSpec((None, tn, tm), lambda i, j: (i, j, 0)),
        out_shape=jax.ShapeDtypeStruct((n_tiles, n_out, tm), BF16),
        compiler_params=_params("arbitrary", "arbitrary"),
        name=name,
    )(x, w, cos_t, sin_t)


def _gateup_kernel(x_ref, w1_ref, w3_ref, w2_ref, o_ref, w2b_ref):
    x = x_ref[...]
    a = jnp.dot(x, w1_ref[...].astype(BF16), preferred_element_type=F32)
    b = jnp.dot(x, w3_ref[...].astype(BF16), preferred_element_type=F32)
    o_ref[...] = (a * jax.nn.sigmoid(a) * b).astype(o_ref.dtype)
    w2b_ref[...] = w2_ref[...].astype(w2b_ref.dtype)


def _gateup(x, w13, w2, layer, *, rows, tm, tn):
    k = x.shape[1]
    f = w13.shape[2] // 2
    n_j = f // tn
    steps = (rows // tm) * n_j
    slab = w2.shape[1] // steps
    assert slab * steps == w2.shape[1] and slab % 16 == 0
    return pl.pallas_call(
        _gateup_kernel,
        grid=(rows // tm, n_j),
        in_specs=[
            pl.BlockSpec((tm, k), lambda i, j: (i, 0)),
            pl.BlockSpec((None, k, tn), lambda i, j: (layer, 0, j)),
            pl.BlockSpec((None, k, tn), lambda i, j: (layer, 0, j + n_j)),
            pl.BlockSpec((None, slab, w2.shape[2]), lambda i, j: (layer, i * n_j + j, 0)),
        ],
        out_specs=[
            pl.BlockSpec((tm, tn), lambda i, j: (i, j)),
            pl.BlockSpec((slab, w2.shape[2]), lambda i, j: (i * n_j + j, 0)),
        ],
        out_shape=[
            jax.ShapeDtypeStruct((rows, f), BF16),
            jax.ShapeDtypeStruct(w2.shape[1:], BF16),
        ],
        compiler_params=_params("arbitrary", "arbitrary"),
        name="ffn_gateup",
    )(x, w13, w13, w2)


_NT = (((1,), (1,)), ((), ()))


def _na_block_rows(rb):
    return rb + NA_WIN_H


N_DR = 2 * NA_WIN_H - 1
N_DC = 2 * NA_WIN_W - 1
_TAB_BOTH, _TAB_FIRST, _TAB_SECOND, _TAB_NONE = 0, N_DR + 1, 2 * N_DR + 1, 3 * N_DR + 1
_TAB_SIZE = 3 * N_DR + 2


def _na_build_bias_tiles(rpb_ref, tab_scr, h0, hb):
    shape = (GRID_W, 2 * GRID_W)
    c = lax.broadcasted_iota(jnp.int32, shape, 0)
    lane = lax.broadcasted_iota(jnp.int32, shape, 1)
    kc = lane & (GRID_W - 1)
    second = lane >= GRID_W
    cs = jnp.clip(c - NA_WIN_W // 2, 0, GRID_W - NA_WIN_W)
    in_win = (kc >= cs) & (kc < cs + NA_WIN_W)
    dci = kc - c + (NA_WIN_W - 1)
    is_dc = [dci == k for k in range(N_DC)]
    neg = jnp.full(shape, NEG_INF, F32)
    for hh in range(hb):
        base = (h0 + hh) * (N_DR * N_DC)
        rows = []
        for d in range(N_DR):
            t = neg
            for k in range(N_DC):
                t = jnp.where(is_dc[k], rpb_ref[base + d * N_DC + k] * LOG2E, t)
            rows.append(jnp.where(in_win, t, NEG_INF))
        for d in range(N_DR + 1):
            lo = rows[d - 1] if d >= 1 else neg
            hi = rows[d] if d < N_DR else neg
            tab_scr[hh, _TAB_BOTH + d] = jnp.where(second, hi, lo)
        for d in range(N_DR):
            tab_scr[hh, _TAB_FIRST + d] = jnp.where(second, neg, rows[d])
            tab_scr[hh, _TAB_SECOND + d] = jnp.where(second, rows[d], neg)
        tab_scr[hh, _TAB_NONE] = neg


def _na_kernel(rpb_ref, q_ref, k_ref, v_ref, kc_ref, vc_ref, o_ref, tab_scr, *, rb, hb, nsb, rows):
    kr = _na_block_rows(rb)
    qn = rb * GRID_W
    half = NA_WIN_H // 2

    @pl.when(pl.program_id(2) == 0)
    def _():
        _na_build_bias_tiles(rpb_ref, tab_scr, pl.program_id(1) * hb, hb)

    starts, tiles = [], []
    for sb in range(nsb):
        r0 = (pl.program_id(2) * nsb + sb) * rb
        kstart = jnp.clip(r0 - half, 0, rows - kr)
        starts.append(pl.multiple_of(kstart * GRID_W, GRID_W))
        tile_idx = []
        for i in range(rb):
            r = r0 + i
            rs = jnp.clip(r - half, 0, rows - NA_WIN_H)
            row_idx = []
            for jp in range(kr // 2):
                k0 = kstart + 2 * jp
                v0 = (k0 >= rs) & (k0 < rs + NA_WIN_H)
                v1 = (k0 + 1 >= rs) & (k0 + 1 < rs + NA_WIN_H)
                d0 = k0 - r + NA_WIN_H - 1
                idx = jnp.where(v0 & v1, _TAB_BOTH + d0 + 1,
                                jnp.where(v0, _TAB_FIRST + d0, jnp.where(v1, _TAB_SECOND + d0 + 1, _TAB_NONE)))
                row_idx.append(jnp.clip(idx, 0, _TAB_SIZE - 1))
            tile_idx.append(row_idx)
        tiles.append(tile_idx)

    def scores(sb, hh):
        cols = slice(hh * HEAD_DIM, (hh + 1) * HEAD_DIM)
        q = q_ref[sb * qn:(sb + 1) * qn, cols]
        ku = k_ref[pl.ds(starts[sb], kr * GRID_W), cols]
        bias = jnp.concatenate(
            [jnp.concatenate([tab_scr[hh, idx] for idx in row_idx], axis=1) for row_idx in tiles[sb]], axis=0)
        s_loc = lax.dot_general(q, ku, _NT, preferred_element_type=F32) + bias
        s_ctx = lax.dot_general(q, kc_ref[:, cols], _NT, preferred_element_type=F32)
        return s_loc, s_ctx

    def finish(sb, hh, s_loc, s_ctx):
        cols = slice(hh * HEAD_DIM, (hh + 1) * HEAD_DIM)
        vu = v_ref[pl.ds(starts[sb], kr * GRID_W), cols]
        m = jnp.maximum(jnp.max(s_loc, axis=-1, keepdims=True), jnp.max(s_ctx, axis=-1, keepdims=True))
        p_loc = jnp.exp2(s_loc - m)
        p_ctx = jnp.exp2(s_ctx - m)
        l = jnp.sum(p_loc, axis=-1, keepdims=True) + jnp.sum(p_ctx, axis=-1, keepdims=True)
        o = (jnp.dot(p_loc.astype(BF16), vu, preferred_element_type=F32)
             + jnp.dot(p_ctx.astype(BF16), vc_ref[:, cols], preferred_element_type=F32))
        o_ref[sb * qn:(sb + 1) * qn, cols] = (o / l).astype(o_ref.dtype)

    pairs = [(sb, hh) for sb in range(nsb) for hh in range(hb)]
    ahead = 1
    pending = [scores(*pair) for pair in pairs[:ahead]]
    for n, pair in enumerate(pairs):
        if n + ahead < len(pairs):
            pending.append(scores(*pairs[n + ahead]))
        finish(*pair, *pending.pop(0))


def _na_attention(qkv, rpb, *, n_batch, seq, ctx_len, n_heads, rb=4, hb=2, nsb=4):
    m_all = qkv.shape[0]
    rows = seq // GRID_W
    assert rpb.shape == (n_heads, N_DR, N_DC) and _na_block_rows(rb) % 2 == 0 and rows % (rb * nsb) == 0
    hw = hb * HEAD_DIM
    hblocks = n_heads // hb
    qrows = nsb * rb * GRID_W
    lat_spec = lambda part: pl.BlockSpec((seq, hw), lambda b, h, r, _: (b, part * hblocks + h))
    ctx_spec = lambda part: pl.BlockSpec((ctx_len, hw),
                                         lambda b, h, r, _: (n_batch * seq // ctx_len + b, part * hblocks + h))
    q_spec = pl.BlockSpec((qrows, hw), lambda b, h, r, _: (b * (seq // qrows) + r, h))
    return pl.pallas_call(
        functools.partial(_na_kernel, rb=rb, hb=hb, nsb=nsb, rows=rows),
        grid_spec=pltpu.PrefetchScalarGridSpec(
            num_scalar_prefetch=1,
            grid=(n_batch, hblocks, rows // (rb * nsb)),
            in_specs=[q_spec, lat_spec(1), lat_spec(2), ctx_spec(1), ctx_spec(2)],
            out_specs=q_spec,
            scratch_shapes=[pltpu.VMEM((hb, _TAB_SIZE, GRID_W, 2 * GRID_W), F32)],
        ),
        out_shape=jax.ShapeDtypeStruct((m_all, n_heads * HEAD_DIM), BF16),
        compiler_params=_params("arbitrary", "arbitrary", "arbitrary"),
        name="na_attention",
    )(rpb.reshape(-1), qkv, qkv, qkv, qkv, qkv)


def _ctx_attn_kernel(q_ref, k_ref, v_ref, o_in_ref, o_ref, *, hb):
    del o_in_ref
    for hh in range(hb):
        cols = slice(hh * HEAD_DIM, (hh + 1) * HEAD_DIM)
        s = lax.dot_general(q_ref[:, cols], k_ref[:, cols], _NT, preferred_element_type=F32)
        p = jnp.exp2(s - jnp.max(s, axis=-1, keepdims=True))
        l = jnp.sum(p, axis=-1, keepdims=True)
        o = jnp.dot(p.astype(BF16), v_ref[:, cols], preferred_element_type=F32)
        o_ref[:, cols] = (o / l).astype(o_ref.dtype)


def _ctx_attention(qkv, o, *, n_batch, seq, ctx_len, n_heads, hb=2):
    hw = hb * HEAD_DIM
    hblocks = n_heads // hb
    row0 = n_batch * seq // ctx_len
    spec = lambda part: pl.BlockSpec((ctx_len, hw), lambda b, h: (row0 + b, part * hblocks + h))
    return pl.pallas_call(
        functools.partial(_ctx_attn_kernel, hb=hb),
        grid=(n_batch, hblocks),
        in_specs=[spec(0), spec(1), spec(2), pl.BlockSpec(memory_space=pl.ANY)],
        out_specs=spec(0),
        out_shape=jax.ShapeDtypeStruct(o.shape, o.dtype),
        input_output_aliases={3: 0},
        compiler_params=_params("arbitrary", "arbitrary"),
        name="ctx_attention",
    )(qkv, qkv, qkv, o)


def _gqa_kernel(q_ref, k_ref, vt_ref, kc_ref, vtc_ref, o_ref, m_scr, l_scr, acc_scr, s0_scr, s1_scr):
    n_chunks, _, tk = vt_ref.shape
    m_scr[...] = jnp.full(m_scr.shape, NEG_INF, F32)
    l_scr[...] = jnp.zeros(l_scr.shape, F32)
    acc_scr[...] = jnp.zeros(acc_scr.shape, F32)

    def scores(k, g):
        return jnp.dot(k, q_ref[g * HEAD_DIM:(g + 1) * HEAD_DIM, :], preferred_element_type=F32)

    def accumulate(s, vt, g):
        m_old = m_scr[g]
        m_new = jnp.maximum(m_old, jnp.max(s, axis=0, keepdims=True))
        alpha = jnp.exp2(m_old - m_new)
        p = jnp.exp2(s - m_new)
        l_scr[g] = alpha * l_scr[g] + jnp.sum(p, axis=0, keepdims=True)
        acc_scr[g] = alpha * acc_scr[g] + jnp.dot(vt, p.astype(BF16), preferred_element_type=F32)
        m_scr[g] = m_new

    def k_chunk(c):
        return k_ref[pl.ds(pl.multiple_of(c * tk, tk), tk), :]

    def stage(cur_scr, nxt_scr, c):
        k_next = k_chunk(c + 1)
        vt = vt_ref[c]
        for g in range(GQA_GROUP):
            nxt_scr[g] = scores(k_next, g)
            accumulate(cur_scr[g], vt, g)

    for g in range(GQA_GROUP):
        s0_scr[g] = scores(k_chunk(0), g)

    def body(j, carry):
        stage(s0_scr, s1_scr, 2 * j)
        stage(s1_scr, s0_scr, 2 * j + 1)
        return carry

    lax.fori_loop(0, n_chunks // 2 - 1, body, 0)
    stage(s0_scr, s1_scr, n_chunks - 2)
    vt_last = vt_ref[n_chunks - 1]
    for g in range(GQA_GROUP):
        s_ctx = scores(kc_ref[...], g)
        accumulate(s1_scr[g], vt_last, g)
        accumulate(s_ctx, vtc_ref[...], g)
    for g in range(GQA_GROUP):
        o_ref[:, g * HEAD_DIM:(g + 1) * HEAD_DIM] = (acc_scr[g] / l_scr[g]).T.astype(o_ref.dtype)


def _gqa_attention(qt, k, vt, *, n_batch, seq, ctx_len):
    n_kv = k.shape[1] // HEAD_DIM
    tq = qt.shape[2]
    tk = vt.shape[2]
    assert (seq // tk) % 2 == 0 and seq % tq == 0
    gw = GQA_GROUP * HEAD_DIM
    q_spec = pl.BlockSpec((tq, gw), lambda b, h, i: (b * (seq // tq) + i, h))
    return pl.pallas_call(
        _gqa_kernel,
        grid=(n_batch, n_kv, seq // tq),
        in_specs=[
            pl.BlockSpec((None, gw, tq), lambda b, h, i: (b * (seq // tq) + i, h, 0)),
            pl.BlockSpec((seq, HEAD_DIM), lambda b, h, i: (b, h)),
            pl.BlockSpec((seq // tk, HEAD_DIM, tk), lambda b, h, i: (b, h, 0)),
            pl.BlockSpec((ctx_len, HEAD_DIM), lambda b, h, i: (n_batch * seq // ctx_len + b, h)),
            pl.BlockSpec((None, HEAD_DIM, ctx_len), lambda b, h, i: (n_batch * seq // tk, h, b)),
        ],
        out_specs=q_spec,
        out_shape=jax.ShapeDtypeStruct((n_batch * seq, qt.shape[1]), BF16),
        scratch_shapes=[
            pltpu.VMEM((GQA_GROUP, 1, tq), F32),
            pltpu.VMEM((GQA_GROUP, 1, tq), F32),
            pltpu.VMEM((GQA_GROUP, HEAD_DIM, tq), F32),
            pltpu.VMEM((GQA_GROUP, tk, tq), F32),
            pltpu.VMEM((GQA_GROUP, tk, tq), F32),
        ],
        compiler_params=_params("arbitrary", "arbitrary", "arbitrary"),
        name="gqa_attention",
    )(qt, k, vt, k, vt)


def _rope_partner(v):
    q4 = HEAD_DIM // 4
    return jnp.concatenate([v[..., q4:2 * q4], v[..., :q4], v[..., 3 * q4:], v[..., 2 * q4:3 * q4]], axis=-1)


def _rope_tables_t(cos, sin, g, scale, rows, tm):
    cos_t = (cos[:rows] * (g * scale)[None, :]).reshape(rows // tm, tm, HEAD_DIM)
    sin_t = (sin[:rows] * (_rope_partner(g) * scale)[None, :]).reshape(rows // tm, tm, HEAD_DIM)
    return jnp.transpose(cos_t, (0, 2, 1)), jnp.transpose(sin_t, (0, 2, 1))


def _rope_tables(n_batch, seq, ctx_len):
    quarter = HEAD_DIM // 4
    t = jnp.arange(seq)
    freqs = ROPE_THETA ** (-jnp.arange(quarter, dtype=F32) / quarter)
    ang_r = (t // GRID_W).astype(F32)[:, None] * freqs[None, :]
    ang_c = (t % GRID_W).astype(F32)[:, None] * freqs[None, :]
    cos = jnp.concatenate([jnp.cos(ang_r)] * 2 + [jnp.cos(ang_c)] * 2, axis=-1)
    sin = jnp.concatenate([-jnp.sin(ang_r), jnp.sin(ang_r), -jnp.sin(ang_c), jnp.sin(ang_c)], axis=-1)
    n_ctx = n_batch * ctx_len
    cos = jnp.concatenate([jnp.tile(cos, (n_batch, 1)), jnp.ones((n_ctx, HEAD_DIM), F32)], axis=0)
    sin = jnp.concatenate([jnp.tile(sin, (n_batch, 1)), jnp.zeros((n_ctx, HEAD_DIM), F32)], axis=0)
    return cos, sin


def kernel(x, c, ctx, c_ctx, ada_w, ada_b, norm_g, na_wqkv, na_wo, na_rpb, gqa_wq, gqa_wkv, gqa_q_norm,
           gqa_k_norm, gqa_wo, ffn_w13, ffn_w2):
    n_batch, seq, d = x.shape
    ctx_len = ctx.shape[1]
    depth = ada_w.shape[0]
    assert depth == 2 and na_wqkv.shape[0] == 1 and gqa_wq.shape[0] == 1
    assert seq % GRID_W == 0 and n_batch + 1 <= MOD_ROWS
    n_heads = d // HEAD_DIM
    m_lat = n_batch * seq
    m_all = m_lat + n_batch * ctx_len
    scale = HEAD_DIM ** -0.5 * LOG2E
    tm_all = m_all // 8
    tm_lat = m_lat // 8
    tm_down = 2
    seg_tiles = lambda t: seq // t
    assert tm_all % 16 == 0 and tm_lat % 16 == 0

    cvec = jnp.zeros((MOD_ROWS, d), F32).at[:n_batch].set(c).at[n_batch].set(c_ctx)
    mod = _ada_mod(cvec, ada_w, ada_b).reshape(depth, MOD_ROWS, N_MOD, d)
    cos, sin = _rope_tables(n_batch, seq, ctx_len)
    cos_qt, sin_qt = _rope_tables_t(cos, sin, gqa_q_norm[0], scale, m_lat, tm_lat)

    tn_w = 512
    wqkv, wq, wkv = na_wqkv[0], gqa_wq[0], gqa_wkv[0]

    x_lat = x.reshape(m_lat, d)
    x_ctx = ctx.reshape(n_batch * ctx_len, d)
    seg_kw = dict(seg_tiles=seg_tiles, n_batch=n_batch)

    h = _prenorm(x_lat, x_ctx, norm_g[0], mod[0], g_row=0, sh_row=0, sc_row=1, **seg_kw)
    qkv = _matmul(h, wqkv, rows=m_all, n_out=wqkv.shape[1], out_dtype=BF16, tm=tm_all, tn=tn_w,
                  scale_blocks=d // tn_w, scale=scale, name="na_qkv")
    o = _na_attention(qkv, na_rpb[0], n_batch=n_batch, seq=seq, ctx_len=ctx_len, n_heads=n_heads)
    o = _ctx_attention(qkv, o, n_batch=n_batch, seq=seq, ctx_len=ctx_len, n_heads=n_heads)
    y = _matmul(o, na_wo[0], rows=m_all, n_out=d, out_dtype=F32, tm=tm_all, tn=tn_w, name="na_wo")
    xa, h = _resid(y, x_lat, norm_g[0], mod[0], x_ctx=x_ctx, rows=m_all, gt_row=2, gpost_row=1,
                   nxt=(2, 3, 4), g2=norm_g[0], mod2=mod[0], **seg_kw)
    gu, w2 = _gateup(h, ffn_w13, ffn_w2, 0, rows=m_all, tm=tm_all, tn=256)
    y = _matmul(gu, w2, rows=m_all, n_out=d, out_dtype=F32, tm=tm_all // tm_down, tn=512, name="ffn_down")
    xa, h = _resid(y, xa, norm_g[0], mod[0], rows=m_all, gt_row=5, gpost_row=3,
                   nxt=(0, 0, 1), g2=norm_g[1], mod2=mod[1], **seg_kw)

    kv_w = wkv.shape[1] // 2
    qt = _matmul_rope_t(h, wq, cos_qt, sin_qt, rows=m_lat, n_out=d, tn=tn_w, name="gqa_q")
    k = _matmul_rope(h, wkv, gqa_k_norm[0], cos, sin, rows=m_all, n_out=kv_w, tm=tm_all, tn=tn_w, scale=1.0, name="gqa_k")
    gqa_tk = 512
    assert seq % gqa_tk == 0 and n_batch * ctx_len == gqa_tk
    vt = _matmul_t(h, wkv, rows=m_all, n_out=kv_w, tm=gqa_tk, tn=tn_w, col_blk_off=kv_w // tn_w, name="gqa_v")
    o = _gqa_attention(qt, k, vt, n_batch=n_batch, seq=seq, ctx_len=ctx_len)
    y = _matmul(o, gqa_wo[0], rows=m_lat, n_out=d, out_dtype=F32, tm=tm_lat, tn=tn_w, name="gqa_wo")
    xl, h = _resid(y, xa, norm_g[1], mod[1], rows=m_lat, gt_row=2, gpost_row=1,
                   nxt=(2, 3, 4), g2=norm_g[1], mod2=mod[1], **seg_kw)
    gu, w2 = _gateup(h, ffn_w13, ffn_w2, 1, rows=m_lat, tm=tm_lat, tn=256)
    y = _matmul(gu, w2, rows=m_lat, n_out=d, out_dtype=F32, tm=tm_lat // tm_down, tn=512, name="ffn_down")
    xl = _resid(y, xl, norm_g[1], mod[1], rows=m_lat, gt_row=5, gpost_row=3, **seg_kw)
    return xl.reshape(n_batch, seq, d)
```

```python
import functools

import jax
import jax.numpy as jnp
from jax import lax
from jax.experimental import pallas as pl
from jax.experimental.pallas import tpu as pltpu

GRID_W = 64
NA_WIN_H = 8
NA_WIN_W = 16
HEAD_DIM = 128
GQA_GROUP = 4
ROPE_THETA = 10000.0
NORM_EPS = 1e-6
NEG_INF = -1e30
LOG2E = 1.4426950408889634
N_MOD = 6

VMEM_LIMIT_BYTES = 56 * 1024 * 1024
MXU_COLS = 256
MOD_ROWS = 8

F32 = jnp.float32
BF16 = jnp.bfloat16


def _params(*sem):
    return pltpu.CompilerParams(dimension_semantics=sem, vmem_limit_bytes=VMEM_LIMIT_BYTES)


def _rms(x, g):
    ms = jnp.mean(x * x, axis=-1, keepdims=True)
    return x * lax.rsqrt(ms + NORM_EPS) * g


def _seg_index(rows_per_seg_tiles, n_batch):
    return lambda i: jnp.minimum(i // rows_per_seg_tiles, n_batch)


def _ada_kernel(c_ref, w_ref, b_ref, o_ref):
    c = c_ref[...]
    s = (c * jax.nn.sigmoid(c)).astype(BF16)
    o_ref[...] = jnp.dot(s, w_ref[...].astype(BF16), preferred_element_type=F32) + b_ref[...]


def _ada_mod(cvec, ada_w, ada_b, tn=1024):
    depth, d, n = ada_w.shape
    return pl.pallas_call(
        _ada_kernel,
        grid=(depth, n // tn),
        in_specs=[
            pl.BlockSpec((MOD_ROWS, d), lambda l, j: (0, 0)),
            pl.BlockSpec((None, d, tn), lambda l, j: (l, 0, j)),
            pl.BlockSpec((None, 1, tn), lambda l, j: (l, 0, j)),
        ],
        out_specs=pl.BlockSpec((None, MOD_ROWS, tn), lambda l, j: (l, 0, j)),
        out_shape=jax.ShapeDtypeStruct((depth, MOD_ROWS, n), F32),
        compiler_params=_params("arbitrary", "arbitrary"),
        name="ada_mod",
    )(cvec, ada_w, ada_b.reshape(depth, 1, n))


def _token_rows(x_lat, x_ctx, rows, tm):
    d = x_lat.shape[1]
    n_lat = min(rows, x_lat.shape[0]) // tm
    if x_ctx is None:
        x_ctx = x_lat
        assert rows <= x_lat.shape[0]
    else:
        assert x_lat.shape[0] % tm == 0 and rows == x_lat.shape[0] + x_ctx.shape[0]
    specs = [pl.BlockSpec((tm, d), lambda i: (jnp.minimum(i, n_lat - 1), 0)),
             pl.BlockSpec((tm, d), lambda i: (jnp.maximum(i - n_lat, 0), 0))]
    return [x_lat, x_ctx], specs, n_lat


def _read_token_rows(xl_ref, xc_ref, n_lat):
    return jnp.where(pl.program_id(0) < n_lat, xl_ref[...], xc_ref[...])


def _prenorm_kernel(xl_ref, xc_ref, g_ref, mod_ref, h_ref, *, n_lat, g_row, sh_row, sc_row):
    y = _rms(_read_token_rows(xl_ref, xc_ref, n_lat), g_ref[g_row:g_row + 1, :])
    h = y * (1.0 + mod_ref[sc_row:sc_row + 1, :]) + mod_ref[sh_row:sh_row + 1, :]
    h_ref[...] = h.astype(h_ref.dtype)


def _prenorm(x_lat, x_ctx, g, mod, *, seg_tiles, n_batch, g_row, sh_row, sc_row, tm=256):
    d = x_lat.shape[1]
    m = x_lat.shape[0] + x_ctx.shape[0]
    seg = _seg_index(seg_tiles(tm), n_batch)
    x_args, x_specs, n_lat = _token_rows(x_lat, x_ctx, m, tm)
    return pl.pallas_call(
        functools.partial(_prenorm_kernel, n_lat=n_lat, g_row=g_row, sh_row=sh_row, sc_row=sc_row),
        grid=(m // tm,),
        in_specs=x_specs + [
            pl.BlockSpec(g.shape, lambda i: (0, 0)),
            pl.BlockSpec((None, N_MOD, d), lambda i: (seg(i), 0, 0)),
        ],
        out_specs=pl.BlockSpec((tm, d), lambda i: (i, 0)),
        out_shape=jax.ShapeDtypeStruct((m, d), BF16),
        compiler_params=_params("arbitrary"),
        name="prenorm",
    )(*x_args, g, mod)


def _resid_kernel(y_ref, xl_ref, xc_ref, g_ref, mod_ref, *rest, n_lat, gt_row, gpost_row, nxt):
    x = _read_token_rows(xl_ref, xc_ref, n_lat)
    xn = x + mod_ref[gt_row:gt_row + 1, :] * _rms(y_ref[...], g_ref[gpost_row:gpost_row + 1, :])
    if nxt is None:
        (xo_ref,) = rest
        xo_ref[...] = xn
        return
    g2_ref, mod2_ref, xo_ref, h_ref = rest
    gpre_row, sh_row, sc_row = nxt
    xo_ref[...] = xn
    h = _rms(xn, g2_ref[gpre_row:gpre_row + 1, :])
    h = h * (1.0 + mod2_ref[sc_row:sc_row + 1, :]) + mod2_ref[sh_row:sh_row + 1, :]
    h_ref[...] = h.astype(h_ref.dtype)


def _resid(y, x, g, mod, *, rows, seg_tiles, n_batch, gt_row, gpost_row, x_ctx=None, nxt=None, g2=None, mod2=None,
           tm=256):
    d = x.shape[1]
    seg = _seg_index(seg_tiles(tm), n_batch)
    row_spec = pl.BlockSpec((tm, d), lambda i: (i, 0))
    mod_spec = pl.BlockSpec((None, N_MOD, d), lambda i: (seg(i), 0, 0))
    x_args, x_specs, n_lat = _token_rows(x, x_ctx, rows, tm)
    in_specs = [row_spec] + x_specs + [pl.BlockSpec(g.shape, lambda i: (0, 0)), mod_spec]
    args = [y] + x_args + [g, mod]
    out_specs = [row_spec]
    out_shape = [jax.ShapeDtypeStruct((rows, d), F32)]
    if nxt is not None:
        in_specs += [pl.BlockSpec(g2.shape, lambda i: (0, 0)), mod_spec]
        args += [g2, mod2]
        out_specs.append(row_spec)
        out_shape.append(jax.ShapeDtypeStruct((rows, d), BF16))
    out = pl.pallas_call(
        functools.partial(_resid_kernel, n_lat=n_lat, gt_row=gt_row, gpost_row=gpost_row, nxt=nxt),
        grid=(rows // tm,),
        in_specs=in_specs,
        out_specs=out_specs,
        out_shape=out_shape,
        compiler_params=_params("arbitrary"),
        name="resid_norm",
    )(*args)
    return out if nxt is not None else out[0]


def _mm_kernel(x_ref, w_ref, o_ref, *, scale_blocks, scale):
    acc = jnp.dot(x_ref[...], w_ref[...].astype(BF16), preferred_element_type=F32)
    if scale_blocks:
        acc = acc * jnp.where(pl.program_id(1) < scale_blocks, scale, 1.0)
    o_ref[...] = acc.astype(o_ref.dtype)


def _matmul(x, w, *, rows, n_out, out_dtype, tm, tn, col_blk_off=0, scale_blocks=0, scale=1.0, name="matmul"):
    k = x.shape[1]
    return pl.pallas_call(
        functools.partial(_mm_kernel, scale_blocks=scale_blocks, scale=scale),
        grid=(rows // tm, n_out // tn),
        in_specs=[
            pl.BlockSpec((tm, k), lambda i, j: (i, 0)),
            pl.BlockSpec((k, tn), lambda i, j: (0, j + col_blk_off)),
        ],
        out_specs=pl.BlockSpec((tm, tn), lambda i, j: (i, j)),
        out_shape=jax.ShapeDtypeStruct((rows, n_out), out_dtype),
        compiler_params=_params("arbitrary", "arbitrary"),
        name=name,
    )(x, w)


def _mm_t_kernel(x_ref, w_ref, o_ref):
    acc = jnp.dot(x_ref[...], w_ref[...].astype(BF16), preferred_element_type=F32)
    o_ref[...] = acc.T.astype(o_ref.dtype)


def _matmul_t(x, w, *, rows, n_out, tm, tn, col_blk_off=0, name="matmul_t"):
    k = x.shape[1]
    return pl.pallas_call(
        _mm_t_kernel,
        grid=(n_out // tn, rows // tm),
        in_specs=[
            pl.BlockSpec((tm, k), lambda j, i: (i, 0)),
            pl.BlockSpec((k, tn), lambda j, i: (0, j + col_blk_off)),
        ],
        out_specs=pl.BlockSpec((None, tn, tm), lambda j, i: (i, j, 0)),
        out_shape=jax.ShapeDtypeStruct((rows // tm, n_out, tm), BF16),
        compiler_params=_params("arbitrary", "arbitrary"),
        name=name,
    )(x, w)


def _swap_halves(y):
    lane = lax.broadcasted_iota(jnp.int32, y.shape, 1)
    return jnp.where((lane & 32) == 0, pltpu.roll(y, 96, 1), pltpu.roll(y, 32, 1))


def _mm_rope_kernel(x_ref, w_ref, g_ref, cos_ref, sin_ref, o_ref, *, scale):
    cos = cos_ref[...]
    sin = sin_ref[...]
    g = g_ref[...]
    x = x_ref[...]
    for grp in range(w_ref.shape[1] // MXU_COLS):
        w = w_ref[:, grp * MXU_COLS:(grp + 1) * MXU_COLS].astype(BF16)
        acc = jnp.dot(x, w, preferred_element_type=F32)
        for hh in range(MXU_COLS // HEAD_DIM):
            y = _rms(acc[:, hh * HEAD_DIM:(hh + 1) * HEAD_DIM], g)
            y = y * cos + _swap_halves(y) * sin
            if scale != 1.0:
                y = y * scale
            col0 = grp * MXU_COLS + hh * HEAD_DIM
            o_ref[:, col0:col0 + HEAD_DIM] = y.astype(o_ref.dtype)


def _matmul_rope(x, w, g, cos, sin, *, rows, n_out, tm, tn, scale, name):
    k = x.shape[1]
    tab_spec = pl.BlockSpec((tm, HEAD_DIM), lambda i, j: (i, 0))
    return pl.pallas_call(
        functools.partial(_mm_rope_kernel, scale=scale),
        grid=(rows // tm, n_out // tn),
        in_specs=[
            pl.BlockSpec((tm, k), lambda i, j: (i, 0)),
            pl.BlockSpec((k, tn), lambda i, j: (0, j)),
            pl.BlockSpec((1, HEAD_DIM), lambda i, j: (0, 0)),
            tab_spec,
            tab_spec,
        ],
        out_specs=pl.BlockSpec((tm, tn), lambda i, j: (i, j)),
        out_shape=jax.ShapeDtypeStruct((rows, n_out), BF16),
        compiler_params=_params("arbitrary", "arbitrary"),
        name=name,
    )(x, w, g.reshape(1, HEAD_DIM), cos, sin)


def _mm_rope_t_kernel(x_ref, w_ref, cos_ref, sin_ref, o_ref):
    cos = cos_ref[...]
    sin = sin_ref[...]
    x = x_ref[...]
    q4 = HEAD_DIM // 4
    for grp in range(w_ref.shape[1] // MXU_COLS):
        w = w_ref[:, grp * MXU_COLS:(grp + 1) * MXU_COLS].astype(BF16)
        acc_t = jnp.dot(x, w, preferred_element_type=F32).T
        for hh in range(MXU_COLS // HEAD_DIM):
            y = acc_t[hh * HEAD_DIM:(hh + 1) * HEAD_DIM, :]
            r = lax.rsqrt(jnp.mean(y * y, axis=0, keepdims=True) + NORM_EPS)
            partner = jnp.concatenate([y[q4:2 * q4], y[:q4], y[3 * q4:], y[2 * q4:3 * q4]], axis=0)
            row0 = grp * MXU_COLS + hh * HEAD_DIM
            o_ref[row0:row0 + HEAD_DIM, :] = ((y * cos + partner * sin) * r).astype(o_ref.dtype)


def _matmul_rope_t(x, w, cos_t, sin_t, *, rows, n_out, tn, name):
    k = x.shape[1]
    n_tiles, _, tm = cos_t.shape
    assert n_tiles * tm == rows
    tab_spec = pl.BlockSpec((None, HEAD_DIM, tm), lambda i, j: (i, 0, 0))
    return pl.pallas_call(
        _mm_rope_t_kernel,
        grid=(n_tiles, n_out // tn),
        in_specs=[
            pl.BlockSpec((tm, k), lambda i, j: (i, 0)),
            pl.BlockSpec((k, tn), lambda i, j: (0, j)),
            tab_spec,
            tab_spec,
        ],
        out_specs=pl.BlockSpec((None, tn, tm), lambda i, j: (i, j, 0)),
        out_shape=jax.ShapeDtypeStruct((n_tiles, n_out, tm), BF16),
        compiler_params=_params("arbitrary", "arbitrary"),
        name=name,
    )(x, w, cos_t, sin_t)


def _gateup_kernel(x_ref, w1_ref, w3_ref, w2_ref, o_ref, w2b_ref):
    x = x_ref[...]
    a = jnp.dot(x, w1_ref[...].astype(BF16), preferred_element_type=F32)
    b = jnp.dot(x, w3_ref[...].astype(BF16), preferred_element_type=F32)
    o_ref[...] = (a * jax.nn.sigmoid(a) * b).astype(o_ref.dtype)
    w2b_ref[...] = w2_ref[...].astype(w2b_ref.dtype)


def _gateup(x, w13, w2, layer, *, rows, tm, tn):
    k = x.shape[1]
    f = w13.shape[2] // 2
    n_j = f // tn
    steps = (rows // tm) * n_j
    slab = w2.shape[1] // steps
    assert slab * steps == w2.shape[1] and slab % 16 == 0
    return pl.pallas_call(
        _gateup_kernel,
        grid=(rows // tm, n_j),
        in_specs=[
            pl.BlockSpec((tm, k), lambda i, j: (i, 0), pipeline_mode=pl.Buffered(1)),
            pl.BlockSpec((None, k, tn), lambda i, j: (layer, 0, j)),
            pl.BlockSpec((None, k, tn), lambda i, j: (layer, 0, j + n_j)),
            pl.BlockSpec((None, slab, w2.shape[2]), lambda i, j: (layer, i * n_j + j, 0)),
        ],
        out_specs=[
            pl.BlockSpec((tm, tn), lambda i, j: (i, j)),
            pl.BlockSpec((slab, w2.shape[2]), lambda i, j: (i * n_j + j, 0)),
        ],
        out_shape=[
            jax.ShapeDtypeStruct((rows, f), BF16),
            jax.ShapeDtypeStruct(w2.shape[1:], BF16),
        ],
        compiler_params=_params("arbitrary", "arbitrary"),
        name="ffn_gateup",
    )(x, w13, w13, w2)


_NT = (((1,), (1,)), ((), ()))


def _na_block_rows(rb):
    return rb + NA_WIN_H


N_DR = 2 * NA_WIN_H - 1
N_DC = 2 * NA_WIN_W - 1
_TAB_BOTH, _TAB_FIRST, _TAB_SECOND, _TAB_NONE = 0, N_DR + 1, 2 * N_DR + 1, 3 * N_DR + 1
_TAB_SIZE = 3 * N_DR + 2


def _na_build_bias_tiles(rpb_ref, tab_scr, h0, hb):
    shape = (GRID_W, 2 * GRID_W)
    c = lax.broadcasted_iota(jnp.int32, shape, 0)
    lane = lax.broadcasted_iota(jnp.int32, shape, 1)
    kc = lane & (GRID_W - 1)
    second = lane >= GRID_W
    cs = jnp.clip(c - NA_WIN_W // 2, 0, GRID_W - NA_WIN_W)
    in_win = (kc >= cs) & (kc < cs + NA_WIN_W)
    dci = kc - c + (NA_WIN_W - 1)
    is_dc = [dci == k for k in range(N_DC)]
    neg = jnp.full(shape, NEG_INF, F32)
    for hh in range(hb):
        base = (h0 + hh) * (N_DR * N_DC)
        rows = []
        for d in range(N_DR):
            t = neg
            for k in range(N_DC):
                t = jnp.where(is_dc[k], rpb_ref[base + d * N_DC + k] * LOG2E, t)
            rows.append(jnp.where(in_win, t, NEG_INF))
        for d in range(N_DR + 1):
            lo = rows[d - 1] if d >= 1 else neg
            hi = rows[d] if d < N_DR else neg
            tab_scr[hh, _TAB_BOTH + d] = jnp.where(second, hi, lo)
        for d in range(N_DR):
            tab_scr[hh, _TAB_FIRST + d] = jnp.where(second, neg, rows[d])
            tab_scr[hh, _TAB_SECOND + d] = jnp.where(second, rows[d], neg)
        tab_scr[hh, _TAB_NONE] = neg


def _na_kernel(rpb_ref, q_ref, k_ref, v_ref, kc_ref, vc_ref, o_ref, tab_scr, *, rb, hb, nsb, rows):
    kr = _na_block_rows(rb)
    qn = rb * GRID_W
    half = NA_WIN_H // 2

    @pl.when((pl.program_id(1) == 0) & (pl.program_id(2) == 0))
    def _():
        _na_build_bias_tiles(rpb_ref, tab_scr, pl.program_id(0) * hb, hb)

    starts, tiles = [], []
    for sb in range(nsb):
        r0 = (pl.program_id(2) * nsb + sb) * rb
        kstart = jnp.clip(r0 - half, 0, rows - kr)
        starts.append(pl.multiple_of(kstart * GRID_W, GRID_W))
        tile_idx = []
        for i in range(rb):
            r = r0 + i
            rs = jnp.clip(r - half, 0, rows - NA_WIN_H)
            row_idx = []
            for jp in range(kr // 2):
                k0 = kstart + 2 * jp
                v0 = (k0 >= rs) & (k0 < rs + NA_WIN_H)
                v1 = (k0 + 1 >= rs) & (k0 + 1 < rs + NA_WIN_H)
                d0 = k0 - r + NA_WIN_H - 1
                idx = jnp.where(v0 & v1, _TAB_BOTH + d0 + 1,
                                jnp.where(v0, _TAB_FIRST + d0, jnp.where(v1, _TAB_SECOND + d0 + 1, _TAB_NONE)))
                row_idx.append(jnp.clip(idx, 0, _TAB_SIZE - 1))
            tile_idx.append(row_idx)
        tiles.append(tile_idx)

    def scores(sb, hh):
        cols = slice(hh * HEAD_DIM, (hh + 1) * HEAD_DIM)
        q = q_ref[sb * qn:(sb + 1) * qn, cols]
        ku = k_ref[pl.ds(starts[sb], kr * GRID_W), cols]
        bias = jnp.concatenate(
            [jnp.concatenate([tab_scr[hh, idx] for idx in row_idx], axis=1) for row_idx in tiles[sb]], axis=0)
        s_loc = lax.dot_general(q, ku, _NT, preferred_element_type=F32) + bias
        s_ctx = lax.dot_general(q, kc_ref[:, cols], _NT, preferred_element_type=F32)
        return s_loc, s_ctx

    def finish(sb, hh, s_loc, s_ctx):
        cols = slice(hh * HEAD_DIM, (hh + 1) * HEAD_DIM)
        vu = v_ref[pl.ds(starts[sb], kr * GRID_W), cols]
        m = jnp.maximum(jnp.max(s_loc, axis=-1, keepdims=True), jnp.max(s_ctx, axis=-1, keepdims=True))
        p_loc = jnp.exp2(s_loc - m)
        p_ctx = jnp.exp2(s_ctx - m)
        l = jnp.sum(p_loc, axis=-1, keepdims=True) + jnp.sum(p_ctx, axis=-1, keepdims=True)
        o = (jnp.dot(p_loc.astype(BF16), vu, preferred_element_type=F32)
             + jnp.dot(p_ctx.astype(BF16), vc_ref[:, cols], preferred_element_type=F32))
        o_ref[sb * qn:(sb + 1) * qn, cols] = (o / l).astype(o_ref.dtype)

    pairs = [(sb, hh) for sb in range(nsb) for hh in range(hb)]
    ahead = 1
    pending = [scores(*pair) for pair in pairs[:ahead]]
    for n, pair in enumerate(pairs):
        if n + ahead < len(pairs):
            pending.append(scores(*pairs[n + ahead]))
        finish(*pair, *pending.pop(0))


def _na_attention(qkv, rpb, *, n_batch, seq, ctx_len, n_heads, rb=4, hb=2, nsb=4):
    m_all = qkv.shape[0]
    rows = seq // GRID_W
    assert rpb.shape == (n_heads, N_DR, N_DC) and _na_block_rows(rb) % 2 == 0 and rows % (rb * nsb) == 0
    hw = hb * HEAD_DIM
    hblocks = n_heads // hb
    qrows = nsb * rb * GRID_W
    lat_spec = lambda part: pl.BlockSpec((seq, hw), lambda h, b, r, _: (b, part * hblocks + h))
    ctx_spec = lambda part: pl.BlockSpec((ctx_len, hw),
                                         lambda h, b, r, _: (n_batch * seq // ctx_len + b, part * hblocks + h))
    q_spec = pl.BlockSpec((qrows, hw), lambda h, b, r, _: (b * (seq // qrows) + r, h))
    return pl.pallas_call(
        functools.partial(_na_kernel, rb=rb, hb=hb, nsb=nsb, rows=rows),
        grid_spec=pltpu.PrefetchScalarGridSpec(
            num_scalar_prefetch=1,
            grid=(hblocks, n_batch, rows // (rb * nsb)),
            in_specs=[q_spec, lat_spec(1), lat_spec(2), ctx_spec(1), ctx_spec(2)],
            out_specs=q_spec,
            scratch_shapes=[pltpu.VMEM((hb, _TAB_SIZE, GRID_W, 2 * GRID_W), F32)],
        ),
        out_shape=jax.ShapeDtypeStruct((m_all, n_heads * HEAD_DIM), BF16),
        compiler_params=_params("arbitrary", "arbitrary", "arbitrary"),
        name="na_attention",
    )(rpb.reshape(-1), qkv, qkv, qkv, qkv, qkv)


def _ctx_attn_kernel(q_ref, k_ref, v_ref, o_in_ref, o_ref, *, hb):
    del o_in_ref
    for hh in range(hb):
        cols = slice(hh * HEAD_DIM, (hh + 1) * HEAD_DIM)
        s = lax.dot_general(q_ref[:, cols], k_ref[:, cols], _NT, preferred_element_type=F32)
        p = jnp.exp2(s - jnp.max(s, axis=-1, keepdims=True))
        l = jnp.sum(p, axis=-1, keepdims=True)
        o = jnp.dot(p.astype(BF16), v_ref[:, cols], preferred_element_type=F32)
        o_ref[:, cols] = (o / l).astype(o_ref.dtype)


def _ctx_attention(qkv, o, *, n_batch, seq, ctx_len, n_heads, hb=2):
    hw = hb * HEAD_DIM
    hblocks = n_heads // hb
    row0 = n_batch * seq // ctx_len
    spec = lambda part: pl.BlockSpec((ctx_len, hw), lambda b, h: (row0 + b, part * hblocks + h))
    return pl.pallas_call(
        functools.partial(_ctx_attn_kernel, hb=hb),
        grid=(n_batch, hblocks),
        in_specs=[spec(0), spec(1), spec(2), pl.BlockSpec(memory_space=pl.ANY)],
        out_specs=spec(0),
        out_shape=jax.ShapeDtypeStruct(o.shape, o.dtype),
        input_output_aliases={3: 0},
        compiler_params=_params("arbitrary", "arbitrary"),
        name="ctx_attention",
    )(qkv, qkv, qkv, o)


def _gqa_kernel(q_ref, k_ref, vt_ref, kc_ref, vtc_ref, o_ref, m_scr, l_scr, acc_scr, s0_scr, s1_scr):
    n_chunks, _, tk = vt_ref.shape
    m_scr[...] = jnp.full(m_scr.shape, NEG_INF, F32)
    l_scr[...] = jnp.zeros(l_scr.shape, F32)
    acc_scr[...] = jnp.zeros(acc_scr.shape, F32)

    def scores(k, g):
        return jnp.dot(k, q_ref[g * HEAD_DIM:(g + 1) * HEAD_DIM, :], preferred_element_type=F32)

    def accumulate(s, vt, g):
        m_old = m_scr[g]
        m_new = jnp.maximum(m_old, jnp.max(s, axis=0, keepdims=True))
        alpha = jnp.exp2(m_old - m_new)
        p = jnp.exp2(s - m_new)
        l_scr[g] = alpha * l_scr[g] + jnp.sum(p, axis=0, keepdims=True)
        acc_scr[g] = alpha * acc_scr[g] + jnp.dot(vt, p.astype(BF16), preferred_element_type=F32)
        m_scr[g] = m_new

    def k_chunk(c):
        return k_ref[pl.ds(pl.multiple_of(c * tk, tk), tk), :]

    def stage(cur_scr, nxt_scr, c):
        k_next = k_chunk(c + 1)
        vt = vt_ref[c]
        for g in range(GQA_GROUP):
            nxt_scr[g] = scores(k_next, g)
            accumulate(cur_scr[g], vt, g)

    for g in range(GQA_GROUP):
        s0_scr[g] = scores(k_chunk(0), g)

    def body(j, carry):
        stage(s0_scr, s1_scr, 2 * j)
        stage(s1_scr, s0_scr, 2 * j + 1)
        return carry

    lax.fori_loop(0, n_chunks // 2 - 1, body, 0)
    stage(s0_scr, s1_scr, n_chunks - 2)
    vt_last = vt_ref[n_chunks - 1]
    for g in range(GQA_GROUP):
        s_ctx = scores(kc_ref[...], g)
        accumulate(s1_scr[g], vt_last, g)
        accumulate(s_ctx, vtc_ref[...], g)
    for g in range(GQA_GROUP):
        o_ref[:, g * HEAD_DIM:(g + 1) * HEAD_DIM] = (acc_scr[g] / l_scr[g]).T.astype(o_ref.dtype)


def _gqa_attention(qt, k, vt, *, n_batch, seq, ctx_len):
    n_kv = k.shape[1] // HEAD_DIM
    tq = qt.shape[2]
    tk = vt.shape[2]
    assert (seq // tk) % 2 == 0 and seq % tq == 0
    gw = GQA_GROUP * HEAD_DIM
    n_qt = seq // tq
    q_spec = pl.BlockSpec((tq, gw), lambda b, h, i: (b * n_qt + i, h))
    return pl.pallas_call(
        _gqa_kernel,
        grid=(n_batch, n_kv, n_qt),
        in_specs=[
            pl.BlockSpec((None, gw, tq), lambda b, h, i: (b * n_qt + i, h, 0)),
            pl.BlockSpec((seq, HEAD_DIM), lambda b, h, i: (b, h)),
            pl.BlockSpec((seq // tk, HEAD_DIM, tk), lambda b, h, i: (b, h, 0)),
            pl.BlockSpec((ctx_len, HEAD_DIM), lambda b, h, i: (n_batch * seq // ctx_len + b, h)),
            pl.BlockSpec((None, HEAD_DIM, ctx_len), lambda b, h, i: (n_batch * seq // tk, h, b)),
        ],
        out_specs=q_spec,
        out_shape=jax.ShapeDtypeStruct((n_batch * seq, qt.shape[1]), BF16),
        scratch_shapes=[
            pltpu.VMEM((GQA_GROUP, 1, tq), F32),
            pltpu.VMEM((GQA_GROUP, 1, tq), F32),
            pltpu.VMEM((GQA_GROUP, HEAD_DIM, tq), F32),
            pltpu.VMEM((GQA_GROUP, tk, tq), F32),
            pltpu.VMEM((GQA_GROUP, tk, tq), F32),
        ],
        compiler_params=_params("arbitrary", "arbitrary", "arbitrary"),
        name="gqa_attention",
    )(qt, k, vt, k, vt)


def _rope_partner(v):
    q4 = HEAD_DIM // 4
    return jnp.concatenate([v[..., q4:2 * q4], v[..., :q4], v[..., 3 * q4:], v[..., 2 * q4:3 * q4]], axis=-1)


def _rope_tables_t(cos, sin, g, scale, rows, tm):
    cos_t = (cos[:rows] * (g * scale)[None, :]).reshape(rows // tm, tm, HEAD_DIM)
    sin_t = (sin[:rows] * (_rope_partner(g) * scale)[None, :]).reshape(rows // tm, tm, HEAD_DIM)
    return jnp.transpose(cos_t, (0, 2, 1)), jnp.transpose(sin_t, (0, 2, 1))


def _rope_tables(n_batch, seq, ctx_len):
    quarter = HEAD_DIM // 4
    t = jnp.arange(seq)
    freqs = ROPE_THETA ** (-jnp.arange(quarter, dtype=F32) / quarter)
    ang_r = (t // GRID_W).astype(F32)[:, None] * freqs[None, :]
    ang_c = (t % GRID_W).astype(F32)[:, None] * freqs[None, :]
    cos = jnp.concatenate([jnp.cos(ang_r)] * 2 + [jnp.cos(ang_c)] * 2, axis=-1)
    sin = jnp.concatenate([-jnp.sin(ang_r), jnp.sin(ang_r), -jnp.sin(ang_c), jnp.sin(ang_c)], axis=-1)
    n_ctx = n_batch * ctx_len
    cos = jnp.concatenate([jnp.tile(cos, (n_batch, 1)), jnp.ones((n_ctx, HEAD_DIM), F32)], axis=0)
    sin = jnp.concatenate([jnp.tile(sin, (n_batch, 1)), jnp.zeros((n_ctx, HEAD_DIM), F32)], axis=0)
    return cos, sin


def kernel(x, c, ctx, c_ctx, ada_w, ada_b, norm_g, na_wqkv, na_wo, na_rpb, gqa_wq, gqa_wkv, gqa_q_norm,
           gqa_k_norm, gqa_wo, ffn_w13, ffn_w2):
    n_batch, seq, d = x.shape
    ctx_len = ctx.shape[1]
    depth = ada_w.shape[0]
    assert depth == 2 and na_wqkv.shape[0] == 1 and gqa_wq.shape[0] == 1
    assert seq % GRID_W == 0 and n_batch + 1 <= MOD_ROWS
    n_heads = d // HEAD_DIM
    m_lat = n_batch * seq
    m_all = m_lat + n_batch * ctx_len
    scale = HEAD_DIM ** -0.5 * LOG2E
    tm_all = m_all // 8
    tm_lat = m_lat // 8
    tm_down = 2
    seg_tiles = lambda t: seq // t
    assert tm_all % 16 == 0 and tm_lat % 16 == 0

    cvec = jnp.zeros((MOD_ROWS, d), F32).at[:n_batch].set(c).at[n_batch].set(c_ctx)
    mod = _ada_mod(cvec, ada_w, ada_b).reshape(depth, MOD_ROWS, N_MOD, d)
    cos, sin = _rope_tables(n_batch, seq, ctx_len)
    cos_qt, sin_qt = _rope_tables_t(cos, sin, gqa_q_norm[0], scale, m_lat, tm_lat)

    tn_w = 512
    wqkv, wq, wkv = na_wqkv[0], gqa_wq[0], gqa_wkv[0]

    x_lat = x.reshape(m_lat, d)
    x_ctx = ctx.reshape(n_batch * ctx_len, d)
    seg_kw = dict(seg_tiles=seg_tiles, n_batch=n_batch)

    h = _prenorm(x_lat, x_ctx, norm_g[0], mod[0], g_row=0, sh_row=0, sc_row=1, **seg_kw)
    qkv = _matmul(h, wqkv, rows=m_all, n_out=wqkv.shape[1], out_dtype=BF16, tm=tm_all, tn=tn_w,
                  scale_blocks=d // tn_w, scale=scale, name="na_qkv")
    o = _na_attention(qkv, na_rpb[0], n_batch=n_batch, seq=seq, ctx_len=ctx_len, n_heads=n_heads)
    o = _ctx_attention(qkv, o, n_batch=n_batch, seq=seq, ctx_len=ctx_len, n_heads=n_heads)
    y = _matmul(o, na_wo[0], rows=m_all, n_out=d, out_dtype=F32, tm=tm_all, tn=tn_w, name="na_wo")
    xa, h = _resid(y, x_lat, norm_g[0], mod[0], x_ctx=x_ctx, rows=m_all, gt_row=2, gpost_row=1,
                   nxt=(2, 3, 4), g2=norm_g[0], mod2=mod[0], **seg_kw)
    gu, w2 = _gateup(h, ffn_w13, ffn_w2, 0, rows=m_all, tm=2 * tm_all, tn=256)
    y = _matmul(gu, w2, rows=m_all, n_out=d, out_dtype=F32, tm=tm_all // tm_down, tn=512, name="ffn_down")
    xa, h = _resid(y, xa, norm_g[0], mod[0], rows=m_all, gt_row=5, gpost_row=3,
                   nxt=(0, 0, 1), g2=norm_g[1], mod2=mod[1], **seg_kw)

    kv_w = wkv.shape[1] // 2
    qt = _matmul_rope_t(h, wq, cos_qt, sin_qt, rows=m_lat, n_out=d, tn=tn_w, name="gqa_q")
    k = _matmul_rope(h, wkv, gqa_k_norm[0], cos, sin, rows=m_all, n_out=kv_w, tm=tm_all, tn=tn_w, scale=1.0, name="gqa_k")
    gqa_tk = 512
    assert seq % gqa_tk == 0 and n_batch * ctx_len == gqa_tk
    vt = _matmul_t(h, wkv, rows=m_all, n_out=kv_w, tm=gqa_tk, tn=tn_w, col_blk_off=kv_w // tn_w, name="gqa_v")
    o = _gqa_attention(qt, k, vt, n_batch=n_batch, seq=seq, ctx_len=ctx_len)
    y = _matmul(o, gqa_wo[0], rows=m_lat, n_out=d, out_dtype=F32, tm=tm_lat, tn=tn_w, name="gqa_wo")
    xl, h = _resid(y, xa, norm_g[1], mod[1], rows=m_lat, gt_row=2, gpost_row=1,
                   nxt=(2, 3, 4), g2=norm_g[1], mod2=mod[1], **seg_kw)
    gu, w2 = _gateup(h, ffn_w13, ffn_w2, 1, rows=m_lat, tm=2 * tm_lat, tn=256)
    y = _matmul(gu, w2, rows=m_lat, n_out=d, out_dtype=F32, tm=tm_lat // tm_down, tn=512, name="ffn_down")
    xl = _resid(y, xl, norm_g[1], mod[1], rows=m_lat, gt_row=5, gpost_row=3, **seg_kw)
    return xl.reshape(n_batch, seq, d)
```

```python
import functools

import jax
import jax.numpy as jnp
from jax import lax
from jax.experimental import pallas as pl
from jax.experimental.pallas import tpu as pltpu

GRID_W = 64
NA_WIN_H = 8
NA_WIN_W = 16
HEAD_DIM = 128
GQA_GROUP = 4
ROPE_THETA = 10000.0
NORM_EPS = 1e-6
NEG_INF = -1e30
LOG2E = 1.4426950408889634
N_MOD = 6

VMEM_LIMIT_BYTES = 56 * 1024 * 1024
MXU_COLS = 256
MOD_ROWS = 8

F32 = jnp.float32
BF16 = jnp.bfloat16


def _params(*sem):
    return pltpu.CompilerParams(dimension_semantics=sem, vmem_limit_bytes=VMEM_LIMIT_BYTES)


def _rms(x, g):
    ms = jnp.mean(x * x, axis=-1, keepdims=True)
    return x * lax.rsqrt(ms + NORM_EPS) * g


def _seg_index(rows_per_seg_tiles, n_batch):
    return lambda i: jnp.minimum(i // rows_per_seg_tiles, n_batch)


def _ada_kernel(c_ref, w_ref, b_ref, o_ref):
    c = c_ref[...]
    s = (c * jax.nn.sigmoid(c)).astype(BF16)
    o_ref[...] = jnp.dot(s, w_ref[...].astype(BF16), preferred_element_type=F32) + b_ref[...]


def _ada_mod(cvec, ada_w, ada_b, tn=1024):
    depth, d, n = ada_w.shape
    return pl.pallas_call(
        _ada_kernel,
        grid=(depth, n // tn),
        in_specs=[
            pl.BlockSpec((MOD_ROWS, d), lambda l, j: (0, 0)),
            pl.BlockSpec((None, d, tn), lambda l, j: (l, 0, j)),
            pl.BlockSpec((None, 1, tn), lambda l, j: (l, 0, j)),
        ],
        out_specs=pl.BlockSpec((None, MOD_ROWS, tn), lambda l, j: (l, 0, j)),
        out_shape=jax.ShapeDtypeStruct((depth, MOD_ROWS, n), F32),
        compiler_params=_params("arbitrary", "arbitrary"),
        name="ada_mod",
    )(cvec, ada_w, ada_b.reshape(depth, 1, n))


def _token_rows(x_lat, x_ctx, rows, tm):
    d = x_lat.shape[1]
    n_lat = min(rows, x_lat.shape[0]) // tm
    if x_ctx is None:
        x_ctx = x_lat
        assert rows <= x_lat.shape[0]
    else:
        assert x_lat.shape[0] % tm == 0 and rows == x_lat.shape[0] + x_ctx.shape[0]
    specs = [pl.BlockSpec((tm, d), lambda i: (jnp.minimum(i, n_lat - 1), 0)),
             pl.BlockSpec((tm, d), lambda i: (jnp.maximum(i - n_lat, 0), 0))]
    return [x_lat, x_ctx], specs, n_lat


def _read_token_rows(xl_ref, xc_ref, n_lat):
    return jnp.where(pl.program_id(0) < n_lat, xl_ref[...], xc_ref[...])


def _prenorm_kernel(xl_ref, xc_ref, g_ref, mod_ref, h_ref, *, n_lat, g_row, sh_row, sc_row):
    y = _rms(_read_token_rows(xl_ref, xc_ref, n_lat), g_ref[g_row:g_row + 1, :])
    h = y * (1.0 + mod_ref[sc_row:sc_row + 1, :]) + mod_ref[sh_row:sh_row + 1, :]
    h_ref[...] = h.astype(h_ref.dtype)


def _prenorm(x_lat, x_ctx, g, mod, *, seg_tiles, n_batch, g_row, sh_row, sc_row, tm=256):
    d = x_lat.shape[1]
    m = x_lat.shape[0] + x_ctx.shape[0]
    seg = _seg_index(seg_tiles(tm), n_batch)
    x_args, x_specs, n_lat = _token_rows(x_lat, x_ctx, m, tm)
    return pl.pallas_call(
        functools.partial(_prenorm_kernel, n_lat=n_lat, g_row=g_row, sh_row=sh_row, sc_row=sc_row),
        grid=(m // tm,),
        in_specs=x_specs + [
            pl.BlockSpec(g.shape, lambda i: (0, 0)),
            pl.BlockSpec((None, N_MOD, d), lambda i: (seg(i), 0, 0)),
        ],
        out_specs=pl.BlockSpec((tm, d), lambda i: (i, 0)),
        out_shape=jax.ShapeDtypeStruct((m, d), BF16),
        compiler_params=_params("arbitrary"),
        name="prenorm",
    )(*x_args, g, mod)


def _resid_kernel(y_ref, xl_ref, xc_ref, g_ref, mod_ref, *rest, n_lat, gt_row, gpost_row, nxt):
    x = _read_token_rows(xl_ref, xc_ref, n_lat)
    xn = x + mod_ref[gt_row:gt_row + 1, :] * _rms(y_ref[...], g_ref[gpost_row:gpost_row + 1, :])
    if nxt is None:
        (xo_ref,) = rest
        xo_ref[...] = xn
        return
    g2_ref, mod2_ref, xo_ref, h_ref = rest
    gpre_row, sh_row, sc_row = nxt
    xo_ref[...] = xn
    h = _rms(xn, g2_ref[gpre_row:gpre_row + 1, :])
    h = h * (1.0 + mod2_ref[sc_row:sc_row + 1, :]) + mod2_ref[sh_row:sh_row + 1, :]
    h_ref[...] = h.astype(h_ref.dtype)


def _resid(y, x, g, mod, *, rows, seg_tiles, n_batch, gt_row, gpost_row, x_ctx=None, nxt=None, g2=None, mod2=None,
           tm=256):
    d = x.shape[1]
    seg = _seg_index(seg_tiles(tm), n_batch)
    row_spec = pl.BlockSpec((tm, d), lambda i: (i, 0))
    mod_spec = pl.BlockSpec((None, N_MOD, d), lambda i: (seg(i), 0, 0))
    x_args, x_specs, n_lat = _token_rows(x, x_ctx, rows, tm)
    in_specs = [row_spec] + x_specs + [pl.BlockSpec(g.shape, lambda i: (0, 0)), mod_spec]
    args = [y] + x_args + [g, mod]
    out_specs = [row_spec]
    out_shape = [jax.ShapeDtypeStruct((rows, d), F32)]
    if nxt is not None:
        in_specs += [pl.BlockSpec(g2.shape, lambda i: (0, 0)), mod_spec]
        args += [g2, mod2]
        out_specs.append(row_spec)
        out_shape.append(jax.ShapeDtypeStruct((rows, d), BF16))
    out = pl.pallas_call(
        functools.partial(_resid_kernel, n_lat=n_lat, gt_row=gt_row, gpost_row=gpost_row, nxt=nxt),
        grid=(rows // tm,),
        in_specs=in_specs,
        out_specs=out_specs,
        out_shape=out_shape,
        compiler_params=_params("arbitrary"),
        name="resid_norm",
    )(*args)
    return out if nxt is not None else out[0]


SIDE_CAST_ROWS = 32


def _mm_kernel(x_ref, w_ref, *rest, scale_blocks, scale, n_side):
    side_in, o_ref, side_out = rest[:n_side], rest[n_side], rest[n_side + 1:]
    acc = jnp.dot(x_ref[...], w_ref[...].astype(BF16), preferred_element_type=F32)
    if scale_blocks:
        acc = acc * jnp.where(pl.program_id(1) < scale_blocks, scale, 1.0)
    o_ref[...] = acc.astype(o_ref.dtype)
    for src, dst in zip(side_in, side_out):
        dst[...] = src[...].astype(dst.dtype)


def _matmul(x, w, *, rows, n_out, out_dtype, tm, tn, col_blk_off=0, scale_blocks=0, scale=1.0, side_casts=(),
            name="matmul"):
    k = x.shape[1]
    n_j = n_out // tn
    n_steps = (rows // tm) * n_j
    in_specs = [
        pl.BlockSpec((tm, k), lambda i, j: (i, 0)),
        pl.BlockSpec((k, tn), lambda i, j: (0, j + col_blk_off)),
    ]
    out_specs = [pl.BlockSpec((tm, tn), lambda i, j: (i, j))]
    out_shape = [jax.ShapeDtypeStruct((rows, n_out), out_dtype)]
    for ws in side_casts:
        n_slabs = ws.shape[0] // SIDE_CAST_ROWS
        assert n_slabs * SIDE_CAST_ROWS == ws.shape[0] and n_slabs <= n_steps
        spec = pl.BlockSpec((SIDE_CAST_ROWS, ws.shape[1]),
                            lambda i, j, n_slabs=n_slabs: (jnp.minimum(i * n_j + j, n_slabs - 1), 0))
        in_specs.append(spec)
        out_specs.append(spec)
        out_shape.append(jax.ShapeDtypeStruct(ws.shape, BF16))
    out = pl.pallas_call(
        functools.partial(_mm_kernel, scale_blocks=scale_blocks, scale=scale, n_side=len(side_casts)),
        grid=(rows // tm, n_j),
        in_specs=in_specs,
        out_specs=out_specs,
        out_shape=out_shape,
        compiler_params=_params("arbitrary", "arbitrary"),
        name=name,
    )(x, w, *side_casts)
    return (out[0], out[1:]) if side_casts else out[0]


def _mm_t_kernel(x_ref, w_ref, o_ref):
    acc = jnp.dot(x_ref[...], w_ref[...].astype(BF16), preferred_element_type=F32)
    o_ref[...] = acc.T.astype(o_ref.dtype)


def _matmul_t(x, w, *, rows, n_out, tm, tn, col_blk_off=0, name="matmul_t"):
    k = x.shape[1]
    return pl.pallas_call(
        _mm_t_kernel,
        grid=(n_out // tn, rows // tm),
        in_specs=[
            pl.BlockSpec((tm, k), lambda j, i: (i, 0)),
            pl.BlockSpec((k, tn), lambda j, i: (0, j + col_blk_off)),
        ],
        out_specs=pl.BlockSpec((None, tn, tm), lambda j, i: (i, j, 0)),
        out_shape=jax.ShapeDtypeStruct((rows // tm, n_out, tm), BF16),
        compiler_params=_params("arbitrary", "arbitrary"),
        name=name,
    )(x, w)


def _swap_halves(y):
    lane = lax.broadcasted_iota(jnp.int32, y.shape, 1)
    return jnp.where((lane & 32) == 0, pltpu.roll(y, 96, 1), pltpu.roll(y, 32, 1))


def _mm_rope_kernel(x_ref, w_ref, g_ref, cos_ref, sin_ref, o_ref, *, scale):
    cos = cos_ref[...]
    sin = sin_ref[...]
    g = g_ref[...]
    x = x_ref[...]
    for grp in range(w_ref.shape[1] // MXU_COLS):
        w = w_ref[:, grp * MXU_COLS:(grp + 1) * MXU_COLS].astype(BF16)
        acc = jnp.dot(x, w, preferred_element_type=F32)
        for hh in range(MXU_COLS // HEAD_DIM):
            y = _rms(acc[:, hh * HEAD_DIM:(hh + 1) * HEAD_DIM], g)
            y = y * cos + _swap_halves(y) * sin
            if scale != 1.0:
                y = y * scale
            col0 = grp * MXU_COLS + hh * HEAD_DIM
            o_ref[:, col0:col0 + HEAD_DIM] = y.astype(o_ref.dtype)


def _matmul_rope(x, w, g, cos, sin, *, rows, n_out, tm, tn, scale, name):
    k = x.shape[1]
    tab_spec = pl.BlockSpec((tm, HEAD_DIM), lambda i, j: (i, 0))
    return pl.pallas_call(
        functools.partial(_mm_rope_kernel, scale=scale),
        grid=(rows // tm, n_out // tn),
        in_specs=[
            pl.BlockSpec((tm, k), lambda i, j: (i, 0)),
            pl.BlockSpec((k, tn), lambda i, j: (0, j)),
            pl.BlockSpec((1, HEAD_DIM), lambda i, j: (0, 0)),
            tab_spec,
            tab_spec,
        ],
        out_specs=pl.BlockSpec((tm, tn), lambda i, j: (i, j)),
        out_shape=jax.ShapeDtypeStruct((rows, n_out), BF16),
        compiler_params=_params("arbitrary", "arbitrary"),
        name=name,
    )(x, w, g.reshape(1, HEAD_DIM), cos, sin)


def _mm_rope_t_kernel(x_ref, w_ref, cos_ref, sin_ref, o_ref):
    cos = cos_ref[...]
    sin = sin_ref[...]
    x = x_ref[...]
    q4 = HEAD_DIM // 4
    for grp in range(w_ref.shape[1] // MXU_COLS):
        w = w_ref[:, grp * MXU_COLS:(grp + 1) * MXU_COLS].astype(BF16)
        acc_t = jnp.dot(x, w, preferred_element_type=F32).T
        for hh in range(MXU_COLS // HEAD_DIM):
            y = acc_t[hh * HEAD_DIM:(hh + 1) * HEAD_DIM, :]
            r = lax.rsqrt(jnp.mean(y * y, axis=0, keepdims=True) + NORM_EPS)
            partner = jnp.concatenate([y[q4:2 * q4], y[:q4], y[3 * q4:], y[2 * q4:3 * q4]], axis=0)
            row0 = grp * MXU_COLS + hh * HEAD_DIM
            o_ref[row0:row0 + HEAD_DIM, :] = ((y * cos + partner * sin) * r).astype(o_ref.dtype)


def _matmul_rope_t(x, w, cos_t, sin_t, *, rows, n_out, tn, name):
    k = x.shape[1]
    n_tiles, _, tm = cos_t.shape
    assert n_tiles * tm == rows
    tab_spec = pl.BlockSpec((None, HEAD_DIM, tm), lambda i, j: (i, 0, 0))
    return pl.pallas_call(
        _mm_rope_t_kernel,
        grid=(n_tiles, n_out // tn),
        in_specs=[
            pl.BlockSpec((tm, k), lambda i, j: (i, 0)),
            pl.BlockSpec((k, tn), lambda i, j: (0, j)),
            tab_spec,
            tab_spec,
        ],
        out_specs=pl.BlockSpec((None, tn, tm), lambda i, j: (i, j, 0)),
        out_shape=jax.ShapeDtypeStruct((n_tiles, n_out, tm), BF16),
        compiler_params=_params("arbitrary", "arbitrary"),
        name=name,
    )(x, w, cos_t, sin_t)


def _gateup_kernel(x_ref, w1_ref, w3_ref, w2_ref, o_ref, w2b_ref):
    x = x_ref[...]
    a = jnp.dot(x, w1_ref[...].astype(BF16), preferred_element_type=F32)
    b = jnp.dot(x, w3_ref[...].astype(BF16), preferred_element_type=F32)
    o_ref[...] = (a * jax.nn.sigmoid(a) * b).astype(o_ref.dtype)
    w2b_ref[...] = w2_ref[...].astype(w2b_ref.dtype)


def _gateup(x, w13, w2, layer, *, rows, tm, tn):
    k = x.shape[1]
    f = w13.shape[2] // 2
    n_j = f // tn
    steps = (rows // tm) * n_j
    slab = w2.shape[1] // steps
    assert slab * steps == w2.shape[1] and slab % 16 == 0
    return pl.pallas_call(
        _gateup_kernel,
        grid=(rows // tm, n_j),
        in_specs=[
            pl.BlockSpec((tm, k), lambda i, j: (i, 0), pipeline_mode=pl.Buffered(1)),
            pl.BlockSpec((None, k, tn), lambda i, j: (layer, 0, j)),
            pl.BlockSpec((None, k, tn), lambda i, j: (layer, 0, j + n_j)),
            pl.BlockSpec((None, slab, w2.shape[2]), lambda i, j: (layer, i * n_j + j, 0)),
        ],
        out_specs=[
            pl.BlockSpec((tm, tn), lambda i, j: (i, j)),
            pl.BlockSpec((slab, w2.shape[2]), lambda i, j: (i * n_j + j, 0)),
        ],
        out_shape=[
            jax.ShapeDtypeStruct((rows, f), BF16),
            jax.ShapeDtypeStruct(w2.shape[1:], BF16),
        ],
        compiler_params=_params("arbitrary", "arbitrary"),
        name="ffn_gateup",
    )(x, w13, w13, w2)


_NT = (((1,), (1,)), ((), ()))


def _na_block_rows(rb):
    return rb + NA_WIN_H


N_DR = 2 * NA_WIN_H - 1
N_DC = 2 * NA_WIN_W - 1
_TAB_BOTH, _TAB_FIRST, _TAB_SECOND, _TAB_NONE = 0, N_DR + 1, 2 * N_DR + 1, 3 * N_DR + 1
_TAB_SIZE = 3 * N_DR + 2


def _na_build_bias_tiles(rpb_ref, tab_scr, h0, hb):
    shape = (GRID_W, 2 * GRID_W)
    c = lax.broadcasted_iota(jnp.int32, shape, 0)
    lane = lax.broadcasted_iota(jnp.int32, shape, 1)
    kc = lane & (GRID_W - 1)
    second = lane >= GRID_W
    cs = jnp.clip(c - NA_WIN_W // 2, 0, GRID_W - NA_WIN_W)
    in_win = (kc >= cs) & (kc < cs + NA_WIN_W)
    dci = kc - c + (NA_WIN_W - 1)
    is_dc = [dci == k for k in range(N_DC)]
    neg = jnp.full(shape, NEG_INF, F32)
    for hh in range(hb):
        base = (h0 + hh) * (N_DR * N_DC)
        rows = []
        for d in range(N_DR):
            t = neg
            for k in range(N_DC):
                t = jnp.where(is_dc[k], rpb_ref[base + d * N_DC + k] * LOG2E, t)
            rows.append(jnp.where(in_win, t, NEG_INF))
        for d in range(N_DR + 1):
            lo = rows[d - 1] if d >= 1 else neg
            hi = rows[d] if d < N_DR else neg
            tab_scr[hh, _TAB_BOTH + d] = jnp.where(second, hi, lo)
        for d in range(N_DR):
            tab_scr[hh, _TAB_FIRST + d] = jnp.where(second, neg, rows[d])
            tab_scr[hh, _TAB_SECOND + d] = jnp.where(second, rows[d], neg)
        tab_scr[hh, _TAB_NONE] = neg


def _na_kernel(rpb_ref, q_ref, k_ref, v_ref, kc_ref, vc_ref, o_ref, tab_scr, *, rb, hb, nsb, rows):
    kr = _na_block_rows(rb)
    qn = rb * GRID_W
    half = NA_WIN_H // 2

    @pl.when((pl.program_id(1) == 0) & (pl.program_id(2) == 0))
    def _():
        _na_build_bias_tiles(rpb_ref, tab_scr, pl.program_id(0) * hb, hb)

    starts, tiles = [], []
    for sb in range(nsb):
        r0 = (pl.program_id(2) * nsb + sb) * rb
        kstart = jnp.clip(r0 - half, 0, rows - kr)
        starts.append(pl.multiple_of(kstart * GRID_W, GRID_W))
        tile_idx = []
        for i in range(rb):
            r = r0 + i
            rs = jnp.clip(r - half, 0, rows - NA_WIN_H)
            row_idx = []
            for jp in range(kr // 2):
                k0 = kstart + 2 * jp
                v0 = (k0 >= rs) & (k0 < rs + NA_WIN_H)
                v1 = (k0 + 1 >= rs) & (k0 + 1 < rs + NA_WIN_H)
                d0 = k0 - r + NA_WIN_H - 1
                idx = jnp.where(v0 & v1, _TAB_BOTH + d0 + 1,
                                jnp.where(v0, _TAB_FIRST + d0, jnp.where(v1, _TAB_SECOND + d0 + 1, _TAB_NONE)))
                row_idx.append(jnp.clip(idx, 0, _TAB_SIZE - 1))
            tile_idx.append(row_idx)
        tiles.append(tile_idx)

    def scores(sb, hh):
        cols = slice(hh * HEAD_DIM, (hh + 1) * HEAD_DIM)
        q = q_ref[sb * qn:(sb + 1) * qn, cols]
        ku = k_ref[pl.ds(starts[sb], kr * GRID_W), cols]
        bias = jnp.concatenate(
            [jnp.concatenate([tab_scr[hh, idx] for idx in row_idx], axis=1) for row_idx in tiles[sb]], axis=0)
        s_loc = lax.dot_general(q, ku, _NT, preferred_element_type=F32) + bias
        s_ctx = lax.dot_general(q, kc_ref[:, cols], _NT, preferred_element_type=F32)
        return s_loc, s_ctx

    def finish(sb, hh, s_loc, s_ctx):
        cols = slice(hh * HEAD_DIM, (hh + 1) * HEAD_DIM)
        vu = v_ref[pl.ds(starts[sb], kr * GRID_W), cols]
        m = jnp.maximum(jnp.max(s_loc, axis=-1, keepdims=True), jnp.max(s_ctx, axis=-1, keepdims=True))
        p_loc = jnp.exp2(s_loc - m)
        p_ctx = jnp.exp2(s_ctx - m)
        l = jnp.sum(p_loc, axis=-1, keepdims=True) + jnp.sum(p_ctx, axis=-1, keepdims=True)
        o = (jnp.dot(p_loc.astype(BF16), vu, preferred_element_type=F32)
             + jnp.dot(p_ctx.astype(BF16), vc_ref[:, cols], preferred_element_type=F32))
        o_ref[sb * qn:(sb + 1) * qn, cols] = (o / l).astype(o_ref.dtype)

    pairs = [(sb, hh) for sb in range(nsb) for hh in range(hb)]
    ahead = 1
    pending = [scores(*pair) for pair in pairs[:ahead]]
    for n, pair in enumerate(pairs):
        if n + ahead < len(pairs):
            pending.append(scores(*pairs[n + ahead]))
        finish(*pair, *pending.pop(0))


def _na_attention(qkv, rpb, *, n_batch, seq, ctx_len, n_heads, rb=4, hb=2, nsb=4):
    m_all = qkv.shape[0]
    rows = seq // GRID_W
    assert rpb.shape == (n_heads, N_DR, N_DC) and _na_block_rows(rb) % 2 == 0 and rows % (rb * nsb) == 0
    hw = hb * HEAD_DIM
    hblocks = n_heads // hb
    qrows = nsb * rb * GRID_W
    lat_spec = lambda part: pl.BlockSpec((seq, hw), lambda h, b, r, _: (b, part * hblocks + h))
    ctx_spec = lambda part: pl.BlockSpec((ctx_len, hw),
                                         lambda h, b, r, _: (n_batch * seq // ctx_len + b, part * hblocks + h))
    q_spec = pl.BlockSpec((qrows, hw), lambda h, b, r, _: (b * (seq // qrows) + r, h))
    return pl.pallas_call(
        functools.partial(_na_kernel, rb=rb, hb=hb, nsb=nsb, rows=rows),
        grid_spec=pltpu.PrefetchScalarGridSpec(
            num_scalar_prefetch=1,
            grid=(hblocks, n_batch, rows // (rb * nsb)),
            in_specs=[q_spec, lat_spec(1), lat_spec(2), ctx_spec(1), ctx_spec(2)],
            out_specs=q_spec,
            scratch_shapes=[pltpu.VMEM((hb, _TAB_SIZE, GRID_W, 2 * GRID_W), F32)],
        ),
        out_shape=jax.ShapeDtypeStruct((m_all, n_heads * HEAD_DIM), BF16),
        compiler_params=_params("arbitrary", "arbitrary", "arbitrary"),
        name="na_attention",
    )(rpb.reshape(-1), qkv, qkv, qkv, qkv, qkv)


def _ctx_attn_kernel(q_ref, k_ref, v_ref, o_in_ref, o_ref, *, hb):
    del o_in_ref
    for hh in range(hb):
        cols = slice(hh * HEAD_DIM, (hh + 1) * HEAD_DIM)
        s = lax.dot_general(q_ref[:, cols], k_ref[:, cols], _NT, preferred_element_type=F32)
        p = jnp.exp2(s - jnp.max(s, axis=-1, keepdims=True))
        l = jnp.sum(p, axis=-1, keepdims=True)
        o = jnp.dot(p.astype(BF16), v_ref[:, cols], preferred_element_type=F32)
        o_ref[:, cols] = (o / l).astype(o_ref.dtype)


def _ctx_attention(qkv, o, *, n_batch, seq, ctx_len, n_heads, hb=2):
    hw = hb * HEAD_DIM
    hblocks = n_heads // hb
    row0 = n_batch * seq // ctx_len
    spec = lambda part: pl.BlockSpec((ctx_len, hw), lambda b, h: (row0 + b, part * hblocks + h))
    return pl.pallas_call(
        functools.partial(_ctx_attn_kernel, hb=hb),
        grid=(n_batch, hblocks),
        in_specs=[spec(0), spec(1), spec(2), pl.BlockSpec(memory_space=pl.ANY)],
        out_specs=spec(0),
        out_shape=jax.ShapeDtypeStruct(o.shape, o.dtype),
        input_output_aliases={3: 0},
        compiler_params=_params("arbitrary", "arbitrary"),
        name="ctx_attention",
    )(qkv, qkv, qkv, o)


def _gqa_kernel(q_ref, k_ref, vt_ref, kc_ref, vtc_ref, o_ref, m_scr, l_scr, acc_scr, s0_scr, s1_scr):
    n_chunks, _, tk = vt_ref.shape
    m_scr[...] = jnp.full(m_scr.shape, NEG_INF, F32)
    l_scr[...] = jnp.zeros(l_scr.shape, F32)
    acc_scr[...] = jnp.zeros(acc_scr.shape, F32)

    def scores(k, g):
        return jnp.dot(k, q_ref[g * HEAD_DIM:(g + 1) * HEAD_DIM, :], preferred_element_type=F32)

    def accumulate(s, vt, g):
        m_old = m_scr[g]
        m_new = jnp.maximum(m_old, jnp.max(s, axis=0, keepdims=True))
        alpha = jnp.exp2(m_old - m_new)
        p = jnp.exp2(s - m_new)
        l_scr[g] = alpha * l_scr[g] + jnp.sum(p, axis=0, keepdims=True)
        acc_scr[g] = alpha * acc_scr[g] + jnp.dot(vt, p.astype(BF16), preferred_element_type=F32)
        m_scr[g] = m_new

    def k_chunk(c):
        return k_ref[pl.ds(pl.multiple_of(c * tk, tk), tk), :]

    def stage(cur_scr, nxt_scr, c):
        k_next = k_chunk(c + 1)
        vt = vt_ref[c]
        for g in range(GQA_GROUP):
            nxt_scr[g] = scores(k_next, g)
            accumulate(cur_scr[g], vt, g)

    for g in range(GQA_GROUP):
        s0_scr[g] = scores(k_chunk(0), g)

    def body(j, carry):
        stage(s0_scr, s1_scr, 2 * j)
        stage(s1_scr, s0_scr, 2 * j + 1)
        return carry

    lax.fori_loop(0, n_chunks // 2 - 1, body, 0)
    stage(s0_scr, s1_scr, n_chunks - 2)
    vt_last = vt_ref[n_chunks - 1]
    for g in range(GQA_GROUP):
        s_ctx = scores(kc_ref[...], g)
        accumulate(s1_scr[g], vt_last, g)
        accumulate(s_ctx, vtc_ref[...], g)
    for g in range(GQA_GROUP):
        o_ref[:, g * HEAD_DIM:(g + 1) * HEAD_DIM] = (acc_scr[g] / l_scr[g]).T.astype(o_ref.dtype)


def _gqa_attention(qt, k, vt, *, n_batch, seq, ctx_len):
    n_kv = k.shape[1] // HEAD_DIM
    tq = qt.shape[2]
    tk = vt.shape[2]
    assert (seq // tk) % 2 == 0 and seq % tq == 0
    gw = GQA_GROUP * HEAD_DIM
    n_qt = seq // tq
    q_spec = pl.BlockSpec((tq, gw), lambda b, h, i: (b * n_qt + i, h))
    return pl.pallas_call(
        _gqa_kernel,
        grid=(n_batch, n_kv, n_qt),
        in_specs=[
            pl.BlockSpec((None, gw, tq), lambda b, h, i: (b * n_qt + i, h, 0)),
            pl.BlockSpec((seq, HEAD_DIM), lambda b, h, i: (b, h)),
            pl.BlockSpec((seq // tk, HEAD_DIM, tk), lambda b, h, i: (b, h, 0)),
            pl.BlockSpec((ctx_len, HEAD_DIM), lambda b, h, i: (n_batch * seq // ctx_len + b, h)),
            pl.BlockSpec((None, HEAD_DIM, ctx_len), lambda b, h, i: (n_batch * seq // tk, h, b)),
        ],
        out_specs=q_spec,
        out_shape=jax.ShapeDtypeStruct((n_batch * seq, qt.shape[1]), BF16),
        scratch_shapes=[
            pltpu.VMEM((GQA_GROUP, 1, tq), F32),
            pltpu.VMEM((GQA_GROUP, 1, tq), F32),
            pltpu.VMEM((GQA_GROUP, HEAD_DIM, tq), F32),
            pltpu.VMEM((GQA_GROUP, tk, tq), F32),
            pltpu.VMEM((GQA_GROUP, tk, tq), F32),
        ],
        compiler_params=_params("arbitrary", "arbitrary", "arbitrary"),
        name="gqa_attention",
    )(qt, k, vt, k, vt)


def _rope_partner(v):
    q4 = HEAD_DIM // 4
    return jnp.concatenate([v[..., q4:2 * q4], v[..., :q4], v[..., 3 * q4:], v[..., 2 * q4:3 * q4]], axis=-1)


def _rope_tables_t(cos, sin, g, scale, rows, tm):
    cos_t = (cos[:rows] * (g * scale)[None, :]).reshape(rows // tm, tm, HEAD_DIM)
    sin_t = (sin[:rows] * (_rope_partner(g) * scale)[None, :]).reshape(rows // tm, tm, HEAD_DIM)
    return jnp.transpose(cos_t, (0, 2, 1)), jnp.transpose(sin_t, (0, 2, 1))


def _rope_tables(n_batch, seq, ctx_len):
    quarter = HEAD_DIM // 4
    t = jnp.arange(seq)
    freqs = ROPE_THETA ** (-jnp.arange(quarter, dtype=F32) / quarter)
    ang_r = (t // GRID_W).astype(F32)[:, None] * freqs[None, :]
    ang_c = (t % GRID_W).astype(F32)[:, None] * freqs[None, :]
    cos = jnp.concatenate([jnp.cos(ang_r)] * 2 + [jnp.cos(ang_c)] * 2, axis=-1)
    sin = jnp.concatenate([-jnp.sin(ang_r), jnp.sin(ang_r), -jnp.sin(ang_c), jnp.sin(ang_c)], axis=-1)
    n_ctx = n_batch * ctx_len
    cos = jnp.concatenate([jnp.tile(cos, (n_batch, 1)), jnp.ones((n_ctx, HEAD_DIM), F32)], axis=0)
    sin = jnp.concatenate([jnp.tile(sin, (n_batch, 1)), jnp.zeros((n_ctx, HEAD_DIM), F32)], axis=0)
    return cos, sin


def kernel(x, c, ctx, c_ctx, ada_w, ada_b, norm_g, na_wqkv, na_wo, na_rpb, gqa_wq, gqa_wkv, gqa_q_norm,
           gqa_k_norm, gqa_wo, ffn_w13, ffn_w2):
    n_batch, seq, d = x.shape
    ctx_len = ctx.shape[1]
    depth = ada_w.shape[0]
    assert depth == 2 and na_wqkv.shape[0] == 1 and gqa_wq.shape[0] == 1
    assert seq % GRID_W == 0 and n_batch + 1 <= MOD_ROWS
    n_heads = d // HEAD_DIM
    m_lat = n_batch * seq
    m_all = m_lat + n_batch * ctx_len
    scale = HEAD_DIM ** -0.5 * LOG2E
    tm_all = m_all // 8
    tm_lat = m_lat // 8
    tm_down = 2
    seg_tiles = lambda t: seq // t
    assert tm_all % 16 == 0 and tm_lat % 16 == 0

    cvec = jnp.zeros((MOD_ROWS, d), F32).at[:n_batch].set(c).at[n_batch].set(c_ctx)
    mod = _ada_mod(cvec, ada_w, ada_b).reshape(depth, MOD_ROWS, N_MOD, d)
    cos, sin = _rope_tables(n_batch, seq, ctx_len)
    cos_qt, sin_qt = _rope_tables_t(cos, sin, gqa_q_norm[0], scale, m_lat, tm_lat)

    tn_f32, tn_b16 = 512, 1024

    x_lat = x.reshape(m_lat, d)
    x_ctx = ctx.reshape(n_batch * ctx_len, d)
    seg_kw = dict(seg_tiles=seg_tiles, n_batch=n_batch)

    h = _prenorm(x_lat, x_ctx, norm_g[0], mod[0], g_row=0, sh_row=0, sc_row=1, **seg_kw)
    qkv, (na_wo_b, wq, wkv, gqa_wo_b) = _matmul(
        h, na_wqkv[0], rows=m_all, n_out=na_wqkv.shape[2], out_dtype=BF16, tm=tm_all, tn=tn_f32,
        scale_blocks=d // tn_f32, scale=scale, side_casts=(na_wo[0], gqa_wq[0], gqa_wkv[0], gqa_wo[0]), name="na_qkv")
    o = _na_attention(qkv, na_rpb[0], n_batch=n_batch, seq=seq, ctx_len=ctx_len, n_heads=n_heads)
    o = _ctx_attention(qkv, o, n_batch=n_batch, seq=seq, ctx_len=ctx_len, n_heads=n_heads)
    y = _matmul(o, na_wo_b, rows=m_all, n_out=d, out_dtype=F32, tm=tm_all, tn=tn_b16, name="na_wo")
    xa, h = _resid(y, x_lat, norm_g[0], mod[0], x_ctx=x_ctx, rows=m_all, gt_row=2, gpost_row=1,
                   nxt=(2, 3, 4), g2=norm_g[0], mod2=mod[0], **seg_kw)
    gu, w2 = _gateup(h, ffn_w13, ffn_w2, 0, rows=m_all, tm=2 * tm_all, tn=256)
    y = _matmul(gu, w2, rows=m_all, n_out=d, out_dtype=F32, tm=tm_all // tm_down, tn=512, name="ffn_down")
    xa, h = _resid(y, xa, norm_g[0], mod[0], rows=m_all, gt_row=5, gpost_row=3,
                   nxt=(0, 0, 1), g2=norm_g[1], mod2=mod[1], **seg_kw)

    kv_w = wkv.shape[1] // 2
    qt = _matmul_rope_t(h, wq, cos_qt, sin_qt, rows=m_lat, n_out=d, tn=tn_b16, name="gqa_q")
    k = _matmul_rope(h, wkv, gqa_k_norm[0], cos, sin, rows=m_all, n_out=kv_w, tm=tm_all, tn=kv_w, scale=1.0, name="gqa_k")
    gqa_tk = 512
    assert seq % gqa_tk == 0 and n_batch * ctx_len == gqa_tk
    vt = _matmul_t(h, wkv, rows=m_all, n_out=kv_w, tm=gqa_tk, tn=kv_w, col_blk_off=1, name="gqa_v")
    o = _gqa_attention(qt, k, vt, n_batch=n_batch, seq=seq, ctx_len=ctx_len)
    y = _matmul(o, gqa_wo_b, rows=m_lat, n_out=d, out_dtype=F32, tm=tm_lat, tn=tn_b16, name="gqa_wo")
    xl, h = _resid(y, xa, norm_g[1], mod[1], rows=m_lat, gt_row=2, gpost_row=1,
                   nxt=(2, 3, 4), g2=norm_g[1], mod2=mod[1], **seg_kw)
    gu, w2 = _gateup(h, ffn_w13, ffn_w2, 1, rows=m_lat, tm=2 * tm_lat, tn=256)
    y = _matmul(gu, w2, rows=m_lat, n_out=d, out_dtype=F32, tm=tm_lat // tm_down, tn=512, name="ffn_down")
    xl = _resid(y, xl, norm_g[1], mod[1], rows=m_lat, gt_row=5, gpost_row=3, **seg_kw)
    return xl.reshape(n_batch, seq, d)
```

```python
import functools

import jax
import jax.numpy as jnp
from jax import lax
from jax.experimental import pallas as pl
from jax.experimental.pallas import tpu as pltpu

GRID_W = 64
NA_WIN_H = 8
NA_WIN_W = 16
HEAD_DIM = 128
GQA_GROUP = 4
ROPE_THETA = 10000.0
NORM_EPS = 1e-6
NEG_INF = -1e30
LOG2E = 1.4426950408889634
N_MOD = 6

VMEM_LIMIT_BYTES = 56 * 1024 * 1024
MXU_COLS = 256
MOD_ROWS = 8

F32 = jnp.float32
BF16 = jnp.bfloat16


def _params(*sem):
    return pltpu.CompilerParams(dimension_semantics=sem, vmem_limit_bytes=VMEM_LIMIT_BYTES)


def _rms(x, g):
    ms = jnp.mean(x * x, axis=-1, keepdims=True)
    return x * lax.rsqrt(ms + NORM_EPS) * g


def _seg_index(rows_per_seg_tiles, n_batch):
    return lambda i: jnp.minimum(i // rows_per_seg_tiles, n_batch)


ADA_SLAB = 512


def _ada_slab(c_ref, w_ref, b_ref, o_ref):
    c = c_ref[...]
    s = (c * jax.nn.sigmoid(c)).astype(BF16)
    o_ref[...] = jnp.dot(s, w_ref[...].astype(BF16), preferred_element_type=F32) + b_ref[...]


def _ada_slab_specs(d, n_early, per_layer, step_of):
    def w_map(*idx):
        f = step_of(*idx) + n_early
        return f // per_layer, 0, f % per_layer
    flat = lambda *idx: (0, step_of(*idx))
    return [pl.BlockSpec((MOD_ROWS, d), lambda *idx: (0, 0)),
            pl.BlockSpec((None, d, ADA_SLAB), w_map),
            pl.BlockSpec((1, ADA_SLAB), flat)], pl.BlockSpec((MOD_ROWS, ADA_SLAB), flat)


def _ada_early(cvec, ada_w, b_early):
    d = ada_w.shape[1]
    n = b_early.shape[1]
    in_specs, out_spec = _ada_slab_specs(d, 0, ada_w.shape[2] // ADA_SLAB, lambda j: j)
    return pl.pallas_call(
        _ada_slab,
        grid=(n // ADA_SLAB,),
        in_specs=in_specs,
        out_specs=out_spec,
        out_shape=jax.ShapeDtypeStruct((MOD_ROWS, n), F32),
        compiler_params=_params("arbitrary"),
        name="ada_mod",
    )(cvec, ada_w, b_early)


def _token_rows(x_lat, x_ctx, rows, tm):
    d = x_lat.shape[1]
    n_lat = min(rows, x_lat.shape[0]) // tm
    if x_ctx is None:
        x_ctx = x_lat
        assert rows <= x_lat.shape[0]
    else:
        assert x_lat.shape[0] % tm == 0 and rows == x_lat.shape[0] + x_ctx.shape[0]
    specs = [pl.BlockSpec((tm, d), lambda i: (jnp.minimum(i, n_lat - 1), 0)),
             pl.BlockSpec((tm, d), lambda i: (jnp.maximum(i - n_lat, 0), 0))]
    return [x_lat, x_ctx], specs, n_lat


def _read_token_rows(xl_ref, xc_ref, n_lat):
    return jnp.where(pl.program_id(0) < n_lat, xl_ref[...], xc_ref[...])


def _prenorm_kernel(xl_ref, xc_ref, g_ref, mod_ref, h_ref, *, n_lat, g_row, sh_row, sc_row):
    y = _rms(_read_token_rows(xl_ref, xc_ref, n_lat), g_ref[g_row:g_row + 1, :])
    h = y * (1.0 + mod_ref[sc_row:sc_row + 1, :]) + mod_ref[sh_row:sh_row + 1, :]
    h_ref[...] = h.astype(h_ref.dtype)


def _prenorm(x_lat, x_ctx, g, mod, *, seg_tiles, n_batch, g_row, sh_row, sc_row, tm=256):
    d = x_lat.shape[1]
    m = x_lat.shape[0] + x_ctx.shape[0]
    seg = _seg_index(seg_tiles(tm), n_batch)
    x_args, x_specs, n_lat = _token_rows(x_lat, x_ctx, m, tm)
    return pl.pallas_call(
        functools.partial(_prenorm_kernel, n_lat=n_lat, g_row=g_row, sh_row=sh_row, sc_row=sc_row),
        grid=(m // tm,),
        in_specs=x_specs + [
            pl.BlockSpec(g.shape, lambda i: (0, 0)),
            pl.BlockSpec((None, mod.shape[1], d), lambda i: (seg(i), 0, 0)),
        ],
        out_specs=pl.BlockSpec((tm, d), lambda i: (i, 0)),
        out_shape=jax.ShapeDtypeStruct((m, d), BF16),
        compiler_params=_params("arbitrary"),
        name="prenorm",
    )(*x_args, g, mod)


def _resid_kernel(y_ref, xl_ref, xc_ref, g_ref, mod_ref, *rest, n_lat, gt_row, gpost_row, nxt):
    x = _read_token_rows(xl_ref, xc_ref, n_lat)
    xn = x + mod_ref[gt_row:gt_row + 1, :] * _rms(y_ref[...], g_ref[gpost_row:gpost_row + 1, :])
    if nxt is None:
        (xo_ref,) = rest
        xo_ref[...] = xn
        return
    g2_ref, mod2_ref, xo_ref, h_ref = rest
    gpre_row, sh_row, sc_row = nxt
    xo_ref[...] = xn
    h = _rms(xn, g2_ref[gpre_row:gpre_row + 1, :])
    h = h * (1.0 + mod2_ref[sc_row:sc_row + 1, :]) + mod2_ref[sh_row:sh_row + 1, :]
    h_ref[...] = h.astype(h_ref.dtype)


def _resid(y, x, g, mod, *, rows, seg_tiles, n_batch, gt_row, gpost_row, x_ctx=None, nxt=None, g2=None, mod2=None,
           tm=256):
    d = x.shape[1]
    seg = _seg_index(seg_tiles(tm), n_batch)
    row_spec = pl.BlockSpec((tm, d), lambda i: (i, 0))
    mod_spec = pl.BlockSpec((None, N_MOD, d), lambda i: (seg(i), 0, 0))
    x_args, x_specs, n_lat = _token_rows(x, x_ctx, rows, tm)
    in_specs = [row_spec] + x_specs + [pl.BlockSpec(g.shape, lambda i: (0, 0)), mod_spec]
    args = [y] + x_args + [g, mod]
    out_specs = [row_spec]
    out_shape = [jax.ShapeDtypeStruct((rows, d), F32)]
    if nxt is not None:
        in_specs += [pl.BlockSpec(g2.shape, lambda i: (0, 0)), mod_spec]
        args += [g2, mod2]
        out_specs.append(row_spec)
        out_shape.append(jax.ShapeDtypeStruct((rows, d), BF16))
    out = pl.pallas_call(
        functools.partial(_resid_kernel, n_lat=n_lat, gt_row=gt_row, gpost_row=gpost_row, nxt=nxt),
        grid=(rows // tm,),
        in_specs=in_specs,
        out_specs=out_specs,
        out_shape=out_shape,
        compiler_params=_params("arbitrary"),
        name="resid_norm",
    )(*args)
    return out if nxt is not None else out[0]


SIDE_CAST_ROWS = 32


def _mm_kernel(x_ref, w_ref, *rest, scale_blocks, scale, n_side):
    side_in, o_ref, side_out = rest[:n_side], rest[n_side], rest[n_side + 1:]
    acc = jnp.dot(x_ref[...], w_ref[...].astype(BF16), preferred_element_type=F32)
    if scale_blocks:
        acc = acc * jnp.where(pl.program_id(1) < scale_blocks, scale, 1.0)
    o_ref[...] = acc.astype(o_ref.dtype)
    for src, dst in zip(side_in, side_out):
        dst[...] = src[...].astype(dst.dtype)


def _matmul(x, w, *, rows, n_out, out_dtype, tm, tn, col_blk_off=0, scale_blocks=0, scale=1.0, side_casts=(),
            name="matmul"):
    k = x.shape[1]
    n_j = n_out // tn
    n_steps = (rows // tm) * n_j
    in_specs = [
        pl.BlockSpec((tm, k), lambda i, j: (i, 0)),
        pl.BlockSpec((k, tn), lambda i, j: (0, j + col_blk_off)),
    ]
    out_specs = [pl.BlockSpec((tm, tn), lambda i, j: (i, j))]
    out_shape = [jax.ShapeDtypeStruct((rows, n_out), out_dtype)]
    for ws in side_casts:
        n_slabs = ws.shape[0] // SIDE_CAST_ROWS
        assert n_slabs * SIDE_CAST_ROWS == ws.shape[0] and n_slabs <= n_steps
        spec = pl.BlockSpec((SIDE_CAST_ROWS, ws.shape[1]),
                            lambda i, j, n_slabs=n_slabs: (jnp.minimum(i * n_j + j, n_slabs - 1), 0))
        in_specs.append(spec)
        out_specs.append(spec)
        out_shape.append(jax.ShapeDtypeStruct(ws.shape, BF16))
    out = pl.pallas_call(
        functools.partial(_mm_kernel, scale_blocks=scale_blocks, scale=scale, n_side=len(side_casts)),
        grid=(rows // tm, n_j),
        in_specs=in_specs,
        out_specs=out_specs,
        out_shape=out_shape,
        compiler_params=_params("arbitrary", "arbitrary"),
        name=name,
    )(x, w, *side_casts)
    return (out[0], out[1:]) if side_casts else out[0]


def _mm_t_kernel(x_ref, w_ref, o_ref):
    acc = jnp.dot(x_ref[...], w_ref[...].astype(BF16), preferred_element_type=F32)
    o_ref[...] = acc.T.astype(o_ref.dtype)


def _matmul_t(x, w, *, rows, n_out, tm, tn, col_blk_off=0, name="matmul_t"):
    k = x.shape[1]
    return pl.pallas_call(
        _mm_t_kernel,
        grid=(n_out // tn, rows // tm),
        in_specs=[
            pl.BlockSpec((tm, k), lambda j, i: (i, 0)),
            pl.BlockSpec((k, tn), lambda j, i: (0, j + col_blk_off)),
        ],
        out_specs=pl.BlockSpec((None, tn, tm), lambda j, i: (i, j, 0)),
        out_shape=jax.ShapeDtypeStruct((rows // tm, n_out, tm), BF16),
        compiler_params=_params("arbitrary", "arbitrary"),
        name=name,
    )(x, w)


def _swap_halves(y):
    lane = lax.broadcasted_iota(jnp.int32, y.shape, 1)
    return jnp.where((lane & 32) == 0, pltpu.roll(y, 96, 1), pltpu.roll(y, 32, 1))


def _mm_rope_kernel(x_ref, w_ref, g_ref, cos_ref, sin_ref, o_ref, *, scale):
    cos = cos_ref[...]
    sin = sin_ref[...]
    g = g_ref[...]
    x = x_ref[...]
    for grp in range(w_ref.shape[1] // MXU_COLS):
        w = w_ref[:, grp * MXU_COLS:(grp + 1) * MXU_COLS].astype(BF16)
        acc = jnp.dot(x, w, preferred_element_type=F32)
        for hh in range(MXU_COLS // HEAD_DIM):
            y = _rms(acc[:, hh * HEAD_DIM:(hh + 1) * HEAD_DIM], g)
            y = y * cos + _swap_halves(y) * sin
            if scale != 1.0:
                y = y * scale
            col0 = grp * MXU_COLS + hh * HEAD_DIM
            o_ref[:, col0:col0 + HEAD_DIM] = y.astype(o_ref.dtype)


def _matmul_rope(x, w, g, cos, sin, *, rows, n_out, tm, tn, scale, name):
    k = x.shape[1]
    tab_spec = pl.BlockSpec((tm, HEAD_DIM), lambda i, j: (i, 0))
    return pl.pallas_call(
        functools.partial(_mm_rope_kernel, scale=scale),
        grid=(rows // tm, n_out // tn),
        in_specs=[
            pl.BlockSpec((tm, k), lambda i, j: (i, 0)),
            pl.BlockSpec((k, tn), lambda i, j: (0, j)),
            pl.BlockSpec((1, HEAD_DIM), lambda i, j: (0, 0)),
            tab_spec,
            tab_spec,
        ],
        out_specs=pl.BlockSpec((tm, tn), lambda i, j: (i, j)),
        out_shape=jax.ShapeDtypeStruct((rows, n_out), BF16),
        compiler_params=_params("arbitrary", "arbitrary"),
        name=name,
    )(x, w, g.reshape(1, HEAD_DIM), cos, sin)


def _mm_rope_t_kernel(x_ref, w_ref, cos_ref, sin_ref, o_ref):
    cos = cos_ref[...]
    sin = sin_ref[...]
    x = x_ref[...]
    q4 = HEAD_DIM // 4
    for grp in range(w_ref.shape[1] // MXU_COLS):
        w = w_ref[:, grp * MXU_COLS:(grp + 1) * MXU_COLS].astype(BF16)
        acc_t = jnp.dot(x, w, preferred_element_type=F32).T
        for hh in range(MXU_COLS // HEAD_DIM):
            y = acc_t[hh * HEAD_DIM:(hh + 1) * HEAD_DIM, :]
            r = lax.rsqrt(jnp.mean(y * y, axis=0, keepdims=True) + NORM_EPS)
            partner = jnp.concatenate([y[q4:2 * q4], y[:q4], y[3 * q4:], y[2 * q4:3 * q4]], axis=0)
            row0 = grp * MXU_COLS + hh * HEAD_DIM
            o_ref[row0:row0 + HEAD_DIM, :] = ((y * cos + partner * sin) * r).astype(o_ref.dtype)


def _matmul_rope_t(x, w, cos_t, sin_t, *, rows, n_out, tn, name):
    k = x.shape[1]
    n_tiles, _, tm = cos_t.shape
    assert n_tiles * tm == rows
    tab_spec = pl.BlockSpec((None, HEAD_DIM, tm), lambda i, j: (i, 0, 0))
    return pl.pallas_call(
        _mm_rope_t_kernel,
        grid=(n_tiles, n_out // tn),
        in_specs=[
            pl.BlockSpec((tm, k), lambda i, j: (i, 0)),
            pl.BlockSpec((k, tn), lambda i, j: (0, j)),
            tab_spec,
            tab_spec,
        ],
        out_specs=pl.BlockSpec((None, tn, tm), lambda i, j: (i, j, 0)),
        out_shape=jax.ShapeDtypeStruct((n_tiles, n_out, tm), BF16),
        compiler_params=_params("arbitrary", "arbitrary"),
        name=name,
    )(x, w, cos_t, sin_t)


def _gateup_kernel(x_ref, w1_ref, w3_ref, w2_ref, o_ref, w2b_ref):
    x = x_ref[...]
    a = jnp.dot(x, w1_ref[...].astype(BF16), preferred_element_type=F32)
    b = jnp.dot(x, w3_ref[...].astype(BF16), preferred_element_type=F32)
    o_ref[...] = (a * jax.nn.sigmoid(a) * b).astype(o_ref.dtype)
    w2b_ref[...] = w2_ref[...].astype(w2b_ref.dtype)


def _gateup(x, w13, w2, layer, *, rows, tm, tn):
    k = x.shape[1]
    f = w13.shape[2] // 2
    n_j = f // tn
    steps = (rows // tm) * n_j
    slab = w2.shape[1] // steps
    assert slab * steps == w2.shape[1] and slab % 16 == 0
    return pl.pallas_call(
        _gateup_kernel,
        grid=(rows // tm, n_j),
        in_specs=[
            pl.BlockSpec((tm, k), lambda i, j: (i, 0), pipeline_mode=pl.Buffered(1)),
            pl.BlockSpec((None, k, tn), lambda i, j: (layer, 0, j)),
            pl.BlockSpec((None, k, tn), lambda i, j: (layer, 0, j + n_j)),
            pl.BlockSpec((None, slab, w2.shape[2]), lambda i, j: (layer, i * n_j + j, 0)),
        ],
        out_specs=[
            pl.BlockSpec((tm, tn), lambda i, j: (i, j)),
            pl.BlockSpec((slab, w2.shape[2]), lambda i, j: (i * n_j + j, 0)),
        ],
        out_shape=[
            jax.ShapeDtypeStruct((rows, f), BF16),
            jax.ShapeDtypeStruct(w2.shape[1:], BF16),
        ],
        compiler_params=_params("arbitrary", "arbitrary"),
        name="ffn_gateup",
    )(x, w13, w13, w2)


_NT = (((1,), (1,)), ((), ()))


def _na_block_rows(rb):
    return rb + NA_WIN_H


N_DR = 2 * NA_WIN_H - 1
N_DC = 2 * NA_WIN_W - 1
_TAB_BOTH, _TAB_FIRST, _TAB_SECOND, _TAB_NONE = 0, N_DR + 1, 2 * N_DR + 1, 3 * N_DR + 1
_TAB_SIZE = 3 * N_DR + 2


def _na_build_bias_tiles(rpb_ref, tab_scr, h0, hb):
    shape = (GRID_W, 2 * GRID_W)
    c = lax.broadcasted_iota(jnp.int32, shape, 0)
    lane = lax.broadcasted_iota(jnp.int32, shape, 1)
    kc = lane & (GRID_W - 1)
    second = lane >= GRID_W
    cs = jnp.clip(c - NA_WIN_W // 2, 0, GRID_W - NA_WIN_W)
    in_win = (kc >= cs) & (kc < cs + NA_WIN_W)
    dci = kc - c + (NA_WIN_W - 1)
    is_dc = [dci == k for k in range(N_DC)]
    neg = jnp.full(shape, NEG_INF, F32)
    for hh in range(hb):
        base = (h0 + hh) * (N_DR * N_DC)
        rows = []
        for d in range(N_DR):
            t = neg
            for k in range(N_DC):
                t = jnp.where(is_dc[k], rpb_ref[base + d * N_DC + k] * LOG2E, t)
            rows.append(jnp.where(in_win, t, NEG_INF))
        for d in range(N_DR + 1):
            lo = rows[d - 1] if d >= 1 else neg
            hi = rows[d] if d < N_DR else neg
            tab_scr[hh, _TAB_BOTH + d] = jnp.where(second, hi, lo)
        for d in range(N_DR):
            tab_scr[hh, _TAB_FIRST + d] = jnp.where(second, neg, rows[d])
            tab_scr[hh, _TAB_SECOND + d] = jnp.where(second, rows[d], neg)
        tab_scr[hh, _TAB_NONE] = neg


def _na_kernel(rpb_ref, q_ref, k_ref, v_ref, kc_ref, vc_ref, c_ref, aw_ref, ab_ref, o_ref, mod_ref, tab_scr, *,
               rb, hb, nsb, rows, n_ada):
    kr = _na_block_rows(rb)
    qn = rb * GRID_W
    half = NA_WIN_H // 2
    step = (pl.program_id(0) * pl.num_programs(1) + pl.program_id(1)) * pl.num_programs(2) + pl.program_id(2)

    @pl.when(step < n_ada)
    def _():
        _ada_slab(c_ref, aw_ref, ab_ref, mod_ref)

    @pl.when((pl.program_id(1) == 0) & (pl.program_id(2) == 0))
    def _():
        _na_build_bias_tiles(rpb_ref, tab_scr, pl.program_id(0) * hb, hb)

    starts, tiles = [], []
    for sb in range(nsb):
        r0 = (pl.program_id(2) * nsb + sb) * rb
        kstart = jnp.clip(r0 - half, 0, rows - kr)
        starts.append(pl.multiple_of(kstart * GRID_W, GRID_W))
        tile_idx = []
        for i in range(rb):
            r = r0 + i
            rs = jnp.clip(r - half, 0, rows - NA_WIN_H)
            row_idx = []
            for jp in range(kr // 2):
                k0 = kstart + 2 * jp
                v0 = (k0 >= rs) & (k0 < rs + NA_WIN_H)
                v1 = (k0 + 1 >= rs) & (k0 + 1 < rs + NA_WIN_H)
                d0 = k0 - r + NA_WIN_H - 1
                idx = jnp.where(v0 & v1, _TAB_BOTH + d0 + 1,
                                jnp.where(v0, _TAB_FIRST + d0, jnp.where(v1, _TAB_SECOND + d0 + 1, _TAB_NONE)))
                row_idx.append(jnp.clip(idx, 0, _TAB_SIZE - 1))
            tile_idx.append(row_idx)
        tiles.append(tile_idx)

    def scores(sb, hh):
        cols = slice(hh * HEAD_DIM, (hh + 1) * HEAD_DIM)
        q = q_ref[sb * qn:(sb + 1) * qn, cols]
        ku = k_ref[pl.ds(starts[sb], kr * GRID_W), cols]
        bias = jnp.concatenate(
            [jnp.concatenate([tab_scr[hh, idx] for idx in row_idx], axis=1) for row_idx in tiles[sb]], axis=0)
        s_loc = lax.dot_general(q, ku, _NT, preferred_element_type=F32) + bias
        s_ctx = lax.dot_general(q, kc_ref[:, cols], _NT, preferred_element_type=F32)
        return s_loc, s_ctx

    def finish(sb, hh, s_loc, s_ctx):
        cols = slice(hh * HEAD_DIM, (hh + 1) * HEAD_DIM)
        vu = v_ref[pl.ds(starts[sb], kr * GRID_W), cols]
        m = jnp.maximum(jnp.max(s_loc, axis=-1, keepdims=True), jnp.max(s_ctx, axis=-1, keepdims=True))
        p_loc = jnp.exp2(s_loc - m)
        p_ctx = jnp.exp2(s_ctx - m)
        l = jnp.sum(p_loc, axis=-1, keepdims=True) + jnp.sum(p_ctx, axis=-1, keepdims=True)
        o = (jnp.dot(p_loc.astype(BF16), vu, preferred_element_type=F32)
             + jnp.dot(p_ctx.astype(BF16), vc_ref[:, cols], preferred_element_type=F32))
        o_ref[sb * qn:(sb + 1) * qn, cols] = (o / l).astype(o_ref.dtype)

    pairs = [(sb, hh) for sb in range(nsb) for hh in range(hb)]
    ahead = 1
    pending = [scores(*pair) for pair in pairs[:ahead]]
    for n, pair in enumerate(pairs):
        if n + ahead < len(pairs):
            pending.append(scores(*pairs[n + ahead]))
        finish(*pair, *pending.pop(0))


def _na_attention(qkv, rpb, cvec, ada_w, b_late, *, n_batch, seq, ctx_len, n_heads, rb=4, hb=2, nsb=4):
    m_all = qkv.shape[0]
    rows = seq // GRID_W
    assert rpb.shape == (n_heads, N_DR, N_DC) and _na_block_rows(rb) % 2 == 0 and rows % (rb * nsb) == 0
    hw = hb * HEAD_DIM
    hblocks = n_heads // hb
    qrows = nsb * rb * GRID_W
    n_r = rows // (rb * nsb)
    depth, d, n_mod = ada_w.shape
    n_late = b_late.shape[1] // ADA_SLAB
    n_early = depth * n_mod // ADA_SLAB - n_late
    assert n_late <= hblocks * n_batch * n_r
    step_of = lambda h, b, r, _: jnp.minimum((h * n_batch + b) * n_r + r, n_late - 1)
    ada_in_specs, ada_out_spec = _ada_slab_specs(d, n_early, n_mod // ADA_SLAB, step_of)
    lat_spec = lambda part: pl.BlockSpec((seq, hw), lambda h, b, r, _: (b, part * hblocks + h))
    ctx_spec = lambda part: pl.BlockSpec((ctx_len, hw),
                                         lambda h, b, r, _: (n_batch * seq // ctx_len + b, part * hblocks + h))
    q_spec = pl.BlockSpec((qrows, hw), lambda h, b, r, _: (b * (seq // qrows) + r, h))
    return pl.pallas_call(
        functools.partial(_na_kernel, rb=rb, hb=hb, nsb=nsb, rows=rows, n_ada=n_late),
        grid_spec=pltpu.PrefetchScalarGridSpec(
            num_scalar_prefetch=1,
            grid=(hblocks, n_batch, n_r),
            in_specs=[q_spec, lat_spec(1), lat_spec(2), ctx_spec(1), ctx_spec(2)] + ada_in_specs,
            out_specs=[q_spec, ada_out_spec],
            scratch_shapes=[pltpu.VMEM((hb, _TAB_SIZE, GRID_W, 2 * GRID_W), F32)],
        ),
        out_shape=[jax.ShapeDtypeStruct((m_all, n_heads * HEAD_DIM), BF16),
                   jax.ShapeDtypeStruct((MOD_ROWS, b_late.shape[1]), F32)],
        compiler_params=_params("arbitrary", "arbitrary", "arbitrary"),
        name="na_attention",
    )(rpb.reshape(-1), qkv, qkv, qkv, qkv, qkv, cvec, ada_w, b_late)


def _ctx_attn_kernel(q_ref, k_ref, v_ref, o_in_ref, o_ref, *, hb):
    del o_in_ref
    for hh in range(hb):
        cols = slice(hh * HEAD_DIM, (hh + 1) * HEAD_DIM)
        s = lax.dot_general(q_ref[:, cols], k_ref[:, cols], _NT, preferred_element_type=F32)
        p = jnp.exp2(s - jnp.max(s, axis=-1, keepdims=True))
        l = jnp.sum(p, axis=-1, keepdims=True)
        o = jnp.dot(p.astype(BF16), v_ref[:, cols], preferred_element_type=F32)
        o_ref[:, cols] = (o / l).astype(o_ref.dtype)


def _ctx_attention(qkv, o, *, n_batch, seq, ctx_len, n_heads, hb=2):
    hw = hb * HEAD_DIM
    hblocks = n_heads // hb
    row0 = n_batch * seq // ctx_len
    spec = lambda part: pl.BlockSpec((ctx_len, hw), lambda b, h: (row0 + b, part * hblocks + h))
    return pl.pallas_call(
        functools.partial(_ctx_attn_kernel, hb=hb),
        grid=(n_batch, hblocks),
        in_specs=[spec(0), spec(1), spec(2), pl.BlockSpec(memory_space=pl.ANY)],
        out_specs=spec(0),
        out_shape=jax.ShapeDtypeStruct(o.shape, o.dtype),
        input_output_aliases={3: 0},
        compiler_params=_params("arbitrary", "arbitrary"),
        name="ctx_attention",
    )(qkv, qkv, qkv, o)


def _gqa_kernel(q_ref, k_ref, vt_ref, kc_ref, vtc_ref, o_ref, m_scr, l_scr, acc_scr, s0_scr, s1_scr):
    n_chunks, _, tk = vt_ref.shape
    m_scr[...] = jnp.full(m_scr.shape, NEG_INF, F32)
    l_scr[...] = jnp.zeros(l_scr.shape, F32)
    acc_scr[...] = jnp.zeros(acc_scr.shape, F32)

    def scores(k, g):
        return jnp.dot(k, q_ref[g * HEAD_DIM:(g + 1) * HEAD_DIM, :], preferred_element_type=F32)

    def accumulate(s, vt, g):
        m_old = m_scr[g]
        m_new = jnp.maximum(m_old, jnp.max(s, axis=0, keepdims=True))
        alpha = jnp.exp2(m_old - m_new)
        p = jnp.exp2(s - m_new)
        l_scr[g] = alpha * l_scr[g] + jnp.sum(p, axis=0, keepdims=True)
        acc_scr[g] = alpha * acc_scr[g] + jnp.dot(vt, p.astype(BF16), preferred_element_type=F32)
        m_scr[g] = m_new

    def k_chunk(c):
        return k_ref[pl.ds(pl.multiple_of(c * tk, tk), tk), :]

    def stage(cur_scr, nxt_scr, c):
        k_next = k_chunk(c + 1)
        vt = vt_ref[c]
        for g in range(GQA_GROUP):
            nxt_scr[g] = scores(k_next, g)
            accumulate(cur_scr[g], vt, g)

    for g in range(GQA_GROUP):
        s0_scr[g] = scores(k_chunk(0), g)

    def body(j, carry):
        stage(s0_scr, s1_scr, 2 * j)
        stage(s1_scr, s0_scr, 2 * j + 1)
        return carry

    lax.fori_loop(0, n_chunks // 2 - 1, body, 0)
    stage(s0_scr, s1_scr, n_chunks - 2)
    vt_last = vt_ref[n_chunks - 1]
    for g in range(GQA_GROUP):
        s_ctx = scores(kc_ref[...], g)
        accumulate(s1_scr[g], vt_last, g)
        accumulate(s_ctx, vtc_ref[...], g)
    for g in range(GQA_GROUP):
        o_ref[:, g * HEAD_DIM:(g + 1) * HEAD_DIM] = (acc_scr[g] / l_scr[g]).T.astype(o_ref.dtype)


def _gqa_attention(qt, k, vt, *, n_batch, seq, ctx_len):
    n_kv = k.shape[1] // HEAD_DIM
    tq = qt.shape[2]
    tk = vt.shape[2]
    assert (seq // tk) % 2 == 0 and seq % tq == 0
    gw = GQA_GROUP * HEAD_DIM
    n_qt = seq // tq
    q_spec = pl.BlockSpec((tq, gw), lambda b, h, i: (b * n_qt + i, h))
    return pl.pallas_call(
        _gqa_kernel,
        grid=(n_batch, n_kv, n_qt),
        in_specs=[
            pl.BlockSpec((None, gw, tq), lambda b, h, i: (b * n_qt + i, h, 0)),
            pl.BlockSpec((seq, HEAD_DIM), lambda b, h, i: (b, h)),
            pl.BlockSpec((seq // tk, HEAD_DIM, tk), lambda b, h, i: (b, h, 0)),
            pl.BlockSpec((ctx_len, HEAD_DIM), lambda b, h, i: (n_batch * seq // ctx_len + b, h)),
            pl.BlockSpec((None, HEAD_DIM, ctx_len), lambda b, h, i: (n_batch * seq // tk, h, b)),
        ],
        out_specs=q_spec,
        out_shape=jax.ShapeDtypeStruct((n_batch * seq, qt.shape[1]), BF16),
        scratch_shapes=[
            pltpu.VMEM((GQA_GROUP, 1, tq), F32),
            pltpu.VMEM((GQA_GROUP, 1, tq), F32),
            pltpu.VMEM((GQA_GROUP, HEAD_DIM, tq), F32),
            pltpu.VMEM((GQA_GROUP, tk, tq), F32),
            pltpu.VMEM((GQA_GROUP, tk, tq), F32),
        ],
        compiler_params=_params("arbitrary", "arbitrary", "arbitrary"),
        name="gqa_attention",
    )(qt, k, vt, k, vt)


def _rope_partner(v):
    q4 = HEAD_DIM // 4
    return jnp.concatenate([v[..., q4:2 * q4], v[..., :q4], v[..., 3 * q4:], v[..., 2 * q4:3 * q4]], axis=-1)


def _rope_tables_t(cos, sin, g, scale, rows, tm):
    cos_t = (cos[:rows] * (g * scale)[None, :]).reshape(rows // tm, tm, HEAD_DIM)
    sin_t = (sin[:rows] * (_rope_partner(g) * scale)[None, :]).reshape(rows // tm, tm, HEAD_DIM)
    return jnp.transpose(cos_t, (0, 2, 1)), jnp.transpose(sin_t, (0, 2, 1))


def _rope_tables(n_batch, seq, ctx_len):
    quarter = HEAD_DIM // 4
    t = jnp.arange(seq)
    freqs = ROPE_THETA ** (-jnp.arange(quarter, dtype=F32) / quarter)
    ang_r = (t // GRID_W).astype(F32)[:, None] * freqs[None, :]
    ang_c = (t % GRID_W).astype(F32)[:, None] * freqs[None, :]
    cos = jnp.concatenate([jnp.cos(ang_r)] * 2 + [jnp.cos(ang_c)] * 2, axis=-1)
    sin = jnp.concatenate([-jnp.sin(ang_r), jnp.sin(ang_r), -jnp.sin(ang_c), jnp.sin(ang_c)], axis=-1)
    n_ctx = n_batch * ctx_len
    cos = jnp.concatenate([jnp.tile(cos, (n_batch, 1)), jnp.ones((n_ctx, HEAD_DIM), F32)], axis=0)
    sin = jnp.concatenate([jnp.tile(sin, (n_batch, 1)), jnp.zeros((n_ctx, HEAD_DIM), F32)], axis=0)
    return cos, sin


def kernel(x, c, ctx, c_ctx, ada_w, ada_b, norm_g, na_wqkv, na_wo, na_rpb, gqa_wq, gqa_wkv, gqa_q_norm,
           gqa_k_norm, gqa_wo, ffn_w13, ffn_w2):
    n_batch, seq, d = x.shape
    ctx_len = ctx.shape[1]
    depth = ada_w.shape[0]
    assert depth == 2 and na_wqkv.shape[0] == 1 and gqa_wq.shape[0] == 1
    assert seq % GRID_W == 0 and n_batch + 1 <= MOD_ROWS
    n_heads = d // HEAD_DIM
    m_lat = n_batch * seq
    m_all = m_lat + n_batch * ctx_len
    scale = HEAD_DIM ** -0.5 * LOG2E
    tm_all = m_all // 8
    tm_lat = m_lat // 8
    tm_down = 2
    seg_tiles = lambda t: seq // t
    assert tm_all % 16 == 0 and tm_lat % 16 == 0

    cvec = jnp.zeros((MOD_ROWS, d), F32).at[:n_batch].set(c).at[n_batch].set(c_ctx)
    n_mod_early = 2 * d
    b_flat = ada_b.reshape(1, depth * N_MOD * d)
    mod_early = _ada_early(cvec, ada_w, b_flat[:, :n_mod_early])
    cos, sin = _rope_tables(n_batch, seq, ctx_len)
    cos_qt, sin_qt = _rope_tables_t(cos, sin, gqa_q_norm[0], scale, m_lat, tm_lat)

    tn_f32, tn_b16 = 512, 1024

    x_lat = x.reshape(m_lat, d)
    x_ctx = ctx.reshape(n_batch * ctx_len, d)
    seg_kw = dict(seg_tiles=seg_tiles, n_batch=n_batch)

    h = _prenorm(x_lat, x_ctx, norm_g[0], mod_early.reshape(MOD_ROWS, 2, d), g_row=0, sh_row=0, sc_row=1, **seg_kw)
    qkv, (na_wo_b, wq, wkv, gqa_wo_b) = _matmul(
        h, na_wqkv[0], rows=m_all, n_out=na_wqkv.shape[2], out_dtype=BF16, tm=tm_all, tn=tn_f32,
        scale_blocks=d // tn_f32, scale=scale, side_casts=(na_wo[0], gqa_wq[0], gqa_wkv[0], gqa_wo[0]), name="na_qkv")
    o, mod_late = _na_attention(qkv, na_rpb[0], cvec, ada_w, b_flat[:, n_mod_early:],
                                n_batch=n_batch, seq=seq, ctx_len=ctx_len, n_heads=n_heads)
    mod = jnp.concatenate([mod_early, mod_late], axis=1).reshape(MOD_ROWS, depth, N_MOD, d)
    mod = [mod[:, i] for i in range(depth)]
    o = _ctx_attention(qkv, o, n_batch=n_batch, seq=seq, ctx_len=ctx_len, n_heads=n_heads)
    y = _matmul(o, na_wo_b, rows=m_all, n_out=d, out_dtype=F32, tm=tm_all, tn=tn_b16, name="na_wo")
    xa, h = _resid(y, x_lat, norm_g[0], mod[0], x_ctx=x_ctx, rows=m_all, gt_row=2, gpost_row=1,
                   nxt=(2, 3, 4), g2=norm_g[0], mod2=mod[0], **seg_kw)
    gu, w2 = _gateup(h, ffn_w13, ffn_w2, 0, rows=m_all, tm=2 * tm_all, tn=256)
    y = _matmul(gu, w2, rows=m_all, n_out=d, out_dtype=F32, tm=tm_all // tm_down, tn=512, name="ffn_down")
    xa, h = _resid(y, xa, norm_g[0], mod[0], rows=m_all, gt_row=5, gpost_row=3,
                   nxt=(0, 0, 1), g2=norm_g[1], mod2=mod[1], **seg_kw)

    kv_w = wkv.shape[1] // 2
    qt = _matmul_rope_t(h, wq, cos_qt, sin_qt, rows=m_lat, n_out=d, tn=tn_b16, name="gqa_q")
    k = _matmul_rope(h, wkv, gqa_k_norm[0], cos, sin, rows=m_all, n_out=kv_w, tm=tm_all, tn=kv_w, scale=1.0, name="gqa_k")
    gqa_tk = 512
    assert seq % gqa_tk == 0 and n_batch * ctx_len == gqa_tk
    vt = _matmul_t(h, wkv, rows=m_all, n_out=kv_w, tm=gqa_tk, tn=kv_w, col_blk_off=1, name="gqa_v")
    o = _gqa_attention(qt, k, vt, n_batch=n_batch, seq=seq, ctx_len=ctx_len)
    y = _matmul(o, gqa_wo_b, rows=m_lat, n_out=d, out_dtype=F32, tm=tm_lat, tn=tn_b16, name="gqa_wo")
    xl, h = _resid(y, xa, norm_g[1], mod[1], rows=m_lat, gt_row=2, gpost_row=1,
                   nxt=(2, 3, 4), g2=norm_g[1], mod2=mod[1], **seg_kw)
    gu, w2 = _gateup(h, ffn_w13, ffn_w2, 1, rows=m_lat, tm=2 * tm_lat, tn=256)
    y = _matmul(gu, w2, rows=m_lat, n_out=d, out_dtype=F32, tm=tm_lat // tm_down, tn=512, name="ffn_down")
    xl = _resid(y, xl, norm_g[1], mod[1], rows=m_lat, gt_row=5, gpost_row=3, **seg_kw)
    return xl.reshape(n_batch, seq, d)
```

```python
import functools

import jax
import jax.numpy as jnp
from jax import lax
from jax.experimental import pallas as pl
from jax.experimental.pallas import tpu as pltpu

GRID_W = 64
NA_WIN_H = 8
NA_WIN_W = 16
HEAD_DIM = 128
GQA_GROUP = 4
ROPE_THETA = 10000.0
NORM_EPS = 1e-6
NEG_INF = -1e30
LOG2E = 1.4426950408889634
N_MOD = 6

VMEM_LIMIT_BYTES = 56 * 1024 * 1024
MXU_COLS = 256
MOD_ROWS = 8

F32 = jnp.float32
BF16 = jnp.bfloat16


def _params(*sem):
    return pltpu.CompilerParams(dimension_semantics=sem, vmem_limit_bytes=VMEM_LIMIT_BYTES)


def _rms(x, g):
    ms = jnp.mean(x * x, axis=-1, keepdims=True)
    return x * lax.rsqrt(ms + NORM_EPS) * g


def _seg_index(rows_per_seg_tiles, n_batch):
    return lambda i: jnp.minimum(i // rows_per_seg_tiles, n_batch)


ADA_SLAB = 1024


def _ada_slab(c_ref, w_ref, b_ref, o_ref):
    c = c_ref[...]
    s = (c * jax.nn.sigmoid(c)).astype(BF16)
    o_ref[...] = jnp.dot(s, w_ref[...].astype(BF16), preferred_element_type=F32) + b_ref[...]


def _ada_slab_specs(d, n_early, per_layer, step_of):
    def w_map(*idx):
        f = step_of(*idx) + n_early
        return f // per_layer, 0, f % per_layer
    flat = lambda *idx: (0, step_of(*idx))
    return [pl.BlockSpec((MOD_ROWS, d), lambda *idx: (0, 0)),
            pl.BlockSpec((None, d, ADA_SLAB), w_map),
            pl.BlockSpec((1, ADA_SLAB), flat)], pl.BlockSpec((MOD_ROWS, ADA_SLAB), flat)


def _ada_early(cvec, ada_w, b_early):
    d = ada_w.shape[1]
    n = b_early.shape[1]
    in_specs, out_spec = _ada_slab_specs(d, 0, ada_w.shape[2] // ADA_SLAB, lambda j: j)
    return pl.pallas_call(
        _ada_slab,
        grid=(n // ADA_SLAB,),
        in_specs=in_specs,
        out_specs=out_spec,
        out_shape=jax.ShapeDtypeStruct((MOD_ROWS, n), F32),
        compiler_params=_params("arbitrary"),
        name="ada_mod",
    )(cvec, ada_w, b_early)


def _token_rows(x_lat, x_ctx, rows, tm):
    d = x_lat.shape[1]
    n_lat = min(rows, x_lat.shape[0]) // tm
    if x_ctx is None:
        x_ctx = x_lat
        assert rows <= x_lat.shape[0]
    else:
        assert x_lat.shape[0] % tm == 0 and rows == x_lat.shape[0] + x_ctx.shape[0]
    specs = [pl.BlockSpec((tm, d), lambda i: (jnp.minimum(i, n_lat - 1), 0)),
             pl.BlockSpec((tm, d), lambda i: (jnp.maximum(i - n_lat, 0), 0))]
    return [x_lat, x_ctx], specs, n_lat


def _read_token_rows(xl_ref, xc_ref, n_lat):
    return jnp.where(pl.program_id(0) < n_lat, xl_ref[...], xc_ref[...])


def _prenorm_kernel(xl_ref, xc_ref, g_ref, mod_ref, h_ref, *, n_lat, g_row, sh_row, sc_row):
    y = _rms(_read_token_rows(xl_ref, xc_ref, n_lat), g_ref[g_row:g_row + 1, :])
    h = y * (1.0 + mod_ref[sc_row:sc_row + 1, :]) + mod_ref[sh_row:sh_row + 1, :]
    h_ref[...] = h.astype(h_ref.dtype)


def _prenorm(x_lat, x_ctx, g, mod, *, seg_tiles, n_batch, g_row, sh_row, sc_row, tm=512):
    d = x_lat.shape[1]
    m = x_lat.shape[0] + x_ctx.shape[0]
    seg = _seg_index(seg_tiles(tm), n_batch)
    x_args, x_specs, n_lat = _token_rows(x_lat, x_ctx, m, tm)
    return pl.pallas_call(
        functools.partial(_prenorm_kernel, n_lat=n_lat, g_row=g_row, sh_row=sh_row, sc_row=sc_row),
        grid=(m // tm,),
        in_specs=x_specs + [
            pl.BlockSpec(g.shape, lambda i: (0, 0)),
            pl.BlockSpec((None, mod.shape[1], d), lambda i: (seg(i), 0, 0)),
        ],
        out_specs=pl.BlockSpec((tm, d), lambda i: (i, 0)),
        out_shape=jax.ShapeDtypeStruct((m, d), BF16),
        compiler_params=_params("arbitrary"),
        name="prenorm",
    )(*x_args, g, mod)


def _resid_kernel(y_ref, xl_ref, xc_ref, g_ref, mod_ref, *rest, n_lat, gt_row, gpost_row, nxt):
    x = _read_token_rows(xl_ref, xc_ref, n_lat)
    xn = x + mod_ref[gt_row:gt_row + 1, :] * _rms(y_ref[...], g_ref[gpost_row:gpost_row + 1, :])
    if nxt is None:
        (xo_ref,) = rest
        xo_ref[...] = xn
        return
    g2_ref, mod2_ref, xo_ref, h_ref = rest
    gpre_row, sh_row, sc_row = nxt
    xo_ref[...] = xn
    h = _rms(xn, g2_ref[gpre_row:gpre_row + 1, :])
    h = h * (1.0 + mod2_ref[sc_row:sc_row + 1, :]) + mod2_ref[sh_row:sh_row + 1, :]
    h_ref[...] = h.astype(h_ref.dtype)


def _resid(y, x, g, mod, *, rows, seg_tiles, n_batch, gt_row, gpost_row, x_ctx=None, nxt=None, g2=None, mod2=None,
           tm=256):
    d = x.shape[1]
    seg = _seg_index(seg_tiles(tm), n_batch)
    row_spec = pl.BlockSpec((tm, d), lambda i: (i, 0))
    mod_spec = pl.BlockSpec((None, N_MOD, d), lambda i: (seg(i), 0, 0))
    x_args, x_specs, n_lat = _token_rows(x, x_ctx, rows, tm)
    in_specs = [row_spec] + x_specs + [pl.BlockSpec(g.shape, lambda i: (0, 0)), mod_spec]
    args = [y] + x_args + [g, mod]
    out_specs = [row_spec]
    out_shape = [jax.ShapeDtypeStruct((rows, d), F32)]
    if nxt is not None:
        in_specs += [pl.BlockSpec(g2.shape, lambda i: (0, 0)), mod_spec]
        args += [g2, mod2]
        out_specs.append(row_spec)
        out_shape.append(jax.ShapeDtypeStruct((rows, d), BF16))
    out = pl.pallas_call(
        functools.partial(_resid_kernel, n_lat=n_lat, gt_row=gt_row, gpost_row=gpost_row, nxt=nxt),
        grid=(rows // tm,),
        in_specs=in_specs,
        out_specs=out_specs,
        out_shape=out_shape,
        compiler_params=_params("arbitrary"),
        name="resid_norm",
    )(*args)
    return out if nxt is not None else out[0]


SIDE_CAST_ROWS = 32


def _mm_kernel(x_ref, w_ref, *rest, scale_blocks, scale, n_side):
    side_in, o_ref, side_out = rest[:n_side], rest[n_side], rest[n_side + 1:]
    acc = jnp.dot(x_ref[...], w_ref[...].astype(BF16), preferred_element_type=F32)
    if scale_blocks:
        acc = acc * jnp.where(pl.program_id(1) < scale_blocks, scale, 1.0)
    o_ref[...] = acc.astype(o_ref.dtype)
    for src, dst in zip(side_in, side_out):
        dst[...] = src[...].astype(dst.dtype)


def _matmul(x, w, *, rows, n_out, out_dtype, tm, tn, col_blk_off=0, scale_blocks=0, scale=1.0, side_casts=(),
            name="matmul"):
    k = x.shape[1]
    n_j = n_out // tn
    n_steps = (rows // tm) * n_j
    in_specs = [
        pl.BlockSpec((tm, k), lambda i, j: (i, 0)),
        pl.BlockSpec((k, tn), lambda i, j: (0, j + col_blk_off)),
    ]
    out_specs = [pl.BlockSpec((tm, tn), lambda i, j: (i, j))]
    out_shape = [jax.ShapeDtypeStruct((rows, n_out), out_dtype)]
    for ws in side_casts:
        n_slabs = ws.shape[0] // SIDE_CAST_ROWS
        assert n_slabs * SIDE_CAST_ROWS == ws.shape[0] and n_slabs <= n_steps
        spec = pl.BlockSpec((SIDE_CAST_ROWS, ws.shape[1]),
                            lambda i, j, n_slabs=n_slabs: (jnp.minimum(i * n_j + j, n_slabs - 1), 0))
        in_specs.append(spec)
        out_specs.append(spec)
        out_shape.append(jax.ShapeDtypeStruct(ws.shape, BF16))
    out = pl.pallas_call(
        functools.partial(_mm_kernel, scale_blocks=scale_blocks, scale=scale, n_side=len(side_casts)),
        grid=(rows // tm, n_j),
        in_specs=in_specs,
        out_specs=out_specs,
        out_shape=out_shape,
        compiler_params=_params("arbitrary", "arbitrary"),
        name=name,
    )(x, w, *side_casts)
    return (out[0], out[1:]) if side_casts else out[0]


def _mm_t_kernel(x_ref, w_ref, o_ref):
    acc = jnp.dot(x_ref[...], w_ref[...].astype(BF16), preferred_element_type=F32)
    o_ref[...] = acc.T.astype(o_ref.dtype)


def _matmul_t(x, w, *, rows, n_out, tm, tn, col_blk_off=0, name="matmul_t"):
    k = x.shape[1]
    return pl.pallas_call(
        _mm_t_kernel,
        grid=(n_out // tn, rows // tm),
        in_specs=[
            pl.BlockSpec((tm, k), lambda j, i: (i, 0)),
            pl.BlockSpec((k, tn), lambda j, i: (0, j + col_blk_off)),
        ],
        out_specs=pl.BlockSpec((None, tn, tm), lambda j, i: (i, j, 0)),
        out_shape=jax.ShapeDtypeStruct((rows // tm, n_out, tm), BF16),
        compiler_params=_params("arbitrary", "arbitrary"),
        name=name,
    )(x, w)


def _swap_halves(y):
    lane = lax.broadcasted_iota(jnp.int32, y.shape, 1)
    return jnp.where((lane & 32) == 0, pltpu.roll(y, 96, 1), pltpu.roll(y, 32, 1))


def _mm_rope_kernel(x_ref, w_ref, g_ref, cos_ref, sin_ref, o_ref, *, scale):
    cos = cos_ref[...]
    sin = sin_ref[...]
    g = g_ref[...]
    x = x_ref[...]
    for grp in range(w_ref.shape[1] // MXU_COLS):
        w = w_ref[:, grp * MXU_COLS:(grp + 1) * MXU_COLS].astype(BF16)
        acc = jnp.dot(x, w, preferred_element_type=F32)
        for hh in range(MXU_COLS // HEAD_DIM):
            y = _rms(acc[:, hh * HEAD_DIM:(hh + 1) * HEAD_DIM], g)
            y = y * cos + _swap_halves(y) * sin
            if scale != 1.0:
                y = y * scale
            col0 = grp * MXU_COLS + hh * HEAD_DIM
            o_ref[:, col0:col0 + HEAD_DIM] = y.astype(o_ref.dtype)


def _matmul_rope(x, w, g, cos, sin, *, rows, n_out, tm, tn, scale, name):
    k = x.shape[1]
    tab_spec = pl.BlockSpec((tm, HEAD_DIM), lambda i, j: (i, 0))
    return pl.pallas_call(
        functools.partial(_mm_rope_kernel, scale=scale),
        grid=(rows // tm, n_out // tn),
        in_specs=[
            pl.BlockSpec((tm, k), lambda i, j: (i, 0)),
            pl.BlockSpec((k, tn), lambda i, j: (0, j)),
            pl.BlockSpec((1, HEAD_DIM), lambda i, j: (0, 0)),
            tab_spec,
            tab_spec,
        ],
        out_specs=pl.BlockSpec((tm, tn), lambda i, j: (i, j)),
        out_shape=jax.ShapeDtypeStruct((rows, n_out), BF16),
        compiler_params=_params("arbitrary", "arbitrary"),
        name=name,
    )(x, w, g.reshape(1, HEAD_DIM), cos, sin)


def _mm_rope_t_kernel(x_ref, w_ref, cos_ref, sin_ref, o_ref):
    cos = cos_ref[...]
    sin = sin_ref[...]
    x = x_ref[...]
    q4 = HEAD_DIM // 4
    for grp in range(w_ref.shape[1] // MXU_COLS):
        w = w_ref[:, grp * MXU_COLS:(grp + 1) * MXU_COLS].astype(BF16)
        acc_t = jnp.dot(x, w, preferred_element_type=F32).T
        for hh in range(MXU_COLS // HEAD_DIM):
            y = acc_t[hh * HEAD_DIM:(hh + 1) * HEAD_DIM, :]
            r = lax.rsqrt(jnp.mean(y * y, axis=0, keepdims=True) + NORM_EPS)
            partner = jnp.concatenate([y[q4:2 * q4], y[:q4], y[3 * q4:], y[2 * q4:3 * q4]], axis=0)
            row0 = grp * MXU_COLS + hh * HEAD_DIM
            o_ref[row0:row0 + HEAD_DIM, :] = ((y * cos + partner * sin) * r).astype(o_ref.dtype)


def _matmul_rope_t(x, w, cos_t, sin_t, *, rows, n_out, tn, name):
    k = x.shape[1]
    n_tiles, _, tm = cos_t.shape
    assert n_tiles * tm == rows
    tab_spec = pl.BlockSpec((None, HEAD_DIM, tm), lambda i, j: (i, 0, 0))
    return pl.pallas_call(
        _mm_rope_t_kernel,
        grid=(n_tiles, n_out // tn),
        in_specs=[
            pl.BlockSpec((tm, k), lambda i, j: (i, 0)),
            pl.BlockSpec((k, tn), lambda i, j: (0, j)),
            tab_spec,
            tab_spec,
        ],
        out_specs=pl.BlockSpec((None, tn, tm), lambda i, j: (i, j, 0)),
        out_shape=jax.ShapeDtypeStruct((n_tiles, n_out, tm), BF16),
        compiler_params=_params("arbitrary", "arbitrary"),
        name=name,
    )(x, w, cos_t, sin_t)


def _gateup_kernel(x_ref, w1_ref, w3_ref, w2_ref, o_ref, w2b_ref):
    x = x_ref[...]
    a = jnp.dot(x, w1_ref[...].astype(BF16), preferred_element_type=F32)
    b = jnp.dot(x, w3_ref[...].astype(BF16), preferred_element_type=F32)
    o_ref[...] = (a * jax.nn.sigmoid(a) * b).astype(o_ref.dtype)
    w2b_ref[...] = w2_ref[...].astype(w2b_ref.dtype)


def _gateup(x, w13, w2, layer, *, rows, tm, tn):
    k = x.shape[1]
    f = w13.shape[2] // 2
    n_j = f // tn
    steps = (rows // tm) * n_j
    slab = w2.shape[1] // steps
    assert slab * steps == w2.shape[1] and slab % 16 == 0
    return pl.pallas_call(
        _gateup_kernel,
        grid=(rows // tm, n_j),
        in_specs=[
            pl.BlockSpec((tm, k), lambda i, j: (i, 0), pipeline_mode=pl.Buffered(1)),
            pl.BlockSpec((None, k, tn), lambda i, j: (layer, 0, j)),
            pl.BlockSpec((None, k, tn), lambda i, j: (layer, 0, j + n_j)),
            pl.BlockSpec((None, slab, w2.shape[2]), lambda i, j: (layer, i * n_j + j, 0)),
        ],
        out_specs=[
            pl.BlockSpec((tm, tn), lambda i, j: (i, j)),
            pl.BlockSpec((slab, w2.shape[2]), lambda i, j: (i * n_j + j, 0)),
        ],
        out_shape=[
            jax.ShapeDtypeStruct((rows, f), BF16),
            jax.ShapeDtypeStruct(w2.shape[1:], BF16),
        ],
        compiler_params=_params("arbitrary", "arbitrary"),
        name="ffn_gateup",
    )(x, w13, w13, w2)


_NT = (((1,), (1,)), ((), ()))


def _na_block_rows(rb):
    return rb + NA_WIN_H


N_DR = 2 * NA_WIN_H - 1
N_DC = 2 * NA_WIN_W - 1
_TAB_BOTH, _TAB_FIRST, _TAB_SECOND, _TAB_NONE = 0, N_DR + 1, 2 * N_DR + 1, 3 * N_DR + 1
_TAB_SIZE = 3 * N_DR + 2


def _na_build_bias_tiles(rpb_ref, tab_scr, h0, hb):
    shape = (GRID_W, 2 * GRID_W)
    c = lax.broadcasted_iota(jnp.int32, shape, 0)
    lane = lax.broadcasted_iota(jnp.int32, shape, 1)
    kc = lane & (GRID_W - 1)
    second = lane >= GRID_W
    cs = jnp.clip(c - NA_WIN_W // 2, 0, GRID_W - NA_WIN_W)
    in_win = (kc >= cs) & (kc < cs + NA_WIN_W)
    dci = kc - c + (NA_WIN_W - 1)
    is_dc = [dci == k for k in range(N_DC)]
    neg = jnp.full(shape, NEG_INF, F32)
    for hh in range(hb):
        base = (h0 + hh) * (N_DR * N_DC)
        rows = []
        for d in range(N_DR):
            t = neg
            for k in range(N_DC):
                t = jnp.where(is_dc[k], rpb_ref[base + d * N_DC + k] * LOG2E, t)
            rows.append(jnp.where(in_win, t, NEG_INF))
        for d in range(N_DR + 1):
            lo = rows[d - 1] if d >= 1 else neg
            hi = rows[d] if d < N_DR else neg
            tab_scr[hh, _TAB_BOTH + d] = jnp.where(second, hi, lo)
        for d in range(N_DR):
            tab_scr[hh, _TAB_FIRST + d] = jnp.where(second, neg, rows[d])
            tab_scr[hh, _TAB_SECOND + d] = jnp.where(second, rows[d], neg)
        tab_scr[hh, _TAB_NONE] = neg


def _na_kernel(rpb_ref, q_ref, k_ref, v_ref, kc_ref, vc_ref, c_ref, aw_ref, ab_ref, o_ref, mod_ref, tab_scr, *,
               rb, hb, nsb, rows, n_ada):
    kr = _na_block_rows(rb)
    qn = rb * GRID_W
    half = NA_WIN_H // 2
    step = (pl.program_id(0) * pl.num_programs(1) + pl.program_id(1)) * pl.num_programs(2) + pl.program_id(2)

    @pl.when(step < n_ada)
    def _():
        _ada_slab(c_ref, aw_ref, ab_ref, mod_ref)

    @pl.when((pl.program_id(1) == 0) & (pl.program_id(2) == 0))
    def _():
        _na_build_bias_tiles(rpb_ref, tab_scr, pl.program_id(0) * hb, hb)

    starts, tiles = [], []
    for sb in range(nsb):
        r0 = (pl.program_id(2) * nsb + sb) * rb
        kstart = jnp.clip(r0 - half, 0, rows - kr)
        starts.append(pl.multiple_of(kstart * GRID_W, GRID_W))
        tile_idx = []
        for i in range(rb):
            r = r0 + i
            rs = jnp.clip(r - half, 0, rows - NA_WIN_H)
            row_idx = []
            for jp in range(kr // 2):
                k0 = kstart + 2 * jp
                v0 = (k0 >= rs) & (k0 < rs + NA_WIN_H)
                v1 = (k0 + 1 >= rs) & (k0 + 1 < rs + NA_WIN_H)
                d0 = k0 - r + NA_WIN_H - 1
                idx = jnp.where(v0 & v1, _TAB_BOTH + d0 + 1,
                                jnp.where(v0, _TAB_FIRST + d0, jnp.where(v1, _TAB_SECOND + d0 + 1, _TAB_NONE)))
                row_idx.append(jnp.clip(idx, 0, _TAB_SIZE - 1))
            tile_idx.append(row_idx)
        tiles.append(tile_idx)

    def scores(sb, hh):
        cols = slice(hh * HEAD_DIM, (hh + 1) * HEAD_DIM)
        q = q_ref[sb * qn:(sb + 1) * qn, cols]
        ku = k_ref[pl.ds(starts[sb], kr * GRID_W), cols]
        bias = jnp.concatenate(
            [jnp.concatenate([tab_scr[hh, idx] for idx in row_idx], axis=1) for row_idx in tiles[sb]], axis=0)
        s_loc = lax.dot_general(q, ku, _NT, preferred_element_type=F32) + bias
        s_ctx = lax.dot_general(q, kc_ref[:, cols], _NT, preferred_element_type=F32)
        return s_loc, s_ctx

    def finish(sb, hh, s_loc, s_ctx):
        cols = slice(hh * HEAD_DIM, (hh + 1) * HEAD_DIM)
        vu = v_ref[pl.ds(starts[sb], kr * GRID_W), cols]
        m = jnp.maximum(jnp.max(s_loc, axis=-1, keepdims=True), jnp.max(s_ctx, axis=-1, keepdims=True))
        p_loc = jnp.exp2(s_loc - m)
        p_ctx = jnp.exp2(s_ctx - m)
        l = jnp.sum(p_loc, axis=-1, keepdims=True) + jnp.sum(p_ctx, axis=-1, keepdims=True)
        o = (jnp.dot(p_loc.astype(BF16), vu, preferred_element_type=F32)
             + jnp.dot(p_ctx.astype(BF16), vc_ref[:, cols], preferred_element_type=F32))
        o_ref[sb * qn:(sb + 1) * qn, cols] = (o / l).astype(o_ref.dtype)

    pairs = [(sb, hh) for sb in range(nsb) for hh in range(hb)]
    ahead = 1
    pending = [scores(*pair) for pair in pairs[:ahead]]
    for n, pair in enumerate(pairs):
        if n + ahead < len(pairs):
            pending.append(scores(*pairs[n + ahead]))
        finish(*pair, *pending.pop(0))


def _na_attention(qkv, rpb, cvec, ada_w, b_late, *, n_batch, seq, ctx_len, n_heads, rb=4, hb=2, nsb=8):
    m_all = qkv.shape[0]
    rows = seq // GRID_W
    assert rpb.shape == (n_heads, N_DR, N_DC) and _na_block_rows(rb) % 2 == 0 and rows % (rb * nsb) == 0
    hw = hb * HEAD_DIM
    hblocks = n_heads // hb
    qrows = nsb * rb * GRID_W
    n_r = rows // (rb * nsb)
    depth, d, n_mod = ada_w.shape
    n_late = b_late.shape[1] // ADA_SLAB
    n_early = depth * n_mod // ADA_SLAB - n_late
    assert n_late <= hblocks * n_batch * n_r
    step_of = lambda h, b, r, _: jnp.minimum((h * n_batch + b) * n_r + r, n_late - 1)
    ada_in_specs, ada_out_spec = _ada_slab_specs(d, n_early, n_mod // ADA_SLAB, step_of)
    lat_spec = lambda part: pl.BlockSpec((seq, hw), lambda h, b, r, _: (b, part * hblocks + h))
    ctx_spec = lambda part: pl.BlockSpec((ctx_len, hw),
                                         lambda h, b, r, _: (n_batch * seq // ctx_len + b, part * hblocks + h))
    q_spec = pl.BlockSpec((qrows, hw), lambda h, b, r, _: (b * (seq // qrows) + r, h))
    return pl.pallas_call(
        functools.partial(_na_kernel, rb=rb, hb=hb, nsb=nsb, rows=rows, n_ada=n_late),
        grid_spec=pltpu.PrefetchScalarGridSpec(
            num_scalar_prefetch=1,
            grid=(hblocks, n_batch, n_r),
            in_specs=[q_spec, lat_spec(1), lat_spec(2), ctx_spec(1), ctx_spec(2)] + ada_in_specs,
            out_specs=[q_spec, ada_out_spec],
            scratch_shapes=[pltpu.VMEM((hb, _TAB_SIZE, GRID_W, 2 * GRID_W), F32)],
        ),
        out_shape=[jax.ShapeDtypeStruct((m_all, n_heads * HEAD_DIM), BF16),
                   jax.ShapeDtypeStruct((MOD_ROWS, b_late.shape[1]), F32)],
        compiler_params=_params("arbitrary", "arbitrary", "arbitrary"),
        name="na_attention",
    )(rpb.reshape(-1), qkv, qkv, qkv, qkv, qkv, cvec, ada_w, b_late)


def _ctx_attn_kernel(q_ref, k_ref, v_ref, o_in_ref, o_ref, *, hb):
    del o_in_ref
    for hh in range(hb):
        cols = slice(hh * HEAD_DIM, (hh + 1) * HEAD_DIM)
        s = lax.dot_general(q_ref[:, cols], k_ref[:, cols], _NT, preferred_element_type=F32)
        p = jnp.exp2(s - jnp.max(s, axis=-1, keepdims=True))
        l = jnp.sum(p, axis=-1, keepdims=True)
        o = jnp.dot(p.astype(BF16), v_ref[:, cols], preferred_element_type=F32)
        o_ref[:, cols] = (o / l).astype(o_ref.dtype)


def _ctx_attention(qkv, o, *, n_batch, seq, ctx_len, n_heads, hb=8):
    hw = hb * HEAD_DIM
    hblocks = n_heads // hb
    row0 = n_batch * seq // ctx_len
    spec = lambda part: pl.BlockSpec((ctx_len, hw), lambda b, h: (row0 + b, part * hblocks + h))
    return pl.pallas_call(
        functools.partial(_ctx_attn_kernel, hb=hb),
        grid=(n_batch, hblocks),
        in_specs=[spec(0), spec(1), spec(2), pl.BlockSpec(memory_space=pl.ANY)],
        out_specs=spec(0),
        out_shape=jax.ShapeDtypeStruct(o.shape, o.dtype),
        input_output_aliases={3: 0},
        compiler_params=_params("arbitrary", "arbitrary"),
        name="ctx_attention",
    )(qkv, qkv, qkv, o)


def _gqa_kernel(q_ref, k_ref, vt_ref, kc_ref, vtc_ref, o_ref, m_scr, l_scr, acc_scr, s0_scr, s1_scr):
    n_chunks, _, tk = vt_ref.shape
    m_scr[...] = jnp.full(m_scr.shape, NEG_INF, F32)
    l_scr[...] = jnp.zeros(l_scr.shape, F32)
    acc_scr[...] = jnp.zeros(acc_scr.shape, F32)

    def scores(k, g):
        return jnp.dot(k, q_ref[g * HEAD_DIM:(g + 1) * HEAD_DIM, :], preferred_element_type=F32)

    def accumulate(s, vt, g):
        m_old = m_scr[g]
        m_new = jnp.maximum(m_old, jnp.max(s, axis=0, keepdims=True))
        alpha = jnp.exp2(m_old - m_new)
        p = jnp.exp2(s - m_new)
        l_scr[g] = alpha * l_scr[g] + jnp.sum(p, axis=0, keepdims=True)
        acc_scr[g] = alpha * acc_scr[g] + jnp.dot(vt, p.astype(BF16), preferred_element_type=F32)
        m_scr[g] = m_new

    def k_chunk(c):
        return k_ref[pl.ds(pl.multiple_of(c * tk, tk), tk), :]

    def stage(cur_scr, nxt_scr, c):
        k_next = k_chunk(c + 1)
        vt = vt_ref[c]
        for g in range(GQA_GROUP):
            nxt_scr[g] = scores(k_next, g)
            accumulate(cur_scr[g], vt, g)

    for g in range(GQA_GROUP):
        s0_scr[g] = scores(k_chunk(0), g)

    def body(j, carry):
        stage(s0_scr, s1_scr, 2 * j)
        stage(s1_scr, s0_scr, 2 * j + 1)
        return carry

    lax.fori_loop(0, n_chunks // 2 - 1, body, 0)
    stage(s0_scr, s1_scr, n_chunks - 2)
    vt_last = vt_ref[n_chunks - 1]
    for g in range(GQA_GROUP):
        s_ctx = scores(kc_ref[...], g)
        accumulate(s1_scr[g], vt_last, g)
        accumulate(s_ctx, vtc_ref[...], g)
    for g in range(GQA_GROUP):
        o_ref[:, g * HEAD_DIM:(g + 1) * HEAD_DIM] = (acc_scr[g] / l_scr[g]).T.astype(o_ref.dtype)


def _gqa_attention(qt, k, vt, *, n_batch, seq, ctx_len):
    n_kv = k.shape[1] // HEAD_DIM
    tq = qt.shape[2]
    tk = vt.shape[2]
    assert (seq // tk) % 2 == 0 and seq % tq == 0
    gw = GQA_GROUP * HEAD_DIM
    n_qt = seq // tq
    q_spec = pl.BlockSpec((tq, gw), lambda b, h, i: (b * n_qt + i, h))
    return pl.pallas_call(
        _gqa_kernel,
        grid=(n_batch, n_kv, n_qt),
        in_specs=[
            pl.BlockSpec((None, gw, tq), lambda b, h, i: (b * n_qt + i, h, 0)),
            pl.BlockSpec((seq, HEAD_DIM), lambda b, h, i: (b, h)),
            pl.BlockSpec((seq // tk, HEAD_DIM, tk), lambda b, h, i: (b, h, 0)),
            pl.BlockSpec((ctx_len, HEAD_DIM), lambda b, h, i: (n_batch * seq // ctx_len + b, h)),
            pl.BlockSpec((None, HEAD_DIM, ctx_len), lambda b, h, i: (n_batch * seq // tk, h, b)),
        ],
        out_specs=q_spec,
        out_shape=jax.ShapeDtypeStruct((n_batch * seq, qt.shape[1]), BF16),
        scratch_shapes=[
            pltpu.VMEM((GQA_GROUP, 1, tq), F32),
            pltpu.VMEM((GQA_GROUP, 1, tq), F32),
            pltpu.VMEM((GQA_GROUP, HEAD_DIM, tq), F32),
            pltpu.VMEM((GQA_GROUP, tk, tq), F32),
            pltpu.VMEM((GQA_GROUP, tk, tq), F32),
        ],
        compiler_params=_params("arbitrary", "arbitrary", "arbitrary"),
        name="gqa_attention",
    )(qt, k, vt, k, vt)


def _rope_partner(v):
    q4 = HEAD_DIM // 4
    return jnp.concatenate([v[..., q4:2 * q4], v[..., :q4], v[..., 3 * q4:], v[..., 2 * q4:3 * q4]], axis=-1)


def _rope_tables_t(cos, sin, g, scale, rows, tm):
    cos_t = (cos[:rows] * (g * scale)[None, :]).reshape(rows // tm, tm, HEAD_DIM)
    sin_t = (sin[:rows] * (_rope_partner(g) * scale)[None, :]).reshape(rows // tm, tm, HEAD_DIM)
    return jnp.transpose(cos_t, (0, 2, 1)), jnp.transpose(sin_t, (0, 2, 1))


def _rope_tables(n_batch, seq, ctx_len):
    quarter = HEAD_DIM // 4
    t = jnp.arange(seq)
    freqs = ROPE_THETA ** (-jnp.arange(quarter, dtype=F32) / quarter)
    ang_r = (t // GRID_W).astype(F32)[:, None] * freqs[None, :]
    ang_c = (t % GRID_W).astype(F32)[:, None] * freqs[None, :]
    cos = jnp.concatenate([jnp.cos(ang_r)] * 2 + [jnp.cos(ang_c)] * 2, axis=-1)
    sin = jnp.concatenate([-jnp.sin(ang_r), jnp.sin(ang_r), -jnp.sin(ang_c), jnp.sin(ang_c)], axis=-1)
    n_ctx = n_batch * ctx_len
    cos = jnp.concatenate([jnp.tile(cos, (n_batch, 1)), jnp.ones((n_ctx, HEAD_DIM), F32)], axis=0)
    sin = jnp.concatenate([jnp.tile(sin, (n_batch, 1)), jnp.zeros((n_ctx, HEAD_DIM), F32)], axis=0)
    return cos, sin


def kernel(x, c, ctx, c_ctx, ada_w, ada_b, norm_g, na_wqkv, na_wo, na_rpb, gqa_wq, gqa_wkv, gqa_q_norm,
           gqa_k_norm, gqa_wo, ffn_w13, ffn_w2):
    n_batch, seq, d = x.shape
    ctx_len = ctx.shape[1]
    depth = ada_w.shape[0]
    assert depth == 2 and na_wqkv.shape[0] == 1 and gqa_wq.shape[0] == 1
    assert seq % GRID_W == 0 and n_batch + 1 <= MOD_ROWS
    n_heads = d // HEAD_DIM
    m_lat = n_batch * seq
    m_all = m_lat + n_batch * ctx_len
    scale = HEAD_DIM ** -0.5 * LOG2E
    tm_all = m_all // 8
    tm_lat = m_lat // 8
    tm_down = 2
    seg_tiles = lambda t: seq // t
    assert tm_all % 16 == 0 and tm_lat % 16 == 0

    cvec = jnp.zeros((MOD_ROWS, d), F32).at[:n_batch].set(c).at[n_batch].set(c_ctx)
    n_mod_early = 2 * d
    b_flat = ada_b.reshape(1, depth * N_MOD * d)
    mod_early = _ada_early(cvec, ada_w, b_flat[:, :n_mod_early])
    cos, sin = _rope_tables(n_batch, seq, ctx_len)
    cos_qt, sin_qt = _rope_tables_t(cos, sin, gqa_q_norm[0], scale, m_lat, tm_lat)

    tn_f32, tn_b16 = 512, 1024

    x_lat = x.reshape(m_lat, d)
    x_ctx = ctx.reshape(n_batch * ctx_len, d)
    seg_kw = dict(seg_tiles=seg_tiles, n_batch=n_batch)

    h = _prenorm(x_lat, x_ctx, norm_g[0], mod_early.reshape(MOD_ROWS, 2, d), g_row=0, sh_row=0, sc_row=1, **seg_kw)
    qkv, (na_wo_b, wq, wkv, gqa_wo_b) = _matmul(
        h, na_wqkv[0], rows=m_all, n_out=na_wqkv.shape[2], out_dtype=BF16, tm=tm_all, tn=tn_f32,
        scale_blocks=d // tn_f32, scale=scale, side_casts=(na_wo[0], gqa_wq[0], gqa_wkv[0], gqa_wo[0]), name="na_qkv")
    o, mod_late = _na_attention(qkv, na_rpb[0], cvec, ada_w, b_flat[:, n_mod_early:],
                                n_batch=n_batch, seq=seq, ctx_len=ctx_len, n_heads=n_heads)
    mod = jnp.concatenate([mod_early, mod_late], axis=1).reshape(MOD_ROWS, depth, N_MOD, d)
    mod = [mod[:, i] for i in range(depth)]
    o = _ctx_attention(qkv, o, n_batch=n_batch, seq=seq, ctx_len=ctx_len, n_heads=n_heads)
    y = _matmul(o, na_wo_b, rows=m_all, n_out=d, out_dtype=F32, tm=tm_all, tn=tn_b16, name="na_wo")
    xa, h = _resid(y, x_lat, norm_g[0], mod[0], x_ctx=x_ctx, rows=m_all, gt_row=2, gpost_row=1,
                   nxt=(2, 3, 4), g2=norm_g[0], mod2=mod[0], **seg_kw)
    gu, w2 = _gateup(h, ffn_w13, ffn_w2, 0, rows=m_all, tm=2 * tm_all, tn=256)
    y = _matmul(gu, w2, rows=m_all, n_out=d, out_dtype=F32, tm=tm_all // tm_down, tn=512, name="ffn_down")
    xa, h = _resid(y, xa, norm_g[0], mod[0], rows=m_all, gt_row=5, gpost_row=3,
                   nxt=(0, 0, 1), g2=norm_g[1], mod2=mod[1], **seg_kw)

    kv_w = wkv.shape[1] // 2
    qt = _matmul_rope_t(h, wq, cos_qt, sin_qt, rows=m_lat, n_out=d, tn=tn_b16, name="gqa_q")
    k = _matmul_rope(h, wkv, gqa_k_norm[0], cos, sin, rows=m_all, n_out=kv_w, tm=tm_all, tn=kv_w, scale=1.0, name="gqa_k")
    gqa_tk = 512
    assert seq % gqa_tk == 0 and n_batch * ctx_len == gqa_tk
    vt = _matmul_t(h, wkv, rows=m_all, n_out=kv_w, tm=gqa_tk, tn=kv_w, col_blk_off=1, name="gqa_v")
    o = _gqa_attention(qt, k, vt, n_batch=n_batch, seq=seq, ctx_len=ctx_len)
    y = _matmul(o, gqa_wo_b, rows=m_lat, n_out=d, out_dtype=F32, tm=tm_lat, tn=tn_b16, name="gqa_wo")
    xl, h = _resid(y, xa, norm_g[1], mod[1], rows=m_lat, gt_row=2, gpost_row=1,
                   nxt=(2, 3, 4), g2=norm_g[1], mod2=mod[1], **seg_kw)
    gu, w2 = _gateup(h, ffn_w13, ffn_w2, 1, rows=m_lat, tm=2 * tm_lat, tn=256)
    y = _matmul(gu, w2, rows=m_lat, n_out=d, out_dtype=F32, tm=tm_lat // tm_down, tn=512, name="ffn_down")
    xl = _resid(y, xl, norm_g[1], mod[1], rows=m_lat, gt_row=5, gpost_row=3, **seg_kw)
    return xl.reshape(n_batch, seq, d)
```

```python
import functools

import jax
import jax.numpy as jnp
from jax import lax
from jax.experimental import pallas as pl
from jax.experimental.pallas import tpu as pltpu

GRID_W = 64
NA_WIN_H = 8
NA_WIN_W = 16
HEAD_DIM = 128
GQA_GROUP = 4
ROPE_THETA = 10000.0
NORM_EPS = 1e-6
NEG_INF = -1e30
LOG2E = 1.4426950408889634
N_MOD = 6

VMEM_LIMIT_BYTES = 56 * 1024 * 1024
MXU_COLS = 256
MOD_ROWS = 8

F32 = jnp.float32
BF16 = jnp.bfloat16


def _params(*sem):
    return pltpu.CompilerParams(dimension_semantics=sem, vmem_limit_bytes=VMEM_LIMIT_BYTES)


def _rms(x, g):
    ms = jnp.mean(x * x, axis=-1, keepdims=True)
    return x * lax.rsqrt(ms + NORM_EPS) * g


def _seg_index(rows_per_seg_tiles, n_batch):
    return lambda i: jnp.minimum(i // rows_per_seg_tiles, n_batch)


ADA_SLAB = 1024


def _ada_slab(c_ref, w_ref, b_ref, o_ref):
    c = c_ref[...]
    s = (c * jax.nn.sigmoid(c)).astype(BF16)
    o_ref[...] = jnp.dot(s, w_ref[...].astype(BF16), preferred_element_type=F32) + b_ref[...]


def _ada_slab_specs(d, n_early, per_layer, step_of):
    def w_map(*idx):
        f = step_of(*idx) + n_early
        return f // per_layer, 0, f % per_layer
    flat = lambda *idx: (0, step_of(*idx))
    return [pl.BlockSpec((MOD_ROWS, d), lambda *idx: (0, 0)),
            pl.BlockSpec((None, d, ADA_SLAB), w_map),
            pl.BlockSpec((1, ADA_SLAB), flat)], pl.BlockSpec((MOD_ROWS, ADA_SLAB), flat)


def _ada_early(cvec, ada_w, b_early):
    d = ada_w.shape[1]
    n = b_early.shape[1]
    in_specs, out_spec = _ada_slab_specs(d, 0, ada_w.shape[2] // ADA_SLAB, lambda j: j)
    return pl.pallas_call(
        _ada_slab,
        grid=(n // ADA_SLAB,),
        in_specs=in_specs,
        out_specs=out_spec,
        out_shape=jax.ShapeDtypeStruct((MOD_ROWS, n), F32),
        compiler_params=_params("arbitrary"),
        name="ada_mod",
    )(cvec, ada_w, b_early)


def _token_rows(x_lat, x_ctx, rows, tm):
    d = x_lat.shape[1]
    n_lat = min(rows, x_lat.shape[0]) // tm
    if x_ctx is None:
        x_ctx = x_lat
        assert rows <= x_lat.shape[0]
    else:
        assert x_lat.shape[0] % tm == 0 and rows == x_lat.shape[0] + x_ctx.shape[0]
    specs = [pl.BlockSpec((tm, d), lambda i: (jnp.minimum(i, n_lat - 1), 0)),
             pl.BlockSpec((tm, d), lambda i: (jnp.maximum(i - n_lat, 0), 0))]
    return [x_lat, x_ctx], specs, n_lat


def _read_token_rows(xl_ref, xc_ref, n_lat):
    return jnp.where(pl.program_id(0) < n_lat, xl_ref[...], xc_ref[...])


def _prenorm_kernel(xl_ref, xc_ref, g_ref, mod_ref, h_ref, *, n_lat, g_row, sh_row, sc_row):
    y = _rms(_read_token_rows(xl_ref, xc_ref, n_lat), g_ref[g_row:g_row + 1, :])
    h = y * (1.0 + mod_ref[sc_row:sc_row + 1, :]) + mod_ref[sh_row:sh_row + 1, :]
    h_ref[...] = h.astype(h_ref.dtype)


def _prenorm(x_lat, x_ctx, g, mod, *, seg_tiles, n_batch, g_row, sh_row, sc_row, tm=512):
    d = x_lat.shape[1]
    m = x_lat.shape[0] + x_ctx.shape[0]
    seg = _seg_index(seg_tiles(tm), n_batch)
    x_args, x_specs, n_lat = _token_rows(x_lat, x_ctx, m, tm)
    return pl.pallas_call(
        functools.partial(_prenorm_kernel, n_lat=n_lat, g_row=g_row, sh_row=sh_row, sc_row=sc_row),
        grid=(m // tm,),
        in_specs=x_specs + [
            pl.BlockSpec(g.shape, lambda i: (0, 0)),
            pl.BlockSpec((None, mod.shape[1], d), lambda i: (seg(i), 0, 0)),
        ],
        out_specs=pl.BlockSpec((tm, d), lambda i: (i, 0)),
        out_shape=jax.ShapeDtypeStruct((m, d), BF16),
        compiler_params=_params("arbitrary"),
        name="prenorm",
    )(*x_args, g, mod)


def _resid_kernel(y_ref, xl_ref, xc_ref, g_ref, mod_ref, *rest, n_lat, gt_row, gpost_row, nxt):
    x = _read_token_rows(xl_ref, xc_ref, n_lat)
    xn = x + mod_ref[gt_row:gt_row + 1, :] * _rms(y_ref[...], g_ref[gpost_row:gpost_row + 1, :])
    if nxt is None:
        (xo_ref,) = rest
        xo_ref[...] = xn
        return
    g2_ref, mod2_ref, xo_ref, h_ref = rest
    gpre_row, sh_row, sc_row = nxt
    xo_ref[...] = xn
    h = _rms(xn, g2_ref[gpre_row:gpre_row + 1, :])
    h = h * (1.0 + mod2_ref[sc_row:sc_row + 1, :]) + mod2_ref[sh_row:sh_row + 1, :]
    h_ref[...] = h.astype(h_ref.dtype)


def _resid(y, x, g, mod, *, rows, seg_tiles, n_batch, gt_row, gpost_row, x_ctx=None, nxt=None, g2=None, mod2=None,
           tm=256):
    d = x.shape[1]
    seg = _seg_index(seg_tiles(tm), n_batch)
    row_spec = pl.BlockSpec((tm, d), lambda i: (i, 0))
    mod_spec = pl.BlockSpec((None, N_MOD, d), lambda i: (seg(i), 0, 0))
    x_args, x_specs, n_lat = _token_rows(x, x_ctx, rows, tm)
    in_specs = [row_spec] + x_specs + [pl.BlockSpec(g.shape, lambda i: (0, 0)), mod_spec]
    args = [y] + x_args + [g, mod]
    out_specs = [row_spec]
    out_shape = [jax.ShapeDtypeStruct((rows, d), F32)]
    if nxt is not None:
        in_specs += [pl.BlockSpec(g2.shape, lambda i: (0, 0)), mod_spec]
        args += [g2, mod2]
        out_specs.append(row_spec)
        out_shape.append(jax.ShapeDtypeStruct((rows, d), BF16))
    out = pl.pallas_call(
        functools.partial(_resid_kernel, n_lat=n_lat, gt_row=gt_row, gpost_row=gpost_row, nxt=nxt),
        grid=(rows // tm,),
        in_specs=in_specs,
        out_specs=out_specs,
        out_shape=out_shape,
        compiler_params=_params("arbitrary"),
        name="resid_norm",
    )(*args)
    return out if nxt is not None else out[0]


SIDE_CAST_ROWS = 32


def _mm_kernel(x_ref, w_ref, *rest, scale_blocks, scale, n_side):
    side_in, o_ref, side_out = rest[:n_side], rest[n_side], rest[n_side + 1:]
    acc = jnp.dot(x_ref[...], w_ref[...].astype(BF16), preferred_element_type=F32)
    if scale_blocks:
        acc = acc * jnp.where(pl.program_id(1) < scale_blocks, scale, 1.0)
    o_ref[...] = acc.astype(o_ref.dtype)
    for src, dst in zip(side_in, side_out):
        dst[...] = src[...].astype(dst.dtype)


def _matmul(x, w, *, rows, n_out, out_dtype, tm, tn, col_blk_off=0, scale_blocks=0, scale=1.0, side_casts=(),
            name="matmul"):
    k = x.shape[1]
    n_j = n_out // tn
    n_steps = (rows // tm) * n_j
    in_specs = [
        pl.BlockSpec((tm, k), lambda i, j: (i, 0)),
        pl.BlockSpec((k, tn), lambda i, j: (0, j + col_blk_off)),
    ]
    out_specs = [pl.BlockSpec((tm, tn), lambda i, j: (i, j))]
    out_shape = [jax.ShapeDtypeStruct((rows, n_out), out_dtype)]
    for ws in side_casts:
        n_slabs = ws.shape[0] // SIDE_CAST_ROWS
        assert n_slabs * SIDE_CAST_ROWS == ws.shape[0] and n_slabs <= n_steps
        spec = pl.BlockSpec((SIDE_CAST_ROWS, ws.shape[1]),
                            lambda i, j, n_slabs=n_slabs: (jnp.minimum(i * n_j + j, n_slabs - 1), 0))
        in_specs.append(spec)
        out_specs.append(spec)
        out_shape.append(jax.ShapeDtypeStruct(ws.shape, BF16))
    out = pl.pallas_call(
        functools.partial(_mm_kernel, scale_blocks=scale_blocks, scale=scale, n_side=len(side_casts)),
        grid=(rows // tm, n_j),
        in_specs=in_specs,
        out_specs=out_specs,
        out_shape=out_shape,
        compiler_params=_params("arbitrary", "arbitrary"),
        name=name,
    )(x, w, *side_casts)
    return (out[0], out[1:]) if side_casts else out[0]


def _mm_t_kernel(x_ref, w_ref, o_ref):
    acc = jnp.dot(x_ref[...], w_ref[...].astype(BF16), preferred_element_type=F32)
    o_ref[...] = acc.T.astype(o_ref.dtype)


def _matmul_t(x, w, *, rows, n_out, tm, tn, col_blk_off=0, name="matmul_t"):
    k = x.shape[1]
    return pl.pallas_call(
        _mm_t_kernel,
        grid=(n_out // tn, rows // tm),
        in_specs=[
            pl.BlockSpec((tm, k), lambda j, i: (i, 0)),
            pl.BlockSpec((k, tn), lambda j, i: (0, j + col_blk_off)),
        ],
        out_specs=pl.BlockSpec((None, tn, tm), lambda j, i: (i, j, 0)),
        out_shape=jax.ShapeDtypeStruct((rows // tm, n_out, tm), BF16),
        compiler_params=_params("arbitrary", "arbitrary"),
        name=name,
    )(x, w)


def _swap_halves(y):
    q4 = HEAD_DIM // 4
    lane = lax.broadcasted_iota(jnp.int32, y.shape, 1)
    return jnp.where((lane & q4) == 0, pltpu.roll(y, HEAD_DIM - q4, 1), pltpu.roll(y, q4, 1))


def _mm_rope_kernel(x_ref, w_ref, g_ref, cos_ref, sin_ref, o_ref, *, scale):
    cos = cos_ref[...]
    sin = sin_ref[...]
    g = g_ref[...]
    x = x_ref[...]
    for grp in range(w_ref.shape[1] // MXU_COLS):
        w = w_ref[:, grp * MXU_COLS:(grp + 1) * MXU_COLS].astype(BF16)
        acc = jnp.dot(x, w, preferred_element_type=F32)
        for hh in range(MXU_COLS // HEAD_DIM):
            y = _rms(acc[:, hh * HEAD_DIM:(hh + 1) * HEAD_DIM], g)
            y = y * cos + _swap_halves(y) * sin
            if scale != 1.0:
                y = y * scale
            col0 = grp * MXU_COLS + hh * HEAD_DIM
            o_ref[:, col0:col0 + HEAD_DIM] = y.astype(o_ref.dtype)


def _matmul_rope(x, w, g, cos, sin, *, rows, n_out, tm, tn, scale, name):
    k = x.shape[1]
    tab_spec = pl.BlockSpec((tm, HEAD_DIM), lambda i, j: (i, 0))
    return pl.pallas_call(
        functools.partial(_mm_rope_kernel, scale=scale),
        grid=(rows // tm, n_out // tn),
        in_specs=[
            pl.BlockSpec((tm, k), lambda i, j: (i, 0)),
            pl.BlockSpec((k, tn), lambda i, j: (0, j)),
            pl.BlockSpec((1, HEAD_DIM), lambda i, j: (0, 0)),
            tab_spec,
            tab_spec,
        ],
        out_specs=pl.BlockSpec((tm, tn), lambda i, j: (i, j)),
        out_shape=jax.ShapeDtypeStruct((rows, n_out), BF16),
        compiler_params=_params("arbitrary", "arbitrary"),
        name=name,
    )(x, w, g.reshape(1, HEAD_DIM), cos, sin)


def _mm_rope_t_kernel(x_ref, w_ref, cos_ref, sin_ref, o_ref):
    cos = cos_ref[...]
    sin = sin_ref[...]
    x = x_ref[...]
    q4 = HEAD_DIM // 4
    for grp in range(w_ref.shape[1] // MXU_COLS):
        w = w_ref[:, grp * MXU_COLS:(grp + 1) * MXU_COLS].astype(BF16)
        acc_t = jnp.dot(x, w, preferred_element_type=F32).T
        for hh in range(MXU_COLS // HEAD_DIM):
            y = acc_t[hh * HEAD_DIM:(hh + 1) * HEAD_DIM, :]
            r = lax.rsqrt(jnp.mean(y * y, axis=0, keepdims=True) + NORM_EPS)
            partner = jnp.concatenate([y[q4:2 * q4], y[:q4], y[3 * q4:], y[2 * q4:3 * q4]], axis=0)
            row0 = grp * MXU_COLS + hh * HEAD_DIM
            o_ref[row0:row0 + HEAD_DIM, :] = ((y * cos + partner * sin) * r).astype(o_ref.dtype)


def _matmul_rope_t(x, w, cos_t, sin_t, *, rows, n_out, tn, name):
    k = x.shape[1]
    n_tiles, _, tm = cos_t.shape
    assert n_tiles * tm == rows
    tab_spec = pl.BlockSpec((None, HEAD_DIM, tm), lambda i, j: (i, 0, 0))
    return pl.pallas_call(
        _mm_rope_t_kernel,
        grid=(n_tiles, n_out // tn),
        in_specs=[
            pl.BlockSpec((tm, k), lambda i, j: (i, 0)),
            pl.BlockSpec((k, tn), lambda i, j: (0, j)),
            tab_spec,
            tab_spec,
        ],
        out_specs=pl.BlockSpec((None, tn, tm), lambda i, j: (i, j, 0)),
        out_shape=jax.ShapeDtypeStruct((n_tiles, n_out, tm), BF16),
        compiler_params=_params("arbitrary", "arbitrary"),
        name=name,
    )(x, w, cos_t, sin_t)


def _gateup_kernel(x_ref, w1_ref, w3_ref, w2_ref, o_ref, w2b_ref):
    x = x_ref[...]
    a = jnp.dot(x, w1_ref[...].astype(BF16), preferred_element_type=F32)
    b = jnp.dot(x, w3_ref[...].astype(BF16), preferred_element_type=F32)
    o_ref[...] = (a * jax.nn.sigmoid(a) * b).astype(o_ref.dtype)
    w2b_ref[...] = w2_ref[...].astype(w2b_ref.dtype)


def _gateup(x, w13, w2, layer, *, rows, tm, tn):
    k = x.shape[1]
    f = w13.shape[2] // 2
    n_j = f // tn
    steps = (rows // tm) * n_j
    slab = w2.shape[1] // steps
    assert slab * steps == w2.shape[1] and slab % 16 == 0
    return pl.pallas_call(
        _gateup_kernel,
        grid=(rows // tm, n_j),
        in_specs=[
            pl.BlockSpec((tm, k), lambda i, j: (i, 0), pipeline_mode=pl.Buffered(1)),
            pl.BlockSpec((None, k, tn), lambda i, j: (layer, 0, j)),
            pl.BlockSpec((None, k, tn), lambda i, j: (layer, 0, j + n_j)),
            pl.BlockSpec((None, slab, w2.shape[2]), lambda i, j: (layer, i * n_j + j, 0)),
        ],
        out_specs=[
            pl.BlockSpec((tm, tn), lambda i, j: (i, j)),
            pl.BlockSpec((slab, w2.shape[2]), lambda i, j: (i * n_j + j, 0)),
        ],
        out_shape=[
            jax.ShapeDtypeStruct((rows, f), BF16),
            jax.ShapeDtypeStruct(w2.shape[1:], BF16),
        ],
        compiler_params=_params("arbitrary", "arbitrary"),
        name="ffn_gateup",
    )(x, w13, w13, w2)


_NT = (((1,), (1,)), ((), ()))


def _na_block_rows(rb):
    return rb + NA_WIN_H


N_DR = 2 * NA_WIN_H - 1
N_DC = 2 * NA_WIN_W - 1
_TAB_BOTH, _TAB_FIRST, _TAB_SECOND, _TAB_NONE = 0, N_DR + 1, 2 * N_DR + 1, 3 * N_DR + 1
_TAB_SIZE = 3 * N_DR + 2


def _na_build_bias_tiles(rpb_ref, tab_scr, h0, hb):
    shape = (GRID_W, 2 * GRID_W)
    c = lax.broadcasted_iota(jnp.int32, shape, 0)
    lane = lax.broadcasted_iota(jnp.int32, shape, 1)
    kc = lane & (GRID_W - 1)
    second = lane >= GRID_W
    cs = jnp.clip(c - NA_WIN_W // 2, 0, GRID_W - NA_WIN_W)
    in_win = (kc >= cs) & (kc < cs + NA_WIN_W)
    dci = kc - c + (NA_WIN_W - 1)
    is_dc = [dci == k for k in range(N_DC)]
    neg = jnp.full(shape, NEG_INF, F32)
    for hh in range(hb):
        base = (h0 + hh) * (N_DR * N_DC)
        rows = []
        for d in range(N_DR):
            t = neg
            for k in range(N_DC):
                t = jnp.where(is_dc[k], rpb_ref[base + d * N_DC + k] * LOG2E, t)
            rows.append(jnp.where(in_win, t, NEG_INF))
        for d in range(N_DR + 1):
            lo = rows[d - 1] if d >= 1 else neg
            hi = rows[d] if d < N_DR else neg
            tab_scr[hh, _TAB_BOTH + d] = jnp.where(second, hi, lo)
        for d in range(N_DR):
            tab_scr[hh, _TAB_FIRST + d] = jnp.where(second, neg, rows[d])
            tab_scr[hh, _TAB_SECOND + d] = jnp.where(second, rows[d], neg)
        tab_scr[hh, _TAB_NONE] = neg


def _na_kernel(rpb_ref, q_ref, k_ref, v_ref, kc_ref, vc_ref, c_ref, aw_ref, ab_ref, o_ref, mod_ref, tab_scr, *,
               rb, hb, nsb, rows, n_ada):
    kr = _na_block_rows(rb)
    qn = rb * GRID_W
    half = NA_WIN_H // 2
    step = (pl.program_id(0) * pl.num_programs(1) + pl.program_id(1)) * pl.num_programs(2) + pl.program_id(2)

    @pl.when(step < n_ada)
    def _():
        _ada_slab(c_ref, aw_ref, ab_ref, mod_ref)

    @pl.when((pl.program_id(1) == 0) & (pl.program_id(2) == 0))
    def _():
        _na_build_bias_tiles(rpb_ref, tab_scr, pl.program_id(0) * hb, hb)

    starts, tiles = [], []
    for sb in range(nsb):
        r0 = (pl.program_id(2) * nsb + sb) * rb
        kstart = jnp.clip(r0 - half, 0, rows - kr)
        starts.append(pl.multiple_of(kstart * GRID_W, GRID_W))
        tile_idx = []
        for i in range(rb):
            r = r0 + i
            rs = jnp.clip(r - half, 0, rows - NA_WIN_H)
            row_idx = []
            for jp in range(kr // 2):
                k0 = kstart + 2 * jp
                v0 = (k0 >= rs) & (k0 < rs + NA_WIN_H)
                v1 = (k0 + 1 >= rs) & (k0 + 1 < rs + NA_WIN_H)
                d0 = k0 - r + NA_WIN_H - 1
                idx = jnp.where(v0 & v1, _TAB_BOTH + d0 + 1,
                                jnp.where(v0, _TAB_FIRST + d0, jnp.where(v1, _TAB_SECOND + d0 + 1, _TAB_NONE)))
                row_idx.append(jnp.clip(idx, 0, _TAB_SIZE - 1))
            tile_idx.append(row_idx)
        tiles.append(tile_idx)

    def scores(sb, hh):
        cols = slice(hh * HEAD_DIM, (hh + 1) * HEAD_DIM)
        q = q_ref[sb * qn:(sb + 1) * qn, cols]
        ku = k_ref[pl.ds(starts[sb], kr * GRID_W), cols]
        bias = jnp.concatenate(
            [jnp.concatenate([tab_scr[hh, idx] for idx in row_idx], axis=1) for row_idx in tiles[sb]], axis=0)
        s_loc = lax.dot_general(q, ku, _NT, preferred_element_type=F32) + bias
        s_ctx = lax.dot_general(q, kc_ref[:, cols], _NT, preferred_element_type=F32)
        return s_loc, s_ctx

    def finish(sb, hh, s_loc, s_ctx):
        cols = slice(hh * HEAD_DIM, (hh + 1) * HEAD_DIM)
        vu = v_ref[pl.ds(starts[sb], kr * GRID_W), cols]
        m = jnp.maximum(jnp.max(s_loc, axis=-1, keepdims=True), jnp.max(s_ctx, axis=-1, keepdims=True))
        p_loc = jnp.exp2(s_loc - m)
        p_ctx = jnp.exp2(s_ctx - m)
        l = jnp.sum(p_loc, axis=-1, keepdims=True) + jnp.sum(p_ctx, axis=-1, keepdims=True)
        o = (jnp.dot(p_loc.astype(BF16), vu, preferred_element_type=F32)
             + jnp.dot(p_ctx.astype(BF16), vc_ref[:, cols], preferred_element_type=F32))
        o_ref[sb * qn:(sb + 1) * qn, cols] = (o / l).astype(o_ref.dtype)

    pairs = [(sb, hh) for sb in range(nsb) for hh in range(hb)]
    ahead = 1
    pending = [scores(*pair) for pair in pairs[:ahead]]
    for n, pair in enumerate(pairs):
        if n + ahead < len(pairs):
            pending.append(scores(*pairs[n + ahead]))
        finish(*pair, *pending.pop(0))


def _na_attention(qkv, rpb, cvec, ada_w, b_late, *, n_batch, seq, ctx_len, n_heads, rb=4, hb=2, nsb=8):
    m_all = qkv.shape[0]
    rows = seq // GRID_W
    assert rpb.shape == (n_heads, N_DR, N_DC) and _na_block_rows(rb) % 2 == 0 and rows % (rb * nsb) == 0
    hw = hb * HEAD_DIM
    hblocks = n_heads // hb
    qrows = nsb * rb * GRID_W
    n_r = rows // (rb * nsb)
    depth, d, n_mod = ada_w.shape
    n_late = b_late.shape[1] // ADA_SLAB
    n_early = depth * n_mod // ADA_SLAB - n_late
    assert n_late <= hblocks * n_batch * n_r
    step_of = lambda h, b, r, _: jnp.minimum((h * n_batch + b) * n_r + r, n_late - 1)
    ada_in_specs, ada_out_spec = _ada_slab_specs(d, n_early, n_mod // ADA_SLAB, step_of)
    lat_spec = lambda part: pl.BlockSpec((seq, hw), lambda h, b, r, _: (b, part * hblocks + h))
    ctx_spec = lambda part: pl.BlockSpec((ctx_len, hw),
                                         lambda h, b, r, _: (n_batch * seq // ctx_len + b, part * hblocks + h))
    q_spec = pl.BlockSpec((qrows, hw), lambda h, b, r, _: (b * (seq // qrows) + r, h))
    return pl.pallas_call(
        functools.partial(_na_kernel, rb=rb, hb=hb, nsb=nsb, rows=rows, n_ada=n_late),
        grid_spec=pltpu.PrefetchScalarGridSpec(
            num_scalar_prefetch=1,
            grid=(hblocks, n_batch, n_r),
            in_specs=[q_spec, lat_spec(1), lat_spec(2), ctx_spec(1), ctx_spec(2)] + ada_in_specs,
            out_specs=[q_spec, ada_out_spec],
            scratch_shapes=[pltpu.VMEM((hb, _TAB_SIZE, GRID_W, 2 * GRID_W), F32)],
        ),
        out_shape=[jax.ShapeDtypeStruct((m_all, n_heads * HEAD_DIM), BF16),
                   jax.ShapeDtypeStruct((MOD_ROWS, b_late.shape[1]), F32)],
        compiler_params=_params("arbitrary", "arbitrary", "arbitrary"),
        name="na_attention",
    )(rpb.reshape(-1), qkv, qkv, qkv, qkv, qkv, cvec, ada_w, b_late)


def _ctx_attn_kernel(q_ref, k_ref, v_ref, o_in_ref, o_ref, *, hb):
    del o_in_ref
    for hh in range(hb):
        cols = slice(hh * HEAD_DIM, (hh + 1) * HEAD_DIM)
        s = lax.dot_general(q_ref[:, cols], k_ref[:, cols], _NT, preferred_element_type=F32)
        p = jnp.exp2(s - jnp.max(s, axis=-1, keepdims=True))
        l = jnp.sum(p, axis=-1, keepdims=True)
        o = jnp.dot(p.astype(BF16), v_ref[:, cols], preferred_element_type=F32)
        o_ref[:, cols] = (o / l).astype(o_ref.dtype)


def _ctx_attention(qkv, o, *, n_batch, seq, ctx_len, n_heads, hb=8):
    hw = hb * HEAD_DIM
    hblocks = n_heads // hb
    row0 = n_batch * seq // ctx_len
    spec = lambda part: pl.BlockSpec((ctx_len, hw), lambda b, h: (row0 + b, part * hblocks + h))
    return pl.pallas_call(
        functools.partial(_ctx_attn_kernel, hb=hb),
        grid=(n_batch, hblocks),
        in_specs=[spec(0), spec(1), spec(2), pl.BlockSpec(memory_space=pl.ANY)],
        out_specs=spec(0),
        out_shape=jax.ShapeDtypeStruct(o.shape, o.dtype),
        input_output_aliases={3: 0},
        compiler_params=_params("arbitrary", "arbitrary"),
        name="ctx_attention",
    )(qkv, qkv, qkv, o)


def _gqa_kernel(q_ref, k_ref, vt_ref, kc_ref, vtc_ref, o_ref, m_scr, l_scr, acc_scr, s0_scr, s1_scr, x0_scr, x1_scr):
    n_chunks, _, tk = vt_ref.shape
    m_scr[...] = jnp.full(m_scr.shape, NEG_INF, F32)
    l_scr[...] = jnp.zeros(l_scr.shape, F32)
    acc_scr[...] = jnp.zeros(acc_scr.shape, F32)

    def scores(k, g):
        s = jnp.dot(k, q_ref[g * HEAD_DIM:(g + 1) * HEAD_DIM, :], preferred_element_type=F32)
        return s, jnp.max(s, axis=0, keepdims=True)

    def accumulate(s, s_max, vt, g):
        m_old = m_scr[g]
        m_new = jnp.maximum(m_old, s_max)
        alpha = jnp.exp2(m_old - m_new)
        p = jnp.exp2(s - m_new)
        l_scr[g] = alpha * l_scr[g] + jnp.sum(p, axis=0, keepdims=True)
        acc_scr[g] = alpha * acc_scr[g] + jnp.dot(vt, p.astype(BF16), preferred_element_type=F32)
        m_scr[g] = m_new

    def k_chunk(c):
        return k_ref[pl.ds(pl.multiple_of(c * tk, tk), tk), :]

    def stage(cur, nxt, c):
        k_next = k_chunk(c + 1)
        vt = vt_ref[c]
        for g in range(GQA_GROUP):
            nxt[0][g], nxt[1][g] = scores(k_next, g)
            accumulate(cur[0][g], cur[1][g], vt, g)

    buf0, buf1 = (s0_scr, x0_scr), (s1_scr, x1_scr)
    for g in range(GQA_GROUP):
        s0_scr[g], x0_scr[g] = scores(k_chunk(0), g)

    def body(j, carry):
        stage(buf0, buf1, 2 * j)
        stage(buf1, buf0, 2 * j + 1)
        return carry

    lax.fori_loop(0, n_chunks // 2 - 1, body, 0)
    stage(buf0, buf1, n_chunks - 2)
    vt_last = vt_ref[n_chunks - 1]
    for g in range(GQA_GROUP):
        s_ctx, x_ctx = scores(kc_ref[...], g)
        accumulate(s1_scr[g], x1_scr[g], vt_last, g)
        accumulate(s_ctx, x_ctx, vtc_ref[...], g)
    for g in range(GQA_GROUP):
        o_ref[:, g * HEAD_DIM:(g + 1) * HEAD_DIM] = (acc_scr[g] / l_scr[g]).T.astype(o_ref.dtype)


def _gqa_attention(qt, k, vt, *, n_batch, seq, ctx_len):
    n_kv = k.shape[1] // HEAD_DIM
    tq = qt.shape[2]
    tk = vt.shape[2]
    assert (seq // tk) % 2 == 0 and seq % tq == 0
    gw = GQA_GROUP * HEAD_DIM
    n_qt = seq // tq
    q_spec = pl.BlockSpec((tq, gw), lambda b, h, i: (b * n_qt + i, h))
    return pl.pallas_call(
        _gqa_kernel,
        grid=(n_batch, n_kv, n_qt),
        in_specs=[
            pl.BlockSpec((None, gw, tq), lambda b, h, i: (b * n_qt + i, h, 0)),
            pl.BlockSpec((seq, HEAD_DIM), lambda b, h, i: (b, h)),
            pl.BlockSpec((seq // tk, HEAD_DIM, tk), lambda b, h, i: (b, h, 0)),
            pl.BlockSpec((ctx_len, HEAD_DIM), lambda b, h, i: (n_batch * seq // ctx_len + b, h)),
            pl.BlockSpec((None, HEAD_DIM, ctx_len), lambda b, h, i: (n_batch * seq // tk, h, b)),
        ],
        out_specs=q_spec,
        out_shape=jax.ShapeDtypeStruct((n_batch * seq, qt.shape[1]), BF16),
        scratch_shapes=[
            pltpu.VMEM((GQA_GROUP, 1, tq), F32),
            pltpu.VMEM((GQA_GROUP, 1, tq), F32),
            pltpu.VMEM((GQA_GROUP, HEAD_DIM, tq), F32),
            pltpu.VMEM((GQA_GROUP, tk, tq), F32),
            pltpu.VMEM((GQA_GROUP, tk, tq), F32),
            pltpu.VMEM((GQA_GROUP, 1, tq), F32),
            pltpu.VMEM((GQA_GROUP, 1, tq), F32),
        ],
        compiler_params=_params("arbitrary", "arbitrary", "arbitrary"),
        name="gqa_attention",
    )(qt, k, vt, k, vt)


def _rope_partner(v):
    q4 = HEAD_DIM // 4
    return jnp.concatenate([v[..., q4:2 * q4], v[..., :q4], v[..., 3 * q4:], v[..., 2 * q4:3 * q4]], axis=-1)


def _rope_tables_t(cos, sin, g, scale, rows, tm):
    cos_t = (cos[:rows] * (g * scale)[None, :]).reshape(rows // tm, tm, HEAD_DIM)
    sin_t = (sin[:rows] * (_rope_partner(g) * scale)[None, :]).reshape(rows // tm, tm, HEAD_DIM)
    return jnp.transpose(cos_t, (0, 2, 1)), jnp.transpose(sin_t, (0, 2, 1))


def _rope_tables(n_batch, seq, ctx_len):
    quarter = HEAD_DIM // 4
    t = jnp.arange(seq)
    freqs = ROPE_THETA ** (-jnp.arange(quarter, dtype=F32) / quarter)
    ang_r = (t // GRID_W).astype(F32)[:, None] * freqs[None, :]
    ang_c = (t % GRID_W).astype(F32)[:, None] * freqs[None, :]
    cos = jnp.concatenate([jnp.cos(ang_r)] * 2 + [jnp.cos(ang_c)] * 2, axis=-1)
    sin = jnp.concatenate([-jnp.sin(ang_r), jnp.sin(ang_r), -jnp.sin(ang_c), jnp.sin(ang_c)], axis=-1)
    n_ctx = n_batch * ctx_len
    cos = jnp.concatenate([jnp.tile(cos, (n_batch, 1)), jnp.ones((n_ctx, HEAD_DIM), F32)], axis=0)
    sin = jnp.concatenate([jnp.tile(sin, (n_batch, 1)), jnp.zeros((n_ctx, HEAD_DIM), F32)], axis=0)
    return cos, sin


def kernel(x, c, ctx, c_ctx, ada_w, ada_b, norm_g, na_wqkv, na_wo, na_rpb, gqa_wq, gqa_wkv, gqa_q_norm,
           gqa_k_norm, gqa_wo, ffn_w13, ffn_w2):
    n_batch, seq, d = x.shape
    ctx_len = ctx.shape[1]
    depth = ada_w.shape[0]
    assert depth == 2 and na_wqkv.shape[0] == 1 and gqa_wq.shape[0] == 1
    assert seq % GRID_W == 0 and n_batch + 1 <= MOD_ROWS
    n_heads = d // HEAD_DIM
    m_lat = n_batch * seq
    m_all = m_lat + n_batch * ctx_len
    scale = HEAD_DIM ** -0.5 * LOG2E
    tm_all = m_all // 8
    tm_lat = m_lat // 8
    tn_f32, tn_b16, tn_gate, tn_down = 2 * MXU_COLS, 4 * MXU_COLS, MXU_COLS, 2 * MXU_COLS
    gqa_tk = 2 * MXU_COLS
    seg_tiles = lambda t: seq // t
    assert tm_all % 32 == 0 and tm_lat % 32 == 0

    cvec = jnp.zeros((MOD_ROWS, d), F32).at[:n_batch].set(c).at[n_batch].set(c_ctx)
    n_mod_early = 2 * d
    b_flat = ada_b.reshape(1, depth * N_MOD * d)
    mod_early = _ada_early(cvec, ada_w, b_flat[:, :n_mod_early])
    cos, sin = _rope_tables(n_batch, seq, ctx_len)
    cos_qt, sin_qt = _rope_tables_t(cos, sin, gqa_q_norm[0], scale, m_lat, tm_lat)

    x_lat = x.reshape(m_lat, d)
    x_ctx = ctx.reshape(n_batch * ctx_len, d)
    seg_kw = dict(seg_tiles=seg_tiles, n_batch=n_batch)

    h = _prenorm(x_lat, x_ctx, norm_g[0], mod_early.reshape(MOD_ROWS, 2, d), g_row=0, sh_row=0, sc_row=1, **seg_kw)
    qkv, (na_wo_b, wq, wkv, gqa_wo_b) = _matmul(
        h, na_wqkv[0], rows=m_all, n_out=na_wqkv.shape[2], out_dtype=BF16, tm=tm_all, tn=tn_f32,
        scale_blocks=d // tn_f32, scale=scale, side_casts=(na_wo[0], gqa_wq[0], gqa_wkv[0], gqa_wo[0]), name="na_qkv")
    o, mod_late = _na_attention(qkv, na_rpb[0], cvec, ada_w, b_flat[:, n_mod_early:],
                                n_batch=n_batch, seq=seq, ctx_len=ctx_len, n_heads=n_heads)
    mod = jnp.concatenate([mod_early, mod_late], axis=1).reshape(MOD_ROWS, depth, N_MOD, d)
    mod = [mod[:, i] for i in range(depth)]
    o = _ctx_attention(qkv, o, n_batch=n_batch, seq=seq, ctx_len=ctx_len, n_heads=n_heads)
    y = _matmul(o, na_wo_b, rows=m_all, n_out=d, out_dtype=F32, tm=tm_all, tn=tn_b16, name="na_wo")
    xa, h = _resid(y, x_lat, norm_g[0], mod[0], x_ctx=x_ctx, rows=m_all, gt_row=2, gpost_row=1,
                   nxt=(2, 3, 4), g2=norm_g[0], mod2=mod[0], **seg_kw)
    gu, w2 = _gateup(h, ffn_w13, ffn_w2, 0, rows=m_all, tm=2 * tm_all, tn=tn_gate)
    y = _matmul(gu, w2, rows=m_all, n_out=d, out_dtype=F32, tm=tm_all // 2, tn=tn_down, name="ffn_down")
    xa, h = _resid(y, xa, norm_g[0], mod[0], rows=m_all, gt_row=5, gpost_row=3,
                   nxt=(0, 0, 1), g2=norm_g[1], mod2=mod[1], **seg_kw)

    kv_w = wkv.shape[1] // 2
    qt = _matmul_rope_t(h, wq, cos_qt, sin_qt, rows=m_lat, n_out=d, tn=tn_b16, name="gqa_q")
    k = _matmul_rope(h, wkv, gqa_k_norm[0], cos, sin, rows=m_all, n_out=kv_w, tm=tm_all, tn=kv_w, scale=1.0, name="gqa_k")
    assert seq % gqa_tk == 0 and n_batch * ctx_len == gqa_tk
    vt = _matmul_t(h, wkv, rows=m_all, n_out=kv_w, tm=gqa_tk, tn=kv_w, col_blk_off=1, name="gqa_v")
    o = _gqa_attention(qt, k, vt, n_batch=n_batch, seq=seq, ctx_len=ctx_len)
    y = _matmul(o, gqa_wo_b, rows=m_lat, n_out=d, out_dtype=F32, tm=tm_lat, tn=tn_b16, name="gqa_wo")
    xl, h = _resid(y, xa, norm_g[1], mod[1], rows=m_lat, gt_row=2, gpost_row=1,
                   nxt=(2, 3, 4), g2=norm_g[1], mod2=mod[1], **seg_kw)
    gu, w2 = _gateup(h, ffn_w13, ffn_w2, 1, rows=m_lat, tm=2 * tm_lat, tn=tn_gate)
    y = _matmul(gu, w2, rows=m_lat, n_out=d, out_dtype=F32, tm=tm_lat // 2, tn=tn_down, name="ffn_down")
    xl = _resid(y, xl, norm_g[1], mod[1], rows=m_lat, gt_row=5, gpost_row=3, **seg_kw)
    return xl.reshape(n_batch, seq, d)
```

```python
import functools

import jax
import jax.numpy as jnp
from jax import lax
from jax.experimental import pallas as pl
from jax.experimental.pallas import tpu as pltpu

GRID_W = 64
NA_WIN_H = 8
NA_WIN_W = 16
HEAD_DIM = 128
GQA_GROUP = 4
ROPE_THETA = 10000.0
NORM_EPS = 1e-6
NEG_INF = -1e30
LOG2E = 1.4426950408889634
N_MOD = 6

VMEM_LIMIT_BYTES = 56 * 1024 * 1024
MXU_COLS = 256
MOD_ROWS = 8

F32 = jnp.float32
BF16 = jnp.bfloat16


def _params(*sem):
    return pltpu.CompilerParams(dimension_semantics=sem, vmem_limit_bytes=VMEM_LIMIT_BYTES)


def _rms(x, g):
    ms = jnp.mean(x * x, axis=-1, keepdims=True)
    return x * lax.rsqrt(ms + NORM_EPS) * g


def _seg_index(rows_per_seg_tiles, n_batch):
    return lambda i: jnp.minimum(i // rows_per_seg_tiles, n_batch)


ADA_SLAB = 1024


def _ada_slab(c_ref, w_ref, b_ref, o_ref):
    c = c_ref[...]
    s = (c * jax.nn.sigmoid(c)).astype(BF16)
    o_ref[...] = jnp.dot(s, w_ref[...].astype(BF16), preferred_element_type=F32) + b_ref[...]


def _ada_slab_specs(d, n_early, per_layer, step_of):
    def w_map(*idx):
        f = step_of(*idx) + n_early
        return f // per_layer, 0, f % per_layer
    flat = lambda *idx: (0, step_of(*idx))
    return [pl.BlockSpec((MOD_ROWS, d), lambda *idx: (0, 0)),
            pl.BlockSpec((None, d, ADA_SLAB), w_map),
            pl.BlockSpec((1, ADA_SLAB), flat)], pl.BlockSpec((MOD_ROWS, ADA_SLAB), flat)


def _ada_early(cvec, ada_w, b_early):
    d = ada_w.shape[1]
    n = b_early.shape[1]
    in_specs, out_spec = _ada_slab_specs(d, 0, ada_w.shape[2] // ADA_SLAB, lambda j: j)
    return pl.pallas_call(
        _ada_slab,
        grid=(n // ADA_SLAB,),
        in_specs=in_specs,
        out_specs=out_spec,
        out_shape=jax.ShapeDtypeStruct((MOD_ROWS, n), F32),
        compiler_params=_params("arbitrary"),
        name="ada_mod",
    )(cvec, ada_w, b_early)


def _token_rows(x_lat, x_ctx, rows, tm):
    d = x_lat.shape[1]
    n_lat = min(rows, x_lat.shape[0]) // tm
    if x_ctx is None:
        x_ctx = x_lat
        assert rows <= x_lat.shape[0]
    else:
        assert x_lat.shape[0] % tm == 0 and rows == x_lat.shape[0] + x_ctx.shape[0]
    specs = [pl.BlockSpec((tm, d), lambda i, *_: (jnp.minimum(i, n_lat - 1), 0)),
             pl.BlockSpec((tm, d), lambda i, *_: (jnp.maximum(i - n_lat, 0), 0))]
    return [x_lat, x_ctx], specs, n_lat


def _read_token_rows(xl_ref, xc_ref, n_lat):
    return jnp.where(pl.program_id(0) < n_lat, xl_ref[...], xc_ref[...])


def _prenorm_kernel(xl_ref, xc_ref, g_ref, mod_ref, h_ref, *, n_lat, g_row, sh_row, sc_row):
    y = _rms(_read_token_rows(xl_ref, xc_ref, n_lat), g_ref[g_row:g_row + 1, :])
    h = y * (1.0 + mod_ref[sc_row:sc_row + 1, :]) + mod_ref[sh_row:sh_row + 1, :]
    h_ref[...] = h.astype(h_ref.dtype)


def _prenorm(x_lat, x_ctx, g, mod, *, seg_tiles, n_batch, g_row, sh_row, sc_row, tm=512):
    d = x_lat.shape[1]
    m = x_lat.shape[0] + x_ctx.shape[0]
    seg = _seg_index(seg_tiles(tm), n_batch)
    x_args, x_specs, n_lat = _token_rows(x_lat, x_ctx, m, tm)
    return pl.pallas_call(
        functools.partial(_prenorm_kernel, n_lat=n_lat, g_row=g_row, sh_row=sh_row, sc_row=sc_row),
        grid=(m // tm,),
        in_specs=x_specs + [
            pl.BlockSpec(g.shape, lambda i: (0, 0)),
            pl.BlockSpec((None, mod.shape[1], d), lambda i: (seg(i), 0, 0)),
        ],
        out_specs=pl.BlockSpec((tm, d), lambda i: (i, 0)),
        out_shape=jax.ShapeDtypeStruct((m, d), BF16),
        compiler_params=_params("arbitrary"),
        name="prenorm",
    )(*x_args, g, mod)


def _resid_kernel(y_ref, xl_ref, xc_ref, g_ref, mod_ref, *rest, n_lat, gt_row, gpost_row, nxt):
    x = _read_token_rows(xl_ref, xc_ref, n_lat)
    xn = x + mod_ref[gt_row:gt_row + 1, :] * _rms(y_ref[...], g_ref[gpost_row:gpost_row + 1, :])
    if nxt is None:
        (xo_ref,) = rest
        xo_ref[...] = xn
        return
    g2_ref, mod2_ref, xo_ref, h_ref = rest
    gpre_row, sh_row, sc_row = nxt
    xo_ref[...] = xn
    h = _rms(xn, g2_ref[gpre_row:gpre_row + 1, :])
    h = h * (1.0 + mod2_ref[sc_row:sc_row + 1, :]) + mod2_ref[sh_row:sh_row + 1, :]
    h_ref[...] = h.astype(h_ref.dtype)


def _resid(y, x, g, mod, *, rows, seg_tiles, n_batch, gt_row, gpost_row, x_ctx=None, nxt=None, g2=None, mod2=None,
           tm=256):
    d = x.shape[1]
    seg = _seg_index(seg_tiles(tm), n_batch)
    row_spec = pl.BlockSpec((tm, d), lambda i: (i, 0))
    mod_spec = pl.BlockSpec((None, N_MOD, d), lambda i: (seg(i), 0, 0))
    x_args, x_specs, n_lat = _token_rows(x, x_ctx, rows, tm)
    in_specs = [row_spec] + x_specs + [pl.BlockSpec(g.shape, lambda i: (0, 0)), mod_spec]
    args = [y] + x_args + [g, mod]
    out_specs = [row_spec]
    out_shape = [jax.ShapeDtypeStruct((rows, d), F32)]
    if nxt is not None:
        in_specs += [pl.BlockSpec(g2.shape, lambda i: (0, 0)), mod_spec]
        args += [g2, mod2]
        out_specs.append(row_spec)
        out_shape.append(jax.ShapeDtypeStruct((rows, d), BF16))
    out = pl.pallas_call(
        functools.partial(_resid_kernel, n_lat=n_lat, gt_row=gt_row, gpost_row=gpost_row, nxt=nxt),
        grid=(rows // tm,),
        in_specs=in_specs,
        out_specs=out_specs,
        out_shape=out_shape,
        compiler_params=_params("arbitrary"),
        name="resid_norm",
    )(*args)
    return out if nxt is not None else out[0]


SIDE_CAST_ROWS = 32


def _mm_kernel(x_ref, w_ref, *rest, scale_blocks, scale, n_side):
    side_in, o_ref, side_out = rest[:n_side], rest[n_side], rest[n_side + 1:]
    acc = jnp.dot(x_ref[...], w_ref[...].astype(BF16), preferred_element_type=F32)
    if scale_blocks:
        acc = acc * jnp.where(pl.program_id(1) < scale_blocks, scale, 1.0)
    o_ref[...] = acc.astype(o_ref.dtype)
    for src, dst in zip(side_in, side_out):
        dst[...] = src[...].astype(dst.dtype)


def _matmul(x, w, *, rows, n_out, out_dtype, tm, tn, col_blk_off=0, scale_blocks=0, scale=1.0, side_casts=(),
            name="matmul"):
    k = x.shape[1]
    n_j = n_out // tn
    n_steps = (rows // tm) * n_j
    in_specs = [
        pl.BlockSpec((tm, k), lambda i, j: (i, 0)),
        pl.BlockSpec((k, tn), lambda i, j: (0, j + col_blk_off)),
    ]
    out_specs = [pl.BlockSpec((tm, tn), lambda i, j: (i, j))]
    out_shape = [jax.ShapeDtypeStruct((rows, n_out), out_dtype)]
    for ws in side_casts:
        n_slabs = ws.shape[0] // SIDE_CAST_ROWS
        assert n_slabs * SIDE_CAST_ROWS == ws.shape[0] and n_slabs <= n_steps
        spec = pl.BlockSpec((SIDE_CAST_ROWS, ws.shape[1]),
                            lambda i, j, n_slabs=n_slabs: (jnp.minimum(i * n_j + j, n_slabs - 1), 0))
        in_specs.append(spec)
        out_specs.append(spec)
        out_shape.append(jax.ShapeDtypeStruct(ws.shape, BF16))
    out = pl.pallas_call(
        functools.partial(_mm_kernel, scale_blocks=scale_blocks, scale=scale, n_side=len(side_casts)),
        grid=(rows // tm, n_j),
        in_specs=in_specs,
        out_specs=out_specs,
        out_shape=out_shape,
        compiler_params=_params("arbitrary", "arbitrary"),
        name=name,
    )(x, w, *side_casts)
    return (out[0], out[1:]) if side_casts else out[0]


def _mm_two_src_kernel(xl_ref, xc_ref, w_ref, o_ref, *, n_lat):
    x = _read_token_rows(xl_ref, xc_ref, n_lat)
    o_ref[...] = jnp.dot(x, w_ref[...], preferred_element_type=F32).astype(o_ref.dtype)


def _matmul_two_src(x_lat, x_ctx, w, *, out_dtype, tm, tn, name):
    rows = x_lat.shape[0] + x_ctx.shape[0]
    k, n_out = w.shape
    x_args, x_specs, n_lat = _token_rows(x_lat, x_ctx, rows, tm)
    return pl.pallas_call(
        functools.partial(_mm_two_src_kernel, n_lat=n_lat),
        grid=(rows // tm, n_out // tn),
        in_specs=x_specs + [pl.BlockSpec((k, tn), lambda i, j: (0, j))],
        out_specs=pl.BlockSpec((tm, tn), lambda i, j: (i, j)),
        out_shape=jax.ShapeDtypeStruct((rows, n_out), out_dtype),
        compiler_params=_params("arbitrary", "arbitrary"),
        name=name,
    )(*x_args, w)


def _mm_t_kernel(x_ref, w_ref, o_ref):
    acc = jnp.dot(x_ref[...], w_ref[...].astype(BF16), preferred_element_type=F32)
    o_ref[...] = acc.T.astype(o_ref.dtype)


def _matmul_t(x, w, *, rows, n_out, tm, tn, col_blk_off=0, name="matmul_t"):
    k = x.shape[1]
    return pl.pallas_call(
        _mm_t_kernel,
        grid=(n_out // tn, rows // tm),
        in_specs=[
            pl.BlockSpec((tm, k), lambda j, i: (i, 0)),
            pl.BlockSpec((k, tn), lambda j, i: (0, j + col_blk_off)),
        ],
        out_specs=pl.BlockSpec((None, tn, tm), lambda j, i: (i, j, 0)),
        out_shape=jax.ShapeDtypeStruct((rows // tm, n_out, tm), BF16),
        compiler_params=_params("arbitrary", "arbitrary"),
        name=name,
    )(x, w)


def _swap_halves(y):
    q4 = HEAD_DIM // 4
    lane = lax.broadcasted_iota(jnp.int32, y.shape, 1)
    return jnp.where((lane & q4) == 0, pltpu.roll(y, HEAD_DIM - q4, 1), pltpu.roll(y, q4, 1))


def _mm_rope_kernel(x_ref, w_ref, g_ref, cos_ref, sin_ref, o_ref, *, scale):
    cos = cos_ref[...]
    sin = sin_ref[...]
    g = g_ref[...]
    x = x_ref[...]
    for grp in range(w_ref.shape[1] // MXU_COLS):
        w = w_ref[:, grp * MXU_COLS:(grp + 1) * MXU_COLS].astype(BF16)
        acc = jnp.dot(x, w, preferred_element_type=F32)
        for hh in range(MXU_COLS // HEAD_DIM):
            y = _rms(acc[:, hh * HEAD_DIM:(hh + 1) * HEAD_DIM], g)
            y = y * cos + _swap_halves(y) * sin
            if scale != 1.0:
                y = y * scale
            col0 = grp * MXU_COLS + hh * HEAD_DIM
            o_ref[:, col0:col0 + HEAD_DIM] = y.astype(o_ref.dtype)


def _matmul_rope(x, w, g, cos, sin, *, rows, n_out, tm, tn, scale, name):
    k = x.shape[1]
    tab_spec = pl.BlockSpec((tm, HEAD_DIM), lambda i, j: (i, 0))
    return pl.pallas_call(
        functools.partial(_mm_rope_kernel, scale=scale),
        grid=(rows // tm, n_out // tn),
        in_specs=[
            pl.BlockSpec((tm, k), lambda i, j: (i, 0)),
            pl.BlockSpec((k, tn), lambda i, j: (0, j)),
            pl.BlockSpec((1, HEAD_DIM), lambda i, j: (0, 0)),
            tab_spec,
            tab_spec,
        ],
        out_specs=pl.BlockSpec((tm, tn), lambda i, j: (i, j)),
        out_shape=jax.ShapeDtypeStruct((rows, n_out), BF16),
        compiler_params=_params("arbitrary", "arbitrary"),
        name=name,
    )(x, w, g.reshape(1, HEAD_DIM), cos, sin)


def _mm_rope_t_kernel(x_ref, w_ref, cos_ref, sin_ref, o_ref):
    cos = cos_ref[...]
    sin = sin_ref[...]
    x = x_ref[...]
    q4 = HEAD_DIM // 4
    for grp in range(w_ref.shape[1] // MXU_COLS):
        w = w_ref[:, grp * MXU_COLS:(grp + 1) * MXU_COLS].astype(BF16)
        acc_t = jnp.dot(x, w, preferred_element_type=F32).T
        for hh in range(MXU_COLS // HEAD_DIM):
            y = acc_t[hh * HEAD_DIM:(hh + 1) * HEAD_DIM, :]
            r = lax.rsqrt(jnp.mean(y * y, axis=0, keepdims=True) + NORM_EPS)
            partner = jnp.concatenate([y[q4:2 * q4], y[:q4], y[3 * q4:], y[2 * q4:3 * q4]], axis=0)
            row0 = grp * MXU_COLS + hh * HEAD_DIM
            o_ref[row0:row0 + HEAD_DIM, :] = ((y * cos + partner * sin) * r).astype(o_ref.dtype)


def _matmul_rope_t(x, w, cos_t, sin_t, *, rows, n_out, tn, name):
    k = x.shape[1]
    n_tiles, _, tm = cos_t.shape
    assert n_tiles * tm == rows
    tab_spec = pl.BlockSpec((None, HEAD_DIM, tm), lambda i, j: (i, 0, 0))
    return pl.pallas_call(
        _mm_rope_t_kernel,
        grid=(n_tiles, n_out // tn),
        in_specs=[
            pl.BlockSpec((tm, k), lambda i, j: (i, 0)),
            pl.BlockSpec((k, tn), lambda i, j: (0, j)),
            tab_spec,
            tab_spec,
        ],
        out_specs=pl.BlockSpec((None, tn, tm), lambda i, j: (i, j, 0)),
        out_shape=jax.ShapeDtypeStruct((n_tiles, n_out, tm), BF16),
        compiler_params=_params("arbitrary", "arbitrary"),
        name=name,
    )(x, w, cos_t, sin_t)


def _gateup_kernel(x_ref, w1_ref, w3_ref, w2_ref, o_ref, w2b_ref):
    x = x_ref[...]
    a = jnp.dot(x, w1_ref[...].astype(BF16), preferred_element_type=F32)
    b = jnp.dot(x, w3_ref[...].astype(BF16), preferred_element_type=F32)
    o_ref[...] = (a * jax.nn.sigmoid(a) * b).astype(o_ref.dtype)
    w2b_ref[...] = w2_ref[...].astype(w2b_ref.dtype)


def _gateup(x, w13, w2, layer, *, rows, tm, tn):
    k = x.shape[1]
    f = w13.shape[2] // 2
    n_j = f // tn
    steps = (rows // tm) * n_j
    slab = w2.shape[1] // steps
    assert slab * steps == w2.shape[1] and slab % 16 == 0
    return pl.pallas_call(
        _gateup_kernel,
        grid=(rows // tm, n_j),
        in_specs=[
            pl.BlockSpec((tm, k), lambda i, j: (i, 0), pipeline_mode=pl.Buffered(1)),
            pl.BlockSpec((None, k, tn), lambda i, j: (layer, 0, j)),
            pl.BlockSpec((None, k, tn), lambda i, j: (layer, 0, j + n_j)),
            pl.BlockSpec((None, slab, w2.shape[2]), lambda i, j: (layer, i * n_j + j, 0)),
        ],
        out_specs=[
            pl.BlockSpec((tm, tn), lambda i, j: (i, j)),
            pl.BlockSpec((slab, w2.shape[2]), lambda i, j: (i * n_j + j, 0)),
        ],
        out_shape=[
            jax.ShapeDtypeStruct((rows, f), BF16),
            jax.ShapeDtypeStruct(w2.shape[1:], BF16),
        ],
        compiler_params=_params("arbitrary", "arbitrary"),
        name="ffn_gateup",
    )(x, w13, w13, w2)


_NT = (((1,), (1,)), ((), ()))


def _na_block_rows(rb):
    return rb + NA_WIN_H


N_DR = 2 * NA_WIN_H - 1
N_DC = 2 * NA_WIN_W - 1
_TAB_BOTH, _TAB_FIRST, _TAB_SECOND, _TAB_NONE = 0, N_DR + 1, 2 * N_DR + 1, 3 * N_DR + 1
_TAB_SIZE = 3 * N_DR + 2


def _na_build_bias_tiles(rpb_ref, tab_scr, h0, hb):
    shape = (GRID_W, 2 * GRID_W)
    c = lax.broadcasted_iota(jnp.int32, shape, 0)
    lane = lax.broadcasted_iota(jnp.int32, shape, 1)
    kc = lane & (GRID_W - 1)
    second = lane >= GRID_W
    cs = jnp.clip(c - NA_WIN_W // 2, 0, GRID_W - NA_WIN_W)
    in_win = (kc >= cs) & (kc < cs + NA_WIN_W)
    dci = kc - c + (NA_WIN_W - 1)
    is_dc = [dci == k for k in range(N_DC)]
    neg = jnp.full(shape, NEG_INF, F32)
    for hh in range(hb):
        base = (h0 + hh) * (N_DR * N_DC)
        rows = []
        for d in range(N_DR):
            t = neg
            for k in range(N_DC):
                t = jnp.where(is_dc[k], rpb_ref[base + d * N_DC + k] * LOG2E, t)
            rows.append(jnp.where(in_win, t, NEG_INF))
        for d in range(N_DR + 1):
            lo = rows[d - 1] if d >= 1 else neg
            hi = rows[d] if d < N_DR else neg
            tab_scr[hh, _TAB_BOTH + d] = jnp.where(second, hi, lo)
        for d in range(N_DR):
            tab_scr[hh, _TAB_FIRST + d] = jnp.where(second, neg, rows[d])
            tab_scr[hh, _TAB_SECOND + d] = jnp.where(second, rows[d], neg)
        tab_scr[hh, _TAB_NONE] = neg


def _na_kernel(rpb_ref, q_ref, k_ref, v_ref, kc_ref, vc_ref, c_ref, aw_ref, ab_ref, o_ref, mod_ref, tab_scr, *,
               rb, hb, nsb, rows, n_ada):
    kr = _na_block_rows(rb)
    qn = rb * GRID_W
    half = NA_WIN_H // 2
    step = (pl.program_id(0) * pl.num_programs(1) + pl.program_id(1)) * pl.num_programs(2) + pl.program_id(2)

    @pl.when(step < n_ada)
    def _():
        _ada_slab(c_ref, aw_ref, ab_ref, mod_ref)

    @pl.when((pl.program_id(1) == 0) & (pl.program_id(2) == 0))
    def _():
        _na_build_bias_tiles(rpb_ref, tab_scr, pl.program_id(0) * hb, hb)

    starts, tiles = [], []
    for sb in range(nsb):
        r0 = (pl.program_id(2) * nsb + sb) * rb
        kstart = jnp.clip(r0 - half, 0, rows - kr)
        starts.append(pl.multiple_of(kstart * GRID_W, GRID_W))
        tile_idx = []
        for i in range(rb):
            r = r0 + i
            rs = jnp.clip(r - half, 0, rows - NA_WIN_H)
            row_idx = []
            for jp in range(kr // 2):
                k0 = kstart + 2 * jp
                v0 = (k0 >= rs) & (k0 < rs + NA_WIN_H)
                v1 = (k0 + 1 >= rs) & (k0 + 1 < rs + NA_WIN_H)
                d0 = k0 - r + NA_WIN_H - 1
                idx = jnp.where(v0 & v1, _TAB_BOTH + d0 + 1,
                                jnp.where(v0, _TAB_FIRST + d0, jnp.where(v1, _TAB_SECOND + d0 + 1, _TAB_NONE)))
                row_idx.append(jnp.clip(idx, 0, _TAB_SIZE - 1))
            tile_idx.append(row_idx)
        tiles.append(tile_idx)

    def scores(sb, hh):
        cols = slice(hh * HEAD_DIM, (hh + 1) * HEAD_DIM)
        q = q_ref[sb * qn:(sb + 1) * qn, cols]
        ku = k_ref[pl.ds(starts[sb], kr * GRID_W), cols]
        bias = jnp.concatenate(
            [jnp.concatenate([tab_scr[hh, idx] for idx in row_idx], axis=1) for row_idx in tiles[sb]], axis=0)
        s_loc = lax.dot_general(q, ku, _NT, preferred_element_type=F32) + bias
        s_ctx = lax.dot_general(q, kc_ref[:, cols], _NT, preferred_element_type=F32)
        return s_loc, s_ctx

    def finish(sb, hh, s_loc, s_ctx):
        cols = slice(hh * HEAD_DIM, (hh + 1) * HEAD_DIM)
        vu = v_ref[pl.ds(starts[sb], kr * GRID_W), cols]
        m = jnp.maximum(jnp.max(s_loc, axis=-1, keepdims=True), jnp.max(s_ctx, axis=-1, keepdims=True))
        p_loc = jnp.exp2(s_loc - m)
        p_ctx = jnp.exp2(s_ctx - m)
        l = jnp.sum(p_loc, axis=-1, keepdims=True) + jnp.sum(p_ctx, axis=-1, keepdims=True)
        o = (jnp.dot(p_loc.astype(BF16), vu, preferred_element_type=F32)
             + jnp.dot(p_ctx.astype(BF16), vc_ref[:, cols], preferred_element_type=F32))
        o_ref[sb * qn:(sb + 1) * qn, cols] = (o / l).astype(o_ref.dtype)

    pairs = [(sb, hh) for sb in range(nsb) for hh in range(hb)]
    ahead = 1
    pending = [scores(*pair) for pair in pairs[:ahead]]
    for n, pair in enumerate(pairs):
        if n + ahead < len(pairs):
            pending.append(scores(*pairs[n + ahead]))
        finish(*pair, *pending.pop(0))


def _na_attention(qkv, rpb, cvec, ada_w, b_late, *, n_batch, seq, ctx_len, n_heads, rb=4, hb=2, nsb=8):
    rows = seq // GRID_W
    assert rpb.shape == (n_heads, N_DR, N_DC) and _na_block_rows(rb) % 2 == 0 and rows % (rb * nsb) == 0
    hw = hb * HEAD_DIM
    hblocks = n_heads // hb
    qrows = nsb * rb * GRID_W
    n_r = rows // (rb * nsb)
    depth, d, n_mod = ada_w.shape
    n_late = b_late.shape[1] // ADA_SLAB
    n_early = depth * n_mod // ADA_SLAB - n_late
    assert n_late <= hblocks * n_batch * n_r
    step_of = lambda h, b, r, _: jnp.minimum((h * n_batch + b) * n_r + r, n_late - 1)
    ada_in_specs, ada_out_spec = _ada_slab_specs(d, n_early, n_mod // ADA_SLAB, step_of)
    lat_spec = lambda part: pl.BlockSpec((seq, hw), lambda h, b, r, _: (b, part * hblocks + h))
    ctx_spec = lambda part: pl.BlockSpec((ctx_len, hw),
                                         lambda h, b, r, _: (n_batch * seq // ctx_len + b, part * hblocks + h))
    q_spec = pl.BlockSpec((qrows, hw), lambda h, b, r, _: (b * (seq // qrows) + r, h))
    return pl.pallas_call(
        functools.partial(_na_kernel, rb=rb, hb=hb, nsb=nsb, rows=rows, n_ada=n_late),
        grid_spec=pltpu.PrefetchScalarGridSpec(
            num_scalar_prefetch=1,
            grid=(hblocks, n_batch, n_r),
            in_specs=[q_spec, lat_spec(1), lat_spec(2), ctx_spec(1), ctx_spec(2)] + ada_in_specs,
            out_specs=[q_spec, ada_out_spec],
            scratch_shapes=[pltpu.VMEM((hb, _TAB_SIZE, GRID_W, 2 * GRID_W), F32)],
        ),
        out_shape=[jax.ShapeDtypeStruct((n_batch * seq, n_heads * HEAD_DIM), BF16),
                   jax.ShapeDtypeStruct((MOD_ROWS, b_late.shape[1]), F32)],
        compiler_params=_params("arbitrary", "arbitrary", "arbitrary"),
        name="na_attention",
    )(rpb.reshape(-1), qkv, qkv, qkv, qkv, qkv, cvec, ada_w, b_late)


def _ctx_attn_kernel(q_ref, k_ref, v_ref, o_ref, *, hb):
    for hh in range(hb):
        cols = slice(hh * HEAD_DIM, (hh + 1) * HEAD_DIM)
        s = lax.dot_general(q_ref[:, cols], k_ref[:, cols], _NT, preferred_element_type=F32)
        p = jnp.exp2(s - jnp.max(s, axis=-1, keepdims=True))
        l = jnp.sum(p, axis=-1, keepdims=True)
        o = jnp.dot(p.astype(BF16), v_ref[:, cols], preferred_element_type=F32)
        o_ref[:, cols] = (o / l).astype(o_ref.dtype)


def _ctx_attention(qkv, *, n_batch, seq, ctx_len, n_heads, hb=8):
    hw = hb * HEAD_DIM
    hblocks = n_heads // hb
    row0 = n_batch * seq // ctx_len
    spec = lambda part: pl.BlockSpec((ctx_len, hw), lambda b, h: (row0 + b, part * hblocks + h))
    return pl.pallas_call(
        functools.partial(_ctx_attn_kernel, hb=hb),
        grid=(n_batch, hblocks),
        in_specs=[spec(0), spec(1), spec(2)],
        out_specs=pl.BlockSpec((ctx_len, hw), lambda b, h: (b, h)),
        out_shape=jax.ShapeDtypeStruct((n_batch * ctx_len, n_heads * HEAD_DIM), qkv.dtype),
        compiler_params=_params("arbitrary", "arbitrary"),
        name="ctx_attention",
    )(qkv, qkv, qkv)


def _gqa_kernel(q_ref, k_ref, vt_ref, kc_ref, vtc_ref, o_ref, m_scr, l_scr, acc_scr, s0_scr, s1_scr, x0_scr, x1_scr):
    n_chunks, _, tk = vt_ref.shape
    m_scr[...] = jnp.full(m_scr.shape, NEG_INF, F32)
    l_scr[...] = jnp.zeros(l_scr.shape, F32)
    acc_scr[...] = jnp.zeros(acc_scr.shape, F32)

    def scores(k, g):
        s = jnp.dot(k, q_ref[g * HEAD_DIM:(g + 1) * HEAD_DIM, :], preferred_element_type=F32)
        return s, jnp.max(s, axis=0, keepdims=True)

    def accumulate(s, s_max, vt, g):
        m_old = m_scr[g]
        m_new = jnp.maximum(m_old, s_max)
        alpha = jnp.exp2(m_old - m_new)
        p = jnp.exp2(s - m_new)
        l_scr[g] = alpha * l_scr[g] + jnp.sum(p, axis=0, keepdims=True)
        acc_scr[g] = alpha * acc_scr[g] + jnp.dot(vt, p.astype(BF16), preferred_element_type=F32)
        m_scr[g] = m_new

    def k_chunk(c):
        return k_ref[pl.ds(pl.multiple_of(c * tk, tk), tk), :]

    def stage(cur, nxt, c):
        k_next = k_chunk(c + 1)
        vt = vt_ref[c]
        for g in range(GQA_GROUP):
            nxt[0][g], nxt[1][g] = scores(k_next, g)
            accumulate(cur[0][g], cur[1][g], vt, g)

    buf0, buf1 = (s0_scr, x0_scr), (s1_scr, x1_scr)
    for g in range(GQA_GROUP):
        s0_scr[g], x0_scr[g] = scores(k_chunk(0), g)

    def body(j, carry):
        stage(buf0, buf1, 2 * j)
        stage(buf1, buf0, 2 * j + 1)
        return carry

    lax.fori_loop(0, n_chunks // 2 - 1, body, 0)
    stage(buf0, buf1, n_chunks - 2)
    vt_last = vt_ref[n_chunks - 1]
    for g in range(GQA_GROUP):
        s_ctx, x_ctx = scores(kc_ref[...], g)
        accumulate(s1_scr[g], x1_scr[g], vt_last, g)
        accumulate(s_ctx, x_ctx, vtc_ref[...], g)
    for g in range(GQA_GROUP):
        o_ref[:, g * HEAD_DIM:(g + 1) * HEAD_DIM] = (acc_scr[g] / l_scr[g]).T.astype(o_ref.dtype)


def _gqa_attention(qt, k, vt, *, n_batch, seq, ctx_len):
    n_kv = k.shape[1] // HEAD_DIM
    tq = qt.shape[2]
    tk = vt.shape[2]
    assert (seq // tk) % 2 == 0 and seq % tq == 0
    gw = GQA_GROUP * HEAD_DIM
    n_qt = seq // tq
    q_spec = pl.BlockSpec((tq, gw), lambda b, h, i: (b * n_qt + i, h))
    return pl.pallas_call(
        _gqa_kernel,
        grid=(n_batch, n_kv, n_qt),
        in_specs=[
            pl.BlockSpec((None, gw, tq), lambda b, h, i: (b * n_qt + i, h, 0)),
            pl.BlockSpec((seq, HEAD_DIM), lambda b, h, i: (b, h)),
            pl.BlockSpec((seq // tk, HEAD_DIM, tk), lambda b, h, i: (b, h, 0)),
            pl.BlockSpec((ctx_len, HEAD_DIM), lambda b, h, i: (n_batch * seq // ctx_len + b, h)),
            pl.BlockSpec((None, HEAD_DIM, ctx_len), lambda b, h, i: (n_batch * seq // tk, h, b)),
        ],
        out_specs=q_spec,
        out_shape=jax.ShapeDtypeStruct((n_batch * seq, qt.shape[1]), BF16),
        scratch_shapes=[
            pltpu.VMEM((GQA_GROUP, 1, tq), F32),
            pltpu.VMEM((GQA_GROUP, 1, tq), F32),
            pltpu.VMEM((GQA_GROUP, HEAD_DIM, tq), F32),
            pltpu.VMEM((GQA_GROUP, tk, tq), F32),
            pltpu.VMEM((GQA_GROUP, tk, tq), F32),
            pltpu.VMEM((GQA_GROUP, 1, tq), F32),
            pltpu.VMEM((GQA_GROUP, 1, tq), F32),
        ],
        compiler_params=_params("arbitrary", "arbitrary", "arbitrary"),
        name="gqa_attention",
    )(qt, k, vt, k, vt)


def _rope_partner(v):
    q4 = HEAD_DIM // 4
    return jnp.concatenate([v[..., q4:2 * q4], v[..., :q4], v[..., 3 * q4:], v[..., 2 * q4:3 * q4]], axis=-1)


def _rope_tables_t(cos, sin, g, scale, rows, tm):
    cos_t = (cos[:rows] * (g * scale)[None, :]).reshape(rows // tm, tm, HEAD_DIM)
    sin_t = (sin[:rows] * (_rope_partner(g) * scale)[None, :]).reshape(rows // tm, tm, HEAD_DIM)
    return jnp.transpose(cos_t, (0, 2, 1)), jnp.transpose(sin_t, (0, 2, 1))


def _rope_tables(n_batch, seq, ctx_len):
    quarter = HEAD_DIM // 4
    t = jnp.arange(seq)
    freqs = ROPE_THETA ** (-jnp.arange(quarter, dtype=F32) / quarter)
    ang_r = (t // GRID_W).astype(F32)[:, None] * freqs[None, :]
    ang_c = (t % GRID_W).astype(F32)[:, None] * freqs[None, :]
    cos = jnp.concatenate([jnp.cos(ang_r)] * 2 + [jnp.cos(ang_c)] * 2, axis=-1)
    sin = jnp.concatenate([-jnp.sin(ang_r), jnp.sin(ang_r), -jnp.sin(ang_c), jnp.sin(ang_c)], axis=-1)
    n_ctx = n_batch * ctx_len
    cos = jnp.concatenate([jnp.tile(cos, (n_batch, 1)), jnp.ones((n_ctx, HEAD_DIM), F32)], axis=0)
    sin = jnp.concatenate([jnp.tile(sin, (n_batch, 1)), jnp.zeros((n_ctx, HEAD_DIM), F32)], axis=0)
    return cos, sin


def kernel(x, c, ctx, c_ctx, ada_w, ada_b, norm_g, na_wqkv, na_wo, na_rpb, gqa_wq, gqa_wkv, gqa_q_norm,
           gqa_k_norm, gqa_wo, ffn_w13, ffn_w2):
    n_batch, seq, d = x.shape
    ctx_len = ctx.shape[1]
    depth = ada_w.shape[0]
    assert depth == 2 and na_wqkv.shape[0] == 1 and gqa_wq.shape[0] == 1
    assert seq % GRID_W == 0 and n_batch + 1 <= MOD_ROWS
    n_heads = d // HEAD_DIM
    m_lat = n_batch * seq
    m_all = m_lat + n_batch * ctx_len
    scale = HEAD_DIM ** -0.5 * LOG2E
    tm_all = m_all // 8
    tm_lat = m_lat // 8
    tn_f32, tn_b16, tn_gate, tn_down = 2 * MXU_COLS, 4 * MXU_COLS, MXU_COLS, 2 * MXU_COLS
    gqa_tk = 2 * MXU_COLS
    seg_tiles = lambda t: seq // t
    assert tm_all % 32 == 0 and tm_lat % 32 == 0

    cvec = jnp.zeros((MOD_ROWS, d), F32).at[:n_batch].set(c).at[n_batch].set(c_ctx)
    n_mod_early = 2 * d
    b_flat = ada_b.reshape(1, depth * N_MOD * d)
    mod_early = _ada_early(cvec, ada_w, b_flat[:, :n_mod_early])
    cos, sin = _rope_tables(n_batch, seq, ctx_len)
    cos_qt, sin_qt = _rope_tables_t(cos, sin, gqa_q_norm[0], scale, m_lat, tm_lat)

    x_lat = x.reshape(m_lat, d)
    x_ctx = ctx.reshape(n_batch * ctx_len, d)
    seg_kw = dict(seg_tiles=seg_tiles, n_batch=n_batch)

    h = _prenorm(x_lat, x_ctx, norm_g[0], mod_early.reshape(MOD_ROWS, 2, d), g_row=0, sh_row=0, sc_row=1, **seg_kw)
    qkv, (na_wo_b, wq, wkv, gqa_wo_b) = _matmul(
        h, na_wqkv[0], rows=m_all, n_out=na_wqkv.shape[2], out_dtype=BF16, tm=tm_all, tn=tn_f32,
        scale_blocks=d // tn_f32, scale=scale, side_casts=(na_wo[0], gqa_wq[0], gqa_wkv[0], gqa_wo[0]), name="na_qkv")
    o, mod_late = _na_attention(qkv, na_rpb[0], cvec, ada_w, b_flat[:, n_mod_early:],
                                n_batch=n_batch, seq=seq, ctx_len=ctx_len, n_heads=n_heads)
    mod = jnp.concatenate([mod_early, mod_late], axis=1).reshape(MOD_ROWS, depth, N_MOD, d)
    mod = [mod[:, i] for i in range(depth)]
    o_ctx = _ctx_attention(qkv, n_batch=n_batch, seq=seq, ctx_len=ctx_len, n_heads=n_heads)
    y = _matmul_two_src(o, o_ctx, na_wo_b, out_dtype=F32, tm=n_batch * ctx_len, tn=tn_b16, name="na_wo")
    xa, h = _resid(y, x_lat, norm_g[0], mod[0], x_ctx=x_ctx, rows=m_all, gt_row=2, gpost_row=1,
                   nxt=(2, 3, 4), g2=norm_g[0], mod2=mod[0], **seg_kw)
    gu, w2 = _gateup(h, ffn_w13, ffn_w2, 0, rows=m_all, tm=2 * tm_all, tn=tn_gate)
    y = _matmul(gu, w2, rows=m_all, n_out=d, out_dtype=F32, tm=tm_all // 2, tn=tn_down, name="ffn_down")
    xa, h = _resid(y, xa, norm_g[0], mod[0], rows=m_all, gt_row=5, gpost_row=3,
                   nxt=(0, 0, 1), g2=norm_g[1], mod2=mod[1], **seg_kw)

    kv_w = wkv.shape[1] // 2
    qt = _matmul_rope_t(h, wq, cos_qt, sin_qt, rows=m_lat, n_out=d, tn=tn_b16, name="gqa_q")
    k = _matmul_rope(h, wkv, gqa_k_norm[0], cos, sin, rows=m_all, n_out=kv_w, tm=tm_all, tn=kv_w, scale=1.0, name="gqa_k")
    assert seq % gqa_tk == 0 and n_batch * ctx_len == gqa_tk
    vt = _matmul_t(h, wkv, rows=m_all, n_out=kv_w, tm=gqa_tk, tn=kv_w, col_blk_off=1, name="gqa_v")
    o = _gqa_attention(qt, k, vt, n_batch=n_batch, seq=seq, ctx_len=ctx_len)
    y = _matmul(o, gqa_wo_b, rows=m_lat, n_out=d, out_dtype=F32, tm=tm_lat, tn=tn_b16, name="gqa_wo")
    xl, h = _resid(y, xa, norm_g[1], mod[1], rows=m_lat, gt_row=2, gpost_row=1,
                   nxt=(2, 3, 4), g2=norm_g[1], mod2=mod[1], **seg_kw)
    gu, w2 = _gateup(h, ffn_w13, ffn_w2, 1, rows=m_lat, tm=2 * tm_lat, tn=tn_gate)
    y = _matmul(gu, w2, rows=m_lat, n_out=d, out_dtype=F32, tm=tm_lat // 2, tn=tn_down, name="ffn_down")
    xl = _resid(y, xl, norm_g[1], mod[1], rows=m_lat, gt_row=5, gpost_row=3, **seg_kw)
    return xl.reshape(n_batch, seq, d)
```

```python
import functools

import jax
import jax.numpy as jnp
from jax import lax
from jax.experimental import pallas as pl
from jax.experimental.pallas import tpu as pltpu

GRID_W = 64
NA_WIN_H = 8
NA_WIN_W = 16
HEAD_DIM = 128
GQA_GROUP = 4
ROPE_THETA = 10000.0
NORM_EPS = 1e-6
NEG_INF = -1e30
LOG2E = 1.4426950408889634
N_MOD = 6

VMEM_LIMIT_BYTES = 56 * 1024 * 1024
MXU_COLS = 256
MOD_ROWS = 8

F32 = jnp.float32
BF16 = jnp.bfloat16


def _params(*sem):
    return pltpu.CompilerParams(dimension_semantics=sem, vmem_limit_bytes=VMEM_LIMIT_BYTES)


def _rms(x, g):
    ms = jnp.mean(x * x, axis=-1, keepdims=True)
    return x * lax.rsqrt(ms + NORM_EPS) * g


def _seg_index(rows_per_seg_tiles, n_batch):
    return lambda i: jnp.minimum(i // rows_per_seg_tiles, n_batch)


ADA_SLAB = 1024


def _ada_slab(c_ref, w_ref, b_ref, o_ref):
    c = c_ref[...]
    s = (c * jax.nn.sigmoid(c)).astype(BF16)
    o_ref[...] = jnp.dot(s, w_ref[...].astype(BF16), preferred_element_type=F32) + b_ref[...]


def _ada_slab_specs(d, n_early, per_layer, step_of):
    def w_map(*idx):
        f = step_of(*idx) + n_early
        return f // per_layer, 0, f % per_layer
    flat = lambda *idx: (0, step_of(*idx))
    return [pl.BlockSpec((MOD_ROWS, d), lambda *idx: (0, 0)),
            pl.BlockSpec((None, d, ADA_SLAB), w_map),
            pl.BlockSpec((1, ADA_SLAB), flat)], pl.BlockSpec((MOD_ROWS, ADA_SLAB), flat)


def _ada_early(cvec, ada_w, b_early):
    d = ada_w.shape[1]
    n = b_early.shape[1]
    in_specs, out_spec = _ada_slab_specs(d, 0, ada_w.shape[2] // ADA_SLAB, lambda j: j)
    return pl.pallas_call(
        _ada_slab,
        grid=(n // ADA_SLAB,),
        in_specs=in_specs,
        out_specs=out_spec,
        out_shape=jax.ShapeDtypeStruct((MOD_ROWS, n), F32),
        compiler_params=_params("arbitrary"),
        name="ada_mod",
    )(cvec, ada_w, b_early)


def _token_rows(x_lat, x_ctx, rows, tm, lat_buffers=2):
    d = x_lat.shape[1]
    n_lat = min(rows, x_lat.shape[0]) // tm
    if x_ctx is None:
        x_ctx = x_lat
        assert rows <= x_lat.shape[0]
    else:
        assert x_lat.shape[0] % tm == 0 and rows == x_lat.shape[0] + x_ctx.shape[0]
    specs = [pl.BlockSpec((tm, d), lambda i, *_: (jnp.minimum(i, n_lat - 1), 0), pipeline_mode=pl.Buffered(lat_buffers)),
             pl.BlockSpec((tm, d), lambda i, *_: (jnp.maximum(i - n_lat, 0), 0))]
    return [x_lat, x_ctx], specs, n_lat


def _read_token_rows(xl_ref, xc_ref, n_lat):
    return jnp.where(pl.program_id(0) < n_lat, xl_ref[...], xc_ref[...])


def _prenorm_kernel(xl_ref, xc_ref, g_ref, mod_ref, h_ref, *, n_lat, g_row, sh_row, sc_row):
    y = _rms(_read_token_rows(xl_ref, xc_ref, n_lat), g_ref[g_row:g_row + 1, :])
    h = y * (1.0 + mod_ref[sc_row:sc_row + 1, :]) + mod_ref[sh_row:sh_row + 1, :]
    h_ref[...] = h.astype(h_ref.dtype)


def _prenorm(x_lat, x_ctx, g, mod, *, seg_tiles, n_batch, g_row, sh_row, sc_row, tm=512):
    d = x_lat.shape[1]
    m = x_lat.shape[0] + x_ctx.shape[0]
    seg = _seg_index(seg_tiles(tm), n_batch)
    x_args, x_specs, n_lat = _token_rows(x_lat, x_ctx, m, tm)
    return pl.pallas_call(
        functools.partial(_prenorm_kernel, n_lat=n_lat, g_row=g_row, sh_row=sh_row, sc_row=sc_row),
        grid=(m // tm,),
        in_specs=x_specs + [
            pl.BlockSpec(g.shape, lambda i: (0, 0)),
            pl.BlockSpec((None, mod.shape[1], d), lambda i: (seg(i), 0, 0)),
        ],
        out_specs=pl.BlockSpec((tm, d), lambda i: (i, 0)),
        out_shape=jax.ShapeDtypeStruct((m, d), BF16),
        compiler_params=_params("arbitrary"),
        name="prenorm",
    )(*x_args, g, mod)


def _resid_kernel(y_ref, xl_ref, xc_ref, g_ref, mod_ref, *rest, n_lat, gt_row, gpost_row, nxt):
    x = _read_token_rows(xl_ref, xc_ref, n_lat)
    xn = x + mod_ref[gt_row:gt_row + 1, :] * _rms(y_ref[...], g_ref[gpost_row:gpost_row + 1, :])
    if nxt is None:
        (xo_ref,) = rest
        xo_ref[...] = xn
        return
    g2_ref, mod2_ref, xo_ref, h_ref = rest
    gpre_row, sh_row, sc_row = nxt
    xo_ref[...] = xn
    h = _rms(xn, g2_ref[gpre_row:gpre_row + 1, :])
    h = h * (1.0 + mod2_ref[sc_row:sc_row + 1, :]) + mod2_ref[sh_row:sh_row + 1, :]
    h_ref[...] = h.astype(h_ref.dtype)


def _resid(y, x, g, mod, *, rows, seg_tiles, n_batch, gt_row, gpost_row, x_ctx=None, nxt=None, g2=None, mod2=None,
           tm=256):
    d = x.shape[1]
    seg = _seg_index(seg_tiles(tm), n_batch)
    row_spec = pl.BlockSpec((tm, d), lambda i: (i, 0))
    mod_spec = pl.BlockSpec((None, N_MOD, d), lambda i: (seg(i), 0, 0))
    x_args, x_specs, n_lat = _token_rows(x, x_ctx, rows, tm)
    in_specs = [row_spec] + x_specs + [pl.BlockSpec(g.shape, lambda i: (0, 0)), mod_spec]
    args = [y] + x_args + [g, mod]
    out_specs = [row_spec]
    out_shape = [jax.ShapeDtypeStruct((rows, d), F32)]
    if nxt is not None:
        in_specs += [pl.BlockSpec(g2.shape, lambda i: (0, 0)), mod_spec]
        args += [g2, mod2]
        out_specs.append(row_spec)
        out_shape.append(jax.ShapeDtypeStruct((rows, d), BF16))
    out = pl.pallas_call(
        functools.partial(_resid_kernel, n_lat=n_lat, gt_row=gt_row, gpost_row=gpost_row, nxt=nxt),
        grid=(rows // tm,),
        in_specs=in_specs,
        out_specs=out_specs,
        out_shape=out_shape,
        compiler_params=_params("arbitrary"),
        name="resid_norm",
    )(*args)
    return out if nxt is not None else out[0]


def _resid_final_piped(y, x, g, mod, *, rows, seg_tiles, n_batch, gt_row, gpost_row, tm=256, n_buf=3):
    d = x.shape[1]
    seg = _seg_index(seg_tiles(tm), n_batch)

    def body(y_ref, x_ref, g_ref, mod_ref, xo_ref):
        gate = mod_ref[0, gt_row:gt_row + 1, :]
        xo_ref[...] = x_ref[...] + gate * _rms(y_ref[...], g_ref[gpost_row:gpost_row + 1, :])

    def outer(y_hbm, x_hbm, g_hbm, mod_hbm, xo_hbm):
        stream = pl.BlockSpec((tm, d), lambda i: (i, 0), pipeline_mode=pl.Buffered(n_buf))
        pltpu.emit_pipeline(
            body,
            grid=(rows // tm,),
            in_specs=[stream, stream, pl.BlockSpec(g.shape, lambda i: (0, 0)),
                      pl.BlockSpec((1, N_MOD, d), lambda i: (seg(i), 0, 0))],
            out_specs=[pl.BlockSpec((tm, d), lambda i: (i, 0))],
        )(y_hbm, x_hbm, g_hbm, mod_hbm, xo_hbm)

    return pl.pallas_call(
        outer,
        in_specs=[pl.BlockSpec(memory_space=pl.ANY)] * 4,
        out_specs=pl.BlockSpec(memory_space=pl.ANY),
        out_shape=jax.ShapeDtypeStruct((rows, d), F32),
        compiler_params=pltpu.CompilerParams(vmem_limit_bytes=VMEM_LIMIT_BYTES),
        name="resid_final_piped",
    )(y, x, g, mod)


SIDE_CAST_ROWS = 32


def _mm_kernel(x_ref, w_ref, *rest, scale_blocks, scale, n_side):
    side_in, o_ref, side_out = rest[:n_side], rest[n_side], rest[n_side + 1:]
    acc = jnp.dot(x_ref[...], w_ref[...].astype(BF16), preferred_element_type=F32)
    if scale_blocks:
        acc = acc * jnp.where(pl.program_id(1) < scale_blocks, scale, 1.0)
    o_ref[...] = acc.astype(o_ref.dtype)
    for src, dst in zip(side_in, side_out):
        dst[...] = src[...].astype(dst.dtype)


def _matmul(x, w, *, rows, n_out, out_dtype, tm, tn, col_blk_off=0, scale_blocks=0, scale=1.0, side_casts=(),
            name="matmul"):
    k = x.shape[1]
    n_j = n_out // tn
    n_steps = (rows // tm) * n_j
    in_specs = [
        pl.BlockSpec((tm, k), lambda i, j: (i, 0)),
        pl.BlockSpec((k, tn), lambda i, j: (0, j + col_blk_off)),
    ]
    out_specs = [pl.BlockSpec((tm, tn), lambda i, j: (i, j))]
    out_shape = [jax.ShapeDtypeStruct((rows, n_out), out_dtype)]
    for ws in side_casts:
        n_slabs = ws.shape[0] // SIDE_CAST_ROWS
        assert n_slabs * SIDE_CAST_ROWS == ws.shape[0] and n_slabs <= n_steps
        spec = pl.BlockSpec((SIDE_CAST_ROWS, ws.shape[1]),
                            lambda i, j, n_slabs=n_slabs: (jnp.minimum(i * n_j + j, n_slabs - 1), 0))
        in_specs.append(spec)
        out_specs.append(spec)
        out_shape.append(jax.ShapeDtypeStruct(ws.shape, BF16))
    out = pl.pallas_call(
        functools.partial(_mm_kernel, scale_blocks=scale_blocks, scale=scale, n_side=len(side_casts)),
        grid=(rows // tm, n_j),
        in_specs=in_specs,
        out_specs=out_specs,
        out_shape=out_shape,
        compiler_params=_params("arbitrary", "arbitrary"),
        name=name,
    )(x, w, *side_casts)
    return (out[0], out[1:]) if side_casts else out[0]


def _mm_two_src_kernel(xl_ref, xc_ref, w_ref, o_ref, *, n_lat):
    x = _read_token_rows(xl_ref, xc_ref, n_lat)
    o_ref[...] = jnp.dot(x, w_ref[...], preferred_element_type=F32).astype(o_ref.dtype)


def _matmul_two_src(x_lat, x_ctx, w, *, out_dtype, tm, tn, name):
    rows = x_lat.shape[0] + x_ctx.shape[0]
    k, n_out = w.shape
    x_args, x_specs, n_lat = _token_rows(x_lat, x_ctx, rows, tm)
    return pl.pallas_call(
        functools.partial(_mm_two_src_kernel, n_lat=n_lat),
        grid=(rows // tm, n_out // tn),
        in_specs=x_specs + [pl.BlockSpec((k, tn), lambda i, j: (0, j))],
        out_specs=pl.BlockSpec((tm, tn), lambda i, j: (i, j)),
        out_shape=jax.ShapeDtypeStruct((rows, n_out), out_dtype),
        compiler_params=_params("arbitrary", "arbitrary"),
        name=name,
    )(*x_args, w)


def _mm_t_kernel(x_ref, w_ref, o_ref):
    acc = jnp.dot(x_ref[...], w_ref[...].astype(BF16), preferred_element_type=F32)
    o_ref[...] = acc.T.astype(o_ref.dtype)


def _matmul_t(x, w, *, rows, n_out, tm, tn, col_blk_off=0, name="matmul_t"):
    k = x.shape[1]
    return pl.pallas_call(
        _mm_t_kernel,
        grid=(n_out // tn, rows // tm),
        in_specs=[
            pl.BlockSpec((tm, k), lambda j, i: (i, 0)),
            pl.BlockSpec((k, tn), lambda j, i: (0, j + col_blk_off)),
        ],
        out_specs=pl.BlockSpec((None, tn, tm), lambda j, i: (i, j, 0)),
        out_shape=jax.ShapeDtypeStruct((rows // tm, n_out, tm), BF16),
        compiler_params=_params("arbitrary", "arbitrary"),
        name=name,
    )(x, w)


def _swap_halves(y):
    q4 = HEAD_DIM // 4
    lane = lax.broadcasted_iota(jnp.int32, y.shape, 1)
    return jnp.where((lane & q4) == 0, pltpu.roll(y, HEAD_DIM - q4, 1), pltpu.roll(y, q4, 1))


def _mm_rope_kernel(x_ref, w_ref, g_ref, cos_ref, sin_ref, o_ref, *, scale):
    cos = cos_ref[...]
    sin = sin_ref[...]
    g = g_ref[...]
    x = x_ref[...]
    for grp in range(w_ref.shape[1] // MXU_COLS):
        w = w_ref[:, grp * MXU_COLS:(grp + 1) * MXU_COLS].astype(BF16)
        acc = jnp.dot(x, w, preferred_element_type=F32)
        for hh in range(MXU_COLS // HEAD_DIM):
            y = _rms(acc[:, hh * HEAD_DIM:(hh + 1) * HEAD_DIM], g)
            y = y * cos + _swap_halves(y) * sin
            if scale != 1.0:
                y = y * scale
            col0 = grp * MXU_COLS + hh * HEAD_DIM
            o_ref[:, col0:col0 + HEAD_DIM] = y.astype(o_ref.dtype)


def _matmul_rope(x, w, g, cos, sin, *, rows, n_out, tm, tn, scale, name):
    k = x.shape[1]
    tab_spec = pl.BlockSpec((tm, HEAD_DIM), lambda i, j: (i, 0))
    return pl.pallas_call(
        functools.partial(_mm_rope_kernel, scale=scale),
        grid=(rows // tm, n_out // tn),
        in_specs=[
            pl.BlockSpec((tm, k), lambda i, j: (i, 0)),
            pl.BlockSpec((k, tn), lambda i, j: (0, j)),
            pl.BlockSpec((1, HEAD_DIM), lambda i, j: (0, 0)),
            tab_spec,
            tab_spec,
        ],
        out_specs=pl.BlockSpec((tm, tn), lambda i, j: (i, j)),
        out_shape=jax.ShapeDtypeStruct((rows, n_out), BF16),
        compiler_params=_params("arbitrary", "arbitrary"),
        name=name,
    )(x, w, g.reshape(1, HEAD_DIM), cos, sin)


def _mm_rope_t_kernel(x_ref, w_ref, cos_ref, sin_ref, o_ref):
    cos = cos_ref[...]
    sin = sin_ref[...]
    x = x_ref[...]
    q4 = HEAD_DIM // 4
    for grp in range(w_ref.shape[1] // MXU_COLS):
        w = w_ref[:, grp * MXU_COLS:(grp + 1) * MXU_COLS].astype(BF16)
        acc_t = jnp.dot(x, w, preferred_element_type=F32).T
        for hh in range(MXU_COLS // HEAD_DIM):
            y = acc_t[hh * HEAD_DIM:(hh + 1) * HEAD_DIM, :]
            r = lax.rsqrt(jnp.mean(y * y, axis=0, keepdims=True) + NORM_EPS)
            partner = jnp.concatenate([y[q4:2 * q4], y[:q4], y[3 * q4:], y[2 * q4:3 * q4]], axis=0)
            row0 = grp * MXU_COLS + hh * HEAD_DIM
            o_ref[row0:row0 + HEAD_DIM, :] = ((y * cos + partner * sin) * r).astype(o_ref.dtype)


def _matmul_rope_t(x, w, cos_t, sin_t, *, rows, n_out, tn, name):
    k = x.shape[1]
    n_tiles, _, tm = cos_t.shape
    assert n_tiles * tm == rows
    tab_spec = pl.BlockSpec((None, HEAD_DIM, tm), lambda i, j: (i, 0, 0))
    return pl.pallas_call(
        _mm_rope_t_kernel,
        grid=(n_tiles, n_out // tn),
        in_specs=[
            pl.BlockSpec((tm, k), lambda i, j: (i, 0)),
            pl.BlockSpec((k, tn), lambda i, j: (0, j)),
            tab_spec,
            tab_spec,
        ],
        out_specs=pl.BlockSpec((None, tn, tm), lambda i, j: (i, j, 0)),
        out_shape=jax.ShapeDtypeStruct((n_tiles, n_out, tm), BF16),
        compiler_params=_params("arbitrary", "arbitrary"),
        name=name,
    )(x, w, cos_t, sin_t)


def _gateup_kernel(x_ref, w1_ref, w3_ref, w2_ref, o_ref, w2b_ref):
    x = x_ref[...]
    a = jnp.dot(x, w1_ref[...].astype(BF16), preferred_element_type=F32)
    b = jnp.dot(x, w3_ref[...].astype(BF16), preferred_element_type=F32)
    o_ref[...] = (a * jax.nn.sigmoid(a) * b).astype(o_ref.dtype)
    w2b_ref[...] = w2_ref[...].astype(w2b_ref.dtype)


def _gateup(x, w13, w2, layer, *, rows, tm, tn):
    k = x.shape[1]
    f = w13.shape[2] // 2
    n_j = f // tn
    steps = (rows // tm) * n_j
    slab = w2.shape[1] // steps
    assert slab * steps == w2.shape[1] and slab % 16 == 0
    return pl.pallas_call(
        _gateup_kernel,
        grid=(rows // tm, n_j),
        in_specs=[
            pl.BlockSpec((tm, k), lambda i, j: (i, 0), pipeline_mode=pl.Buffered(1)),
            pl.BlockSpec((None, k, tn), lambda i, j: (layer, 0, j)),
            pl.BlockSpec((None, k, tn), lambda i, j: (layer, 0, j + n_j)),
            pl.BlockSpec((None, slab, w2.shape[2]), lambda i, j: (layer, i * n_j + j, 0)),
        ],
        out_specs=[
            pl.BlockSpec((tm, tn), lambda i, j: (i, j)),
            pl.BlockSpec((slab, w2.shape[2]), lambda i, j: (i * n_j + j, 0)),
        ],
        out_shape=[
            jax.ShapeDtypeStruct((rows, f), BF16),
            jax.ShapeDtypeStruct(w2.shape[1:], BF16),
        ],
        compiler_params=_params("arbitrary", "arbitrary"),
        name="ffn_gateup",
    )(x, w13, w13, w2)


_NT = (((1,), (1,)), ((), ()))


def _na_block_rows(rb):
    return rb + NA_WIN_H


N_DR = 2 * NA_WIN_H - 1
N_DC = 2 * NA_WIN_W - 1
_TAB_BOTH, _TAB_FIRST, _TAB_SECOND, _TAB_NONE = 0, N_DR + 1, 2 * N_DR + 1, 3 * N_DR + 1
_TAB_SIZE = 3 * N_DR + 2


def _na_build_bias_tiles(rpb_ref, tab_scr, h0, hb):
    shape = (GRID_W, 2 * GRID_W)
    c = lax.broadcasted_iota(jnp.int32, shape, 0)
    lane = lax.broadcasted_iota(jnp.int32, shape, 1)
    kc = lane & (GRID_W - 1)
    second = lane >= GRID_W
    cs = jnp.clip(c - NA_WIN_W // 2, 0, GRID_W - NA_WIN_W)
    in_win = (kc >= cs) & (kc < cs + NA_WIN_W)
    dci = kc - c + (NA_WIN_W - 1)
    is_dc = [dci == k for k in range(N_DC)]
    neg = jnp.full(shape, NEG_INF, F32)
    for hh in range(hb):
        base = (h0 + hh) * (N_DR * N_DC)
        rows = []
        for d in range(N_DR):
            t = neg
            for k in range(N_DC):
                t = jnp.where(is_dc[k], rpb_ref[base + d * N_DC + k] * LOG2E, t)
            rows.append(jnp.where(in_win, t, NEG_INF))
        for d in range(N_DR + 1):
            lo = rows[d - 1] if d >= 1 else neg
            hi = rows[d] if d < N_DR else neg
            tab_scr[hh, _TAB_BOTH + d] = jnp.where(second, hi, lo)
        for d in range(N_DR):
            tab_scr[hh, _TAB_FIRST + d] = jnp.where(second, neg, rows[d])
            tab_scr[hh, _TAB_SECOND + d] = jnp.where(second, rows[d], neg)
        tab_scr[hh, _TAB_NONE] = neg


def _na_kernel(rpb_ref, q_ref, k_ref, v_ref, kc_ref, vc_ref, c_ref, aw_ref, ab_ref, o_ref, mod_ref, tab_scr, *,
               rb, hb, nsb, rows, n_ada):
    kr = _na_block_rows(rb)
    qn = rb * GRID_W
    half = NA_WIN_H // 2
    step = (pl.program_id(0) * pl.num_programs(1) + pl.program_id(1)) * pl.num_programs(2) + pl.program_id(2)

    @pl.when(step < n_ada)
    def _():
        _ada_slab(c_ref, aw_ref, ab_ref, mod_ref)

    @pl.when((pl.program_id(1) == 0) & (pl.program_id(2) == 0))
    def _():
        _na_build_bias_tiles(rpb_ref, tab_scr, pl.program_id(0) * hb, hb)

    starts, tiles = [], []
    for sb in range(nsb):
        r0 = (pl.program_id(2) * nsb + sb) * rb
        kstart = jnp.clip(r0 - half, 0, rows - kr)
        starts.append(pl.multiple_of(kstart * GRID_W, GRID_W))
        tile_idx = []
        for i in range(rb):
            r = r0 + i
            rs = jnp.clip(r - half, 0, rows - NA_WIN_H)
            row_idx = []
            for jp in range(kr // 2):
                k0 = kstart + 2 * jp
                v0 = (k0 >= rs) & (k0 < rs + NA_WIN_H)
                v1 = (k0 + 1 >= rs) & (k0 + 1 < rs + NA_WIN_H)
                d0 = k0 - r + NA_WIN_H - 1
                idx = jnp.where(v0 & v1, _TAB_BOTH + d0 + 1,
                                jnp.where(v0, _TAB_FIRST + d0, jnp.where(v1, _TAB_SECOND + d0 + 1, _TAB_NONE)))
                row_idx.append(jnp.clip(idx, 0, _TAB_SIZE - 1))
            tile_idx.append(row_idx)
        tiles.append(tile_idx)

    def scores(sb, hh):
        cols = slice(hh * HEAD_DIM, (hh + 1) * HEAD_DIM)
        q = q_ref[sb * qn:(sb + 1) * qn, cols]
        ku = k_ref[pl.ds(starts[sb], kr * GRID_W), cols]
        bias = jnp.concatenate(
            [jnp.concatenate([tab_scr[hh, idx] for idx in row_idx], axis=1) for row_idx in tiles[sb]], axis=0)
        s_loc = lax.dot_general(q, ku, _NT, preferred_element_type=F32) + bias
        s_ctx = lax.dot_general(q, kc_ref[:, cols], _NT, preferred_element_type=F32)
        return s_loc, s_ctx

    def finish(sb, hh, s_loc, s_ctx):
        cols = slice(hh * HEAD_DIM, (hh + 1) * HEAD_DIM)
        vu = v_ref[pl.ds(starts[sb], kr * GRID_W), cols]
        m = jnp.maximum(jnp.max(s_loc, axis=-1, keepdims=True), jnp.max(s_ctx, axis=-1, keepdims=True))
        p_loc = jnp.exp2(s_loc - m)
        p_ctx = jnp.exp2(s_ctx - m)
        l = jnp.sum(p_loc, axis=-1, keepdims=True) + jnp.sum(p_ctx, axis=-1, keepdims=True)
        o = (jnp.dot(p_loc.astype(BF16), vu, preferred_element_type=F32)
             + jnp.dot(p_ctx.astype(BF16), vc_ref[:, cols], preferred_element_type=F32))
        o_ref[sb * qn:(sb + 1) * qn, cols] = (o / l).astype(o_ref.dtype)

    pairs = [(sb, hh) for sb in range(nsb) for hh in range(hb)]
    ahead = 1
    pending = [scores(*pair) for pair in pairs[:ahead]]
    for n, pair in enumerate(pairs):
        if n + ahead < len(pairs):
            pending.append(scores(*pairs[n + ahead]))
        finish(*pair, *pending.pop(0))


def _na_attention(qkv, rpb, cvec, ada_w, b_late, *, n_batch, seq, ctx_len, n_heads, rb=4, hb=2, nsb=8):
    rows = seq // GRID_W
    assert rpb.shape == (n_heads, N_DR, N_DC) and _na_block_rows(rb) % 2 == 0 and rows % (rb * nsb) == 0
    hw = hb * HEAD_DIM
    hblocks = n_heads // hb
    qrows = nsb * rb * GRID_W
    n_r = rows // (rb * nsb)
    depth, d, n_mod = ada_w.shape
    n_late = b_late.shape[1] // ADA_SLAB
    n_early = depth * n_mod // ADA_SLAB - n_late
    assert n_late <= hblocks * n_batch * n_r
    step_of = lambda h, b, r, _: jnp.minimum((h * n_batch + b) * n_r + r, n_late - 1)
    ada_in_specs, ada_out_spec = _ada_slab_specs(d, n_early, n_mod // ADA_SLAB, step_of)
    lat_spec = lambda part: pl.BlockSpec((seq, hw), lambda h, b, r, _: (b, part * hblocks + h))
    ctx_spec = lambda part: pl.BlockSpec((ctx_len, hw),
                                         lambda h, b, r, _: (n_batch * seq // ctx_len + b, part * hblocks + h))
    q_spec = pl.BlockSpec((qrows, hw), lambda h, b, r, _: (b * (seq // qrows) + r, h))
    return pl.pallas_call(
        functools.partial(_na_kernel, rb=rb, hb=hb, nsb=nsb, rows=rows, n_ada=n_late),
        grid_spec=pltpu.PrefetchScalarGridSpec(
            num_scalar_prefetch=1,
            grid=(hblocks, n_batch, n_r),
            in_specs=[q_spec, lat_spec(1), lat_spec(2), ctx_spec(1), ctx_spec(2)] + ada_in_specs,
            out_specs=[q_spec, ada_out_spec],
            scratch_shapes=[pltpu.VMEM((hb, _TAB_SIZE, GRID_W, 2 * GRID_W), F32)],
        ),
        out_shape=[jax.ShapeDtypeStruct((n_batch * seq, n_heads * HEAD_DIM), BF16),
                   jax.ShapeDtypeStruct((MOD_ROWS, b_late.shape[1]), F32)],
        compiler_params=_params("arbitrary", "arbitrary", "arbitrary"),
        name="na_attention",
    )(rpb.reshape(-1), qkv, qkv, qkv, qkv, qkv, cvec, ada_w, b_late)


def _ctx_attn_kernel(q_ref, k_ref, v_ref, o_ref, *, hb):
    for hh in range(hb):
        cols = slice(hh * HEAD_DIM, (hh + 1) * HEAD_DIM)
        s = lax.dot_general(q_ref[:, cols], k_ref[:, cols], _NT, preferred_element_type=F32)
        p = jnp.exp2(s - jnp.max(s, axis=-1, keepdims=True))
        l = jnp.sum(p, axis=-1, keepdims=True)
        o = jnp.dot(p.astype(BF16), v_ref[:, cols], preferred_element_type=F32)
        o_ref[:, cols] = (o / l).astype(o_ref.dtype)


def _ctx_attention(qkv, *, n_batch, seq, ctx_len, n_heads, hb=8):
    hw = hb * HEAD_DIM
    hblocks = n_heads // hb
    row0 = n_batch * seq // ctx_len
    spec = lambda part: pl.BlockSpec((ctx_len, hw), lambda b, h: (row0 + b, part * hblocks + h))
    return pl.pallas_call(
        functools.partial(_ctx_attn_kernel, hb=hb),
        grid=(n_batch, hblocks),
        in_specs=[spec(0), spec(1), spec(2)],
        out_specs=pl.BlockSpec((ctx_len, hw), lambda b, h: (b, h)),
        out_shape=jax.ShapeDtypeStruct((n_batch * ctx_len, n_heads * HEAD_DIM), qkv.dtype),
        compiler_params=_params("arbitrary", "arbitrary"),
        name="ctx_attention",
    )(qkv, qkv, qkv)


def _gqa_kernel(q_ref, k_ref, vt_ref, kc_ref, vtc_ref, o_ref, m_scr, l_scr, acc_scr, s0_scr, s1_scr, x0_scr, x1_scr):
    n_chunks, _, tk = vt_ref.shape
    m_scr[...] = jnp.full(m_scr.shape, NEG_INF, F32)
    l_scr[...] = jnp.zeros(l_scr.shape, F32)
    acc_scr[...] = jnp.zeros(acc_scr.shape, F32)

    def scores(k, g):
        s = jnp.dot(k, q_ref[g * HEAD_DIM:(g + 1) * HEAD_DIM, :], preferred_element_type=F32)
        return s, jnp.max(s, axis=0, keepdims=True)

    def accumulate(s, s_max, vt, g):
        m_old = m_scr[g]
        m_new = jnp.maximum(m_old, s_max)
        alpha = jnp.exp2(m_old - m_new)
        p = jnp.exp2(s - m_new)
        l_scr[g] = alpha * l_scr[g] + jnp.sum(p, axis=0, keepdims=True)
        acc_scr[g] = alpha * acc_scr[g] + jnp.dot(vt, p.astype(BF16), preferred_element_type=F32)
        m_scr[g] = m_new

    def k_chunk(c):
        return k_ref[pl.ds(pl.multiple_of(c * tk, tk), tk), :]

    def stage(cur, nxt, c):
        k_next = k_chunk(c + 1)
        vt = vt_ref[c]
        for g in range(GQA_GROUP):
            nxt[0][g], nxt[1][g] = scores(k_next, g)
            accumulate(cur[0][g], cur[1][g], vt, g)

    buf0, buf1 = (s0_scr, x0_scr), (s1_scr, x1_scr)
    for g in range(GQA_GROUP):
        s0_scr[g], x0_scr[g] = scores(k_chunk(0), g)

    def body(j, carry):
        stage(buf0, buf1, 2 * j)
        stage(buf1, buf0, 2 * j + 1)
        return carry

    lax.fori_loop(0, n_chunks // 2 - 1, body, 0)
    stage(buf0, buf1, n_chunks - 2)
    vt_last = vt_ref[n_chunks - 1]
    for g in range(GQA_GROUP):
        s_ctx, x_ctx = scores(kc_ref[...], g)
        accumulate(s1_scr[g], x1_scr[g], vt_last, g)
        accumulate(s_ctx, x_ctx, vtc_ref[...], g)
    for g in range(GQA_GROUP):
        o_ref[:, g * HEAD_DIM:(g + 1) * HEAD_DIM] = (acc_scr[g] / l_scr[g]).T.astype(o_ref.dtype)


def _gqa_attention(qt, k, vt, *, n_batch, seq, ctx_len):
    n_kv = k.shape[1] // HEAD_DIM
    tq = qt.shape[2]
    tk = vt.shape[2]
    assert (seq // tk) % 2 == 0 and seq % tq == 0
    gw = GQA_GROUP * HEAD_DIM
    n_qt = seq // tq
    q_spec = pl.BlockSpec((tq, gw), lambda b, h, i: (b * n_qt + i, h))
    return pl.pallas_call(
        _gqa_kernel,
        grid=(n_batch, n_kv, n_qt),
        in_specs=[
            pl.BlockSpec((None, gw, tq), lambda b, h, i: (b * n_qt + i, h, 0)),
            pl.BlockSpec((seq, HEAD_DIM), lambda b, h, i: (b, h)),
            pl.BlockSpec((seq // tk, HEAD_DIM, tk), lambda b, h, i: (b, h, 0)),
            pl.BlockSpec((ctx_len, HEAD_DIM), lambda b, h, i: (n_batch * seq // ctx_len + b, h)),
            pl.BlockSpec((None, HEAD_DIM, ctx_len), lambda b, h, i: (n_batch * seq // tk, h, b)),
        ],
        out_specs=q_spec,
        out_shape=jax.ShapeDtypeStruct((n_batch * seq, qt.shape[1]), BF16),
        scratch_shapes=[
            pltpu.VMEM((GQA_GROUP, 1, tq), F32),
            pltpu.VMEM((GQA_GROUP, 1, tq), F32),
            pltpu.VMEM((GQA_GROUP, HEAD_DIM, tq), F32),
            pltpu.VMEM((GQA_GROUP, tk, tq), F32),
            pltpu.VMEM((GQA_GROUP, tk, tq), F32),
            pltpu.VMEM((GQA_GROUP, 1, tq), F32),
            pltpu.VMEM((GQA_GROUP, 1, tq), F32),
        ],
        compiler_params=_params("arbitrary", "arbitrary", "arbitrary"),
        name="gqa_attention",
    )(qt, k, vt, k, vt)


def _rope_partner(v):
    q4 = HEAD_DIM // 4
    return jnp.concatenate([v[..., q4:2 * q4], v[..., :q4], v[..., 3 * q4:], v[..., 2 * q4:3 * q4]], axis=-1)


def _rope_tables_t(cos, sin, g, scale, rows, tm):
    cos_t = (cos[:rows] * (g * scale)[None, :]).reshape(rows // tm, tm, HEAD_DIM)
    sin_t = (sin[:rows] * (_rope_partner(g) * scale)[None, :]).reshape(rows // tm, tm, HEAD_DIM)
    return jnp.transpose(cos_t, (0, 2, 1)), jnp.transpose(sin_t, (0, 2, 1))


def _rope_tables(n_batch, seq, ctx_len):
    quarter = HEAD_DIM // 4
    t = jnp.arange(seq)
    freqs = ROPE_THETA ** (-jnp.arange(quarter, dtype=F32) / quarter)
    ang_r = (t // GRID_W).astype(F32)[:, None] * freqs[None, :]
    ang_c = (t % GRID_W).astype(F32)[:, None] * freqs[None, :]
    cos = jnp.concatenate([jnp.cos(ang_r)] * 2 + [jnp.cos(ang_c)] * 2, axis=-1)
    sin = jnp.concatenate([-jnp.sin(ang_r), jnp.sin(ang_r), -jnp.sin(ang_c), jnp.sin(ang_c)], axis=-1)
    n_ctx = n_batch * ctx_len
    cos = jnp.concatenate([jnp.tile(cos, (n_batch, 1)), jnp.ones((n_ctx, HEAD_DIM), F32)], axis=0)
    sin = jnp.concatenate([jnp.tile(sin, (n_batch, 1)), jnp.zeros((n_ctx, HEAD_DIM), F32)], axis=0)
    return cos, sin


def kernel(x, c, ctx, c_ctx, ada_w, ada_b, norm_g, na_wqkv, na_wo, na_rpb, gqa_wq, gqa_wkv, gqa_q_norm,
           gqa_k_norm, gqa_wo, ffn_w13, ffn_w2):
    n_batch, seq, d = x.shape
    ctx_len = ctx.shape[1]
    depth = ada_w.shape[0]
    assert depth == 2 and na_wqkv.shape[0] == 1 and gqa_wq.shape[0] == 1
    assert seq % GRID_W == 0 and n_batch + 1 <= MOD_ROWS
    n_heads = d // HEAD_DIM
    m_lat = n_batch * seq
    m_all = m_lat + n_batch * ctx_len
    scale = HEAD_DIM ** -0.5 * LOG2E
    tm_all = m_all // 8
    tm_lat = m_lat // 8
    tn_f32, tn_b16, tn_gate, tn_down = 2 * MXU_COLS, 4 * MXU_COLS, MXU_COLS, 2 * MXU_COLS
    gqa_tk = 2 * MXU_COLS
    seg_tiles = lambda t: seq // t
    assert tm_all % 32 == 0 and tm_lat % 32 == 0

    cvec = jnp.zeros((MOD_ROWS, d), F32).at[:n_batch].set(c).at[n_batch].set(c_ctx)
    n_mod_early = 2 * d
    b_flat = ada_b.reshape(1, depth * N_MOD * d)
    mod_early = _ada_early(cvec, ada_w, b_flat[:, :n_mod_early])
    cos, sin = _rope_tables(n_batch, seq, ctx_len)
    cos_qt, sin_qt = _rope_tables_t(cos, sin, gqa_q_norm[0], scale, m_lat, tm_lat)

    x_lat = x.reshape(m_lat, d)
    x_ctx = ctx.reshape(n_batch * ctx_len, d)
    seg_kw = dict(seg_tiles=seg_tiles, n_batch=n_batch)

    h = _prenorm(x_lat, x_ctx, norm_g[0], mod_early.reshape(MOD_ROWS, 2, d), g_row=0, sh_row=0, sc_row=1, **seg_kw)
    qkv, (na_wo_b, wq, wkv, gqa_wo_b) = _matmul(
        h, na_wqkv[0], rows=m_all, n_out=na_wqkv.shape[2], out_dtype=BF16, tm=tm_all, tn=tn_f32,
        scale_blocks=d // tn_f32, scale=scale, side_casts=(na_wo[0], gqa_wq[0], gqa_wkv[0], gqa_wo[0]), name="na_qkv")
    o, mod_late = _na_attention(qkv, na_rpb[0], cvec, ada_w, b_flat[:, n_mod_early:],
                                n_batch=n_batch, seq=seq, ctx_len=ctx_len, n_heads=n_heads)
    mod = jnp.concatenate([mod_early, mod_late], axis=1).reshape(MOD_ROWS, depth, N_MOD, d)
    mod = [mod[:, i] for i in range(depth)]
    o_ctx = _ctx_attention(qkv, n_batch=n_batch, seq=seq, ctx_len=ctx_len, n_heads=n_heads)
    y = _matmul_two_src(o, o_ctx, na_wo_b, out_dtype=F32, tm=n_batch * ctx_len, tn=tn_b16, name="na_wo")
    xa, h = _resid(y, x_lat, norm_g[0], mod[0], x_ctx=x_ctx, rows=m_all, gt_row=2, gpost_row=1,
                   nxt=(2, 3, 4), g2=norm_g[0], mod2=mod[0], **seg_kw)
    gu, w2 = _gateup(h, ffn_w13, ffn_w2, 0, rows=m_all, tm=2 * tm_all, tn=tn_gate)
    y = _matmul(gu, w2, rows=m_all, n_out=d, out_dtype=F32, tm=tm_all // 2, tn=tn_down, name="ffn_down")
    xa, h = _resid(y, xa, norm_g[0], mod[0], rows=m_all, gt_row=5, gpost_row=3,
                   nxt=(0, 0, 1), g2=norm_g[1], mod2=mod[1], **seg_kw)

    kv_w = wkv.shape[1] // 2
    qt = _matmul_rope_t(h, wq, cos_qt, sin_qt, rows=m_lat, n_out=d, tn=tn_b16, name="gqa_q")
    k = _matmul_rope(h, wkv, gqa_k_norm[0], cos, sin, rows=m_all, n_out=kv_w, tm=tm_all, tn=kv_w, scale=1.0, name="gqa_k")
    assert seq % gqa_tk == 0 and n_batch * ctx_len == gqa_tk
    vt = _matmul_t(h, wkv, rows=m_all, n_out=kv_w, tm=gqa_tk, tn=kv_w, col_blk_off=1, name="gqa_v")
    o = _gqa_attention(qt, k, vt, n_batch=n_batch, seq=seq, ctx_len=ctx_len)
    y = _matmul(o, gqa_wo_b, rows=m_lat, n_out=d, out_dtype=F32, tm=tm_lat, tn=tn_b16, name="gqa_wo")
    xl, h = _resid(y, xa, norm_g[1], mod[1], rows=m_lat, gt_row=2, gpost_row=1,
                   nxt=(2, 3, 4), g2=norm_g[1], mod2=mod[1], **seg_kw)
    gu, w2 = _gateup(h, ffn_w13, ffn_w2, 1, rows=m_lat, tm=2 * tm_lat, tn=tn_gate)
    y = _matmul(gu, w2, rows=m_lat, n_out=d, out_dtype=F32, tm=tm_lat // 2, tn=tn_down, name="ffn_down")
    xl = _resid_final_piped(y, xl, norm_g[1], mod[1], rows=m_lat, gt_row=5, gpost_row=3, **seg_kw)
    return xl.reshape(n_batch, seq, d)
```
